```python
import jax, jax.numpy as jnp
from jax import lax
import numpy as np

D_MODEL = 1024
BATCH = 4
SEQ = 4096
DEPTH = 1
DEC_BATCH = 128
DEC_SEQ = 4
PAST_LEN = 2048
PAGE_SIZE = 128

D_LRU = D_MODEL // 2
LRU_BLOCKS = 8
LRU_BLOCK = D_LRU // LRU_BLOCKS
CONV_W = 4
LRU_C = 8.0
N_HEADS = 8
HEAD_DIM = (D_MODEL - D_LRU) // N_HEADS
D_ATT = N_HEADS * HEAD_DIM
D_MIX = D_LRU + D_ATT
IDX_HEADS = 8
IDX_DIM = 64
TOPK_MAX = 256
ROPE_DIM = HEAD_DIM // 4
ROPE_THETA = 500000.0
D_FF = -(-8 * D_MODEL // (3 * 256)) * 256
Q_BLOCK = 128
ALPHA = (2.0 * DEPTH) ** 0.25
BETA = (8.0 * DEPTH) ** -0.25
D_IN = 2 * D_LRU + 3 * D_ATT + IDX_HEADS * IDX_DIM + IDX_DIM + IDX_HEADS

kernel_name = "hymba_rglru_dsa_decoder_step"


def split_points():
    widths = [D_LRU, D_LRU, D_ATT, D_ATT, D_ATT, IDX_HEADS * IDX_DIM, IDX_DIM]
    return [int(v) for v in np.cumsum(widths)]


def layer_norm(x, g, b, eps=1e-5):
    xf = x.astype(jnp.float32)
    mu = jnp.mean(xf, -1, keepdims=True)
    var = jnp.mean(jnp.square(xf - mu), -1, keepdims=True)
    return ((xf - mu) * lax.rsqrt(var + eps) * g + b).astype(x.dtype)


def rms_norm(x, g, eps=1e-6):
    xf = x.astype(jnp.float32)
    return (xf * lax.rsqrt(jnp.mean(jnp.square(xf), -1, keepdims=True) + eps) * g).astype(x.dtype)


def rope(x, pos):
    half = ROPE_DIM // 2
    freqs = ROPE_THETA ** (-jnp.arange(half, dtype=jnp.float32) / half)
    ang = pos.astype(jnp.float32)[:, None] * freqs[None, :]
    cos, sin = jnp.cos(ang)[:, None, :], jnp.sin(ang)[:, None, :]
    x1 = x[..., :half].astype(jnp.float32)
    x2 = x[..., half:ROPE_DIM].astype(jnp.float32)
    rot = jnp.concatenate([x1 * cos - x2 * sin, x2 * cos + x1 * sin], -1).astype(x.dtype)
    return jnp.concatenate([rot, x[..., ROPE_DIM:]], -1)


def gather_rows(rows, idx):
    return jax.vmap(lambda r, i: r[i])(rows, idx)


def input_streams(x, w_in, pos):
    B, T = x.shape[0], x.shape[1]
    z = jnp.einsum('btd,de->bte', x, w_in)
    xl, gate, q, k, v, qi, ki, wi = jnp.split(z, split_points(), axis=-1)
    q = rope(q.reshape(B, T, N_HEADS, HEAD_DIM), pos)
    k = rope(k.reshape(B, T, N_HEADS, HEAD_DIM), pos)
    v = v.reshape(B, T, N_HEADS, HEAD_DIM)
    qi = rope(qi.reshape(B, T, IDX_HEADS, IDX_DIM), pos)
    ki = rope(ki[:, :, None, :], pos)[:, :, 0]
    return xl, gate, q, k, v, qi, ki, wi


def rg_lru_branch(xl, gate, conv_prev, h0, conv_w, conv_b, w_a, b_a, w_x, b_x, lam):
    B, T = xl.shape[0], xl.shape[1]
    xp = jnp.concatenate([conv_prev.astype(xl.dtype), xl], axis=1)
    xc = conv_b + sum(xp[:, j:j + T] * conv_w[j] for j in range(CONV_W))
    xb = xc.reshape(B, T, LRU_BLOCKS, LRU_BLOCK)
    r = jax.nn.sigmoid(jnp.einsum('btnd,nde->btne', xb, w_a).reshape(B, T, D_LRU) + b_a)
    i = jax.nn.sigmoid(jnp.einsum('btnd,nde->btne', xb, w_x).reshape(B, T, D_LRU) + b_x)
    log_a = (-LRU_C * r.astype(jnp.float32) * jax.nn.softplus(-lam.astype(jnp.float32)))
    a = jnp.exp(log_a)
    b = jnp.sqrt(-jnp.expm1(2.0 * log_a)) * (i * xc).astype(jnp.float32)
    b = b.at[:, 0].add(a[:, 0] * h0.astype(jnp.float32))

    def combine(e1, e2):
        a1, b1 = e1
        a2, b2 = e2
        return a1 * a2, a2 * b1 + b2

    _, h = lax.associative_scan(combine, (a, b), axis=1)
    y = h.astype(xl.dtype) * jax.nn.gelu(gate)
    return y, xp[:, -(CONV_W - 1):], h[:, -1]


def indexer_scores(qi, wi, ki_all):
    dots = jax.nn.relu(jnp.einsum('bthd,bsd->bths', qi, ki_all).astype(jnp.float32))
    return jnp.einsum('bths,bth->bts', dots, wi.astype(jnp.float32))


def sparse_attend(q, k_sel, v_sel, valid):
    logits = jnp.einsum('bthd,btkhd->bthk', q, k_sel).astype(jnp.float32) * (HEAD_DIM ** -0.5)
    logits = jnp.where(valid[:, :, None, :], logits, -jnp.inf)
    p = jax.nn.softmax(logits, axis=-1).astype(v_sel.dtype)
    o = jnp.einsum('bthk,btkhd->bthd', p, v_sel)
    return o.reshape(o.shape[0], o.shape[1], D_ATT)


def finish_layer(x, y_lru, y_att, gn_lru, gn_att, w_out, ln1_g, ln1_b, w_ffn_in, w_ffn_out, ln2_g, ln2_b):
    mix = jnp.concatenate([rms_norm(y_lru, gn_lru), rms_norm(y_att, gn_att)], -1)
    y = jnp.einsum('bte,ed->btd', mix, w_out)
    x1 = layer_norm(ALPHA * x + y, ln1_g, ln1_b)
    u, g = jnp.split(jnp.einsum('btd,df->btf', x1, w_ffn_in), 2, axis=-1)
    f = jnp.einsum('btf,fd->btd', jax.nn.silu(g) * u, w_ffn_out)
    return layer_norm(ALPHA * x1 + f, ln2_g, ln2_b)


def prompt_layer(x, lru_w, fin_w, w_in):
    B = x.shape[0]
    pos = jnp.arange(SEQ)
    xl, gate, q, k, v, qi, ki, wi = input_streams(x, w_in, pos)
    y_lru, conv_new, h_new = rg_lru_branch(
        xl, gate, jnp.zeros((B, CONV_W - 1, D_LRU), x.dtype), jnp.zeros((B, D_LRU), jnp.float32), *lru_w)
    topk = min(TOPK_MAX, SEQ // 4)
    key_pos = jnp.arange(SEQ)

    def block(bi):
        s0 = bi * Q_BLOCK
        q_b = lax.dynamic_slice_in_dim(q, s0, Q_BLOCK, axis=1)
        qi_b = lax.dynamic_slice_in_dim(qi, s0, Q_BLOCK, axis=1)
        wi_b = lax.dynamic_slice_in_dim(wi, s0, Q_BLOCK, axis=1)
        qpos = s0 + jnp.arange(Q_BLOCK)
        scores = indexer_scores(qi_b, wi_b, ki)
        scores = jnp.where((key_pos[None, :] <= qpos[:, None])[None], scores, -jnp.inf)
        _, idx = lax.top_k(scores, topk)
        valid = idx <= qpos[None, :, None]
        return sparse_attend(q_b, gather_rows(k, idx), gather_rows(v, idx), valid)

    y_att = lax.map(block, jnp.arange(SEQ // Q_BLOCK))
    y_att = jnp.moveaxis(y_att, 0, 1).reshape(B, SEQ, D_ATT)
    y = finish_layer(x, y_lru, y_att, *fin_w)
    return y, k, v, ki, conv_new, h_new


def sample_layer(x, cache_k, cache_v, cache_kidx, conv_prev, h_prev, page_table, lru_w, fin_w, w_in):
    DB = x.shape[0]
    pos = PAST_LEN + jnp.arange(DEC_SEQ)
    xl, gate, q, k, v, qi, ki, wi = input_streams(x, w_in, pos)
    y_lru, conv_new, h_new = rg_lru_branch(xl, gate, conv_prev, h_prev, *lru_w)
    n_keys = PAST_LEN + DEC_SEQ
    topk = min(TOPK_MAX, n_keys // 4)
    ki_past = cache_kidx[page_table].reshape(DB, PAST_LEN, IDX_DIM)
    ki_all = jnp.concatenate([ki_past.astype(ki.dtype), ki], axis=1)
    scores = indexer_scores(qi, wi, ki_all)
    scores = jnp.where((jnp.arange(n_keys)[None, :] <= pos[:, None])[None], scores, -jnp.inf)
    _, idx = lax.top_k(scores, topk)
    valid = idx <= pos[None, :, None]
    in_past = idx < PAST_LEN
    pidx = jnp.minimum(idx, PAST_LEN - 1)
    phys = gather_rows(page_table, pidx // PAGE_SIZE)
    slot = pidx % PAGE_SIZE
    nidx = jnp.clip(idx - PAST_LEN, 0, DEC_SEQ - 1)
    k_sel = jnp.where(in_past[..., None, None], cache_k[phys, slot].astype(k.dtype), gather_rows(k, nidx))
    v_sel = jnp.where(in_past[..., None, None], cache_v[phys, slot].astype(v.dtype), gather_rows(v, nidx))
    y_att = sparse_attend(q, k_sel, v_sel, valid)
    y = finish_layer(x, y_lru, y_att, *fin_w)
    return y, k, v, ki, conv_new, h_new


def setup_inputs(seed: int = 0) -> dict:
    key = jax.random.key(seed)
    ks = jax.random.split(key, 32)
    f32 = jnp.float32
    n_pages = PAST_LEN // PAGE_SIZE
    n_used = DEC_BATCH * n_pages
    n_phys = n_used + max(1, n_used // 4)
    nrm = lambda k, shape, s: jax.random.normal(k, shape, f32) * s
    w_in = nrm(ks[0], (DEPTH, D_MODEL, D_IN), D_MODEL ** -0.5)
    v_lo = 2 * D_LRU + 2 * D_ATT
    col_scale = jnp.ones((D_IN,), f32).at[v_lo:v_lo + D_ATT].set(BETA)
    w_in = w_in * col_scale
    a8 = jax.random.uniform(ks[1], (DEPTH, D_LRU), f32, 0.9, 0.999)
    a_base = a8 ** (1.0 / LRU_C)
    lam = jnp.log(a_base) - jnp.log1p(-a_base)
    page_table = jax.random.permutation(ks[2], n_phys)[:n_used].reshape(DEC_BATCH, n_pages).astype(jnp.int32)
    return {
        "x_prompt": nrm(ks[3], (BATCH, SEQ, D_MODEL), 1.0),
        "x_sample": nrm(ks[4], (DEC_BATCH, DEC_SEQ, D_MODEL), 1.0),
        "cache_k": nrm(ks[5], (DEPTH, n_phys, PAGE_SIZE, N_HEADS, HEAD_DIM), 1.0),
        "cache_v": nrm(ks[6], (DEPTH, n_phys, PAGE_SIZE, N_HEADS, HEAD_DIM), BETA),
        "cache_kidx": nrm(ks[7], (DEPTH, n_phys, PAGE_SIZE, IDX_DIM), 1.0),
        "state_conv": nrm(ks[8], (DEPTH, DEC_BATCH, CONV_W - 1, D_LRU), 1.0),
        "state_h": nrm(ks[9], (DEPTH, DEC_BATCH, D_LRU), 0.5),
        "page_table": page_table,
        "w_in": w_in,
        "conv_w": nrm(ks[10], (DEPTH, CONV_W, D_LRU), CONV_W ** -0.5),
        "conv_b": nrm(ks[11], (DEPTH, D_LRU), 0.02),
        "w_a": nrm(ks[12], (DEPTH, LRU_BLOCKS, LRU_BLOCK, LRU_BLOCK), LRU_BLOCK ** -0.5),
        "b_a": nrm(ks[13], (DEPTH, D_LRU), 0.02),
        "w_x": nrm(ks[14], (DEPTH, LRU_BLOCKS, LRU_BLOCK, LRU_BLOCK), LRU_BLOCK ** -0.5),
        "b_x": nrm(ks[15], (DEPTH, D_LRU), 0.02),
        "lam": lam,
        "gn_lru": 1.0 + nrm(ks[16], (DEPTH, D_LRU), 0.02),
        "gn_att": 1.0 + nrm(ks[17], (DEPTH, D_ATT), 0.02),
        "w_out": nrm(ks[18], (DEPTH, D_MIX, D_MODEL), D_MIX ** -0.5 * BETA),
        "ln1_g": 1.0 + nrm(ks[19], (DEPTH, D_MODEL), 0.02),
        "ln1_b": nrm(ks[20], (DEPTH, D_MODEL), 0.02),
        "w_ffn_in": nrm(ks[21], (DEPTH, D_MODEL, 2 * D_FF), D_MODEL ** -0.5 * BETA),
        "w_ffn_out": nrm(ks[22], (DEPTH, D_FF, D_MODEL), D_FF ** -0.5 * BETA),
        "ln2_g": 1.0 + nrm(ks[23], (DEPTH, D_MODEL), 0.02),
        "ln2_b": nrm(ks[24], (DEPTH, D_MODEL), 0.02),
    }


def reference(x_prompt, x_sample, cache_k, cache_v, cache_kidx, state_conv, state_h, page_table,
              w_in, conv_w, conv_b, w_a, b_a, w_x, b_x, lam, gn_lru, gn_att, w_out,
              ln1_g, ln1_b, w_ffn_in, w_ffn_out, ln2_g, ln2_b):
    kp, vp, kip, cp, hp = [], [], [], [], []
    ks_, vs_, kis, cs, hs = [], [], [], [], []
    xp, xs = x_prompt, x_sample
    for l in range(DEPTH):
        lru_w = (conv_w[l], conv_b[l], w_a[l], b_a[l], w_x[l], b_x[l], lam[l])
        fin_w = (gn_lru[l], gn_att[l], w_out[l], ln1_g[l], ln1_b[l], w_ffn_in[l], w_ffn_out[l], ln2_g[l], ln2_b[l])
        xp, k1, v1, ki1, c1, h1 = prompt_layer(xp, lru_w, fin_w, w_in[l])
        xs, k2, v2, ki2, c2, h2 = sample_layer(xs, cache_k[l], cache_v[l], cache_kidx[l], state_conv[l],
                                               state_h[l], page_table, lru_w, fin_w, w_in[l])
        kp.append(k1); vp.append(v1); kip.append(ki1); cp.append(c1); hp.append(h1)
        ks_.append(k2); vs_.append(v2); kis.append(ki2); cs.append(c2); hs.append(h2)
    y_prompt, y_sample = xp, xs
    k_prompt, v_prompt, kidx_prompt = jnp.stack(kp), jnp.stack(vp), jnp.stack(kip)
    conv_prompt, h_prompt = jnp.stack(cp), jnp.stack(hp)
    k_sample, v_sample, kidx_sample = jnp.stack(ks_), jnp.stack(vs_), jnp.stack(kis)
    conv_sample, h_sample = jnp.stack(cs), jnp.stack(hs)
    return (y_prompt, y_sample, k_prompt, v_prompt, kidx_prompt, conv_prompt, h_prompt,
            k_sample, v_sample, kidx_sample, conv_sample, h_sample)
```

```python
import functools

import jax
import jax.numpy as jnp
import numpy as np
from jax import lax
from jax.experimental import pallas as pl
from jax.experimental.pallas import tpu as pltpu

CONV_W = 4
LRU_C = 8.0
LRU_BLOCKS = 8
IDX_HEADS = 8
TOPK_MAX = 256
ROPE_FRACTION = 4
ROPE_THETA = 500000.0
RMS_EPS = 1e-6
LN_EPS = 1e-5

LANES = 128
SUBLANES = 8
VMEM_LIMIT = 56 * 1024 * 1024
MASKED = -1e30
INT_MIN = -2 ** 31

F32 = jnp.float32
BF16 = jnp.bfloat16
I32 = jnp.int32


def _params(*sem):
    return pltpu.CompilerParams(dimension_semantics=sem, vmem_limit_bytes=VMEM_LIMIT)


def _nt_dot(a, b):
    return lax.dot_general(a, b, (((1,), (1,)), ((), ())), preferred_element_type=F32)


def _proj_body(x_ref, w_ref, c_ref, s1_ref, s2_ref,
               xl_ref, gate_ref, qb_ref, k_ref, kb_ref, v_ref, vb_ref, qib_ref, tail_ref,
               *, d_lru, d_att, d_qi, idx_dim, rope_half, q_scale):
    xb = x_ref[...].astype(BF16)
    c, s1, s2 = c_ref[...], s1_ref[...], s2_ref[...]

    def proj(lo, width):
        return jnp.dot(xb, w_ref[:, lo:lo + width], preferred_element_type=F32)

    def tiled(t, width):
        reps = width // LANES
        return t if reps == 1 else jnp.concatenate([t] * reps, axis=1)

    def rope(z, cc, ss1, ss2):
        width = z.shape[1]
        return (z * cc + pltpu.roll(z, width - rope_half, 1) * ss1
                + pltpu.roll(z, rope_half, 1) * ss2)

    lo = 0
    xl_ref[...] = proj(lo, d_lru); lo += d_lru
    gate_ref[...] = proj(lo, d_lru); lo += d_lru
    q = rope(proj(lo, d_att), tiled(c, d_att), tiled(s1, d_att), tiled(s2, d_att)); lo += d_att
    qb_ref[...] = (q * q_scale).astype(BF16)
    k = rope(proj(lo, d_att), tiled(c, d_att), tiled(s1, d_att), tiled(s2, d_att)); lo += d_att
    k_ref[...] = k
    kb_ref[...] = k.astype(BF16)
    v = proj(lo, d_att); lo += d_att
    v_ref[...] = v
    vb_ref[...] = v.astype(BF16)
    qi = rope(proj(lo, d_qi), tiled(c, d_qi), tiled(s1, d_qi), tiled(s2, d_qi)); lo += d_qi
    qib_ref[...] = qi.astype(BF16)
    tail = proj(lo, LANES)
    is_key = lax.broadcasted_iota(I32, tail.shape, 1) < idx_dim
    tail_ref[...] = rope(tail, jnp.where(is_key, c, 1.0), jnp.where(is_key, s1, 0.0),
                         jnp.where(is_key, s2, 0.0))


def _rope_tables(pos, head_dim):
    rope_dim = head_dim // ROPE_FRACTION
    half = rope_dim // 2
    freqs = ROPE_THETA ** (-jnp.arange(half, dtype=F32) / half)
    ang = pos.astype(F32)[:, None] * freqs[None, :]
    cos, sin = jnp.cos(ang), jnp.sin(ang)
    n = pos.shape[0]
    rest = head_dim - rope_dim
    c = jnp.concatenate([cos, cos, jnp.ones((n, rest), F32)], 1)
    s1 = jnp.concatenate([-sin, jnp.zeros((n, half + rest), F32)], 1)
    s2 = jnp.concatenate([jnp.zeros((n, half), F32), sin, jnp.zeros((n, rest), F32)], 1)
    reps = LANES // head_dim
    return [jnp.tile(t, (1, reps)) for t in (c, s1, s2)], half


def _project(x2d, w_pad, pos, *, d_lru, d_att, d_qi, idx_dim, head_dim, tm):
    n, d_model = x2d.shape
    (c, s1, s2), half = _rope_tables(pos, head_dim)
    row = lambda w: pl.BlockSpec((tm, w), lambda i: (i, 0))
    outs = [
        (d_lru, F32), (d_lru, F32), (d_att, BF16), (d_att, F32), (d_att, BF16),
        (d_att, F32), (d_att, BF16), (d_qi, BF16), (LANES, F32)]
    body = functools.partial(_proj_body, d_lru=d_lru, d_att=d_att, d_qi=d_qi, idx_dim=idx_dim,
                             rope_half=half, q_scale=head_dim ** -0.5)
    return pl.pallas_call(
        body,
        grid=(n // tm,),
        in_specs=[row(d_model), pl.BlockSpec(w_pad.shape, lambda i: (0, 0)),
                  row(LANES), row(LANES), row(LANES)],
        out_specs=[row(w) for w, _ in outs],
        out_shape=[jax.ShapeDtypeStruct((n, w), dt) for w, dt in outs],
        compiler_params=_params("parallel"),
        name="proj",
    )(x2d, w_pad, c, s1, s2)


def _softplus(x):
    return jnp.maximum(x, 0.0) + jnp.log1p(jnp.exp(-jnp.abs(x)))


def _gelu_tanh(x):
    return 0.5 * x * (1.0 + jnp.tanh(np.sqrt(2.0 / np.pi).astype(np.float32)
                                     * (x + 0.044715 * (x * x * x))))


def _lru_gates(xc, wa_ref, wx_ref, b_a, b_x, lam):
    xcb = xc.astype(BF16)
    r = jax.nn.sigmoid(jnp.dot(xcb, wa_ref[...], preferred_element_type=F32) + b_a)
    i = jax.nn.sigmoid(jnp.dot(xcb, wx_ref[...], preferred_element_type=F32) + b_x)
    log_a = -LRU_C * r * _softplus(-lam)
    a = jnp.exp(log_a)
    t = jnp.tanh(log_a)
    b = jnp.sqrt(-2.0 * t / (1.0 - t)) * (i * xc)
    return a, b


def _rms_gain(y, g):
    return y * lax.rsqrt(jnp.mean(y * y, axis=-1, keepdims=True) + RMS_EPS) * g


def _lru_prompt_body(xl_ref, gate_ref, cprev_ref, h0_ref, cw_ref, p_ref, wa_ref, wx_ref,
                     mix_ref, hlast_ref, ext_ref, hc_ref, *, tt):
    j = pl.program_id(1)

    @pl.when(j == 0)
    def _():
        ext_ref[0:SUBLANES, :] = cprev_ref[0]
        hc_ref[0:1, :] = h0_ref[0]

    xl = xl_ref[0]
    ext_ref[SUBLANES:SUBLANES + tt, :] = xl
    conv_b, b_a, b_x, lam, gn = (p_ref[r:r + 1, :] for r in range(5))
    xc = conv_b + (cw_ref[0:1, :] * ext_ref[SUBLANES - 3:SUBLANES - 3 + tt, :]
                   + cw_ref[1:2, :] * ext_ref[SUBLANES - 2:SUBLANES - 2 + tt, :]
                   + cw_ref[2:3, :] * ext_ref[SUBLANES - 1:SUBLANES - 1 + tt, :]
                   + cw_ref[3:4, :] * xl)
    ext_ref[0:SUBLANES, :] = ext_ref[tt:tt + SUBLANES, :]

    a, b = _lru_gates(xc, wa_ref, wx_ref, b_a, b_x, lam)
    row = lax.broadcasted_iota(I32, a.shape, 0)
    d = 1
    while d < tt:
        keep = row >= d
        a_prev = jnp.where(keep, pltpu.roll(a, d, 0), 1.0)
        b_prev = jnp.where(keep, pltpu.roll(b, d, 0), 0.0)
        b = a * b_prev + b
        a = a * a_prev
        d *= 2
    h = a * hc_ref[0:1, :] + b
    hc_ref[0:1, :] = h[tt - 1:tt, :]
    hlast_ref[0] = h[tt - 1:tt, :]
    mix_ref[0] = _rms_gain(h * _gelu_tanh(gate_ref[0]), gn).astype(BF16)


def _lru_prompt(xl, gate, conv_prev, h0, conv_w, pvec, wa_bd, wx_bd, *, tt):
    bsz, t, d = xl.shape
    cprev8 = jnp.concatenate(
        [jnp.zeros((bsz, SUBLANES - (CONV_W - 1), d), F32), conv_prev.astype(F32)], axis=1)
    const = lambda shape: pl.BlockSpec(shape, lambda b, j: (0,) * len(shape))
    return pl.pallas_call(
        functools.partial(_lru_prompt_body, tt=tt),
        grid=(bsz, t // tt),
        in_specs=[pl.BlockSpec((1, tt, d), lambda b, j: (b, j, 0)),
                  pl.BlockSpec((1, tt, d), lambda b, j: (b, j, 0)),
                  pl.BlockSpec((1, SUBLANES, d), lambda b, j: (b, 0, 0)),
                  pl.BlockSpec((1, 1, d), lambda b, j: (b, 0, 0)),
                  const(conv_w.shape), const(pvec.shape), const(wa_bd.shape), const(wx_bd.shape)],
        out_specs=[pl.BlockSpec((1, tt, d), lambda b, j: (b, j, 0)),
                   pl.BlockSpec((1, 1, d), lambda b, j: (b, 0, 0))],
        out_shape=[jax.ShapeDtypeStruct((bsz, t, d), BF16),
                   jax.ShapeDtypeStruct((bsz, 1, d), F32)],
        scratch_shapes=[pltpu.VMEM((tt + SUBLANES, d), F32), pltpu.VMEM((SUBLANES, d), F32)],
        compiler_params=_params("parallel", "arbitrary"),
        name="lru_prompt",
    )(xl, gate, cprev8, h0.astype(F32)[:, None, :], conv_w, pvec, wa_bd, wx_bd)


def _lru_sample_body(xl_ref, gate_ref, cprev_ref, h0_ref, cw_ref, p_ref, wa_ref, wx_ref,
                     mix_ref, hlast_ref, *, t_len):
    conv_b, b_a, b_x, lam, gn = (p_ref[r:r + 1, :] for r in range(5))
    xp = [cprev_ref[s] for s in range(CONV_W - 1)] + [xl_ref[s] for s in range(t_len)]
    h = h0_ref[...]
    for s in range(t_len):
        xc = conv_b + (cw_ref[0:1, :] * xp[s] + cw_ref[1:2, :] * xp[s + 1]
                       + cw_ref[2:3, :] * xp[s + 2] + cw_ref[3:4, :] * xp[s + 3])
        a, b = _lru_gates(xc, wa_ref, wx_ref, b_a, b_x, lam)
        h = a * h + b
        mix_ref[s] = _rms_gain(h * _gelu_tanh(gate_ref[s]), gn).astype(BF16)
    hlast_ref[...] = h


def _lru_sample(xl_t, gate_t, cprev_t, h0, conv_w, pvec, wa_bd, wx_bd):
    t_len, dbs, d = xl_t.shape
    return pl.pallas_call(
        functools.partial(_lru_sample_body, t_len=t_len),
        out_shape=[jax.ShapeDtypeStruct((t_len, dbs, d), BF16),
                   jax.ShapeDtypeStruct((dbs, d), F32)],
        compiler_params=pltpu.CompilerParams(vmem_limit_bytes=VMEM_LIMIT),
        name="lru_sample",
    )(xl_t, gate_t, cprev_t, h0.astype(F32), conv_w, pvec, wa_bd, wx_bd)


def _sortable_key(score):
    bits = pltpu.bitcast(score, I32)
    key = jnp.where(bits >= 0, bits, bits ^ 0x7FFFFFFF)
    return jnp.where(bits == INT_MIN, 0, key)


def _select_topk(key_ref, nc, topk, idx_bits):
    _, rows, tk = key_ref.shape
    lane = lax.broadcasted_iota(I32, (rows, tk), 1)

    def count(pred):
        def body(c, acc):
            hit = jnp.where(pred(c, key_ref[c]), 1, 0).astype(I32)
            for g in range(tk // LANES):
                acc = acc + hit[:, g * LANES:(g + 1) * LANES]
            return acc
        acc = lax.fori_loop(0, nc, body, jnp.zeros((rows, LANES), I32))
        return jnp.sum(acc, axis=1, keepdims=True)

    def value_step(it, base):
        trial = base ^ lax.shift_left(jnp.int32(1), (31 - it).astype(I32))
        return jnp.where(count(lambda c, k: k >= trial) >= topk, trial, base)

    theta = lax.fori_loop(0, 32, value_step, jnp.full((rows, 1), INT_MIN, I32))
    need = topk - count(lambda c, k: k > theta)

    def index_step(it, pos):
        trial = pos + lax.shift_left(jnp.int32(1), (idx_bits - 1 - it).astype(I32))
        below = count(lambda c, k: (k == theta) & (c * tk + lane < trial))
        return jnp.where(below < need, trial, pos)

    pos = lax.fori_loop(0, idx_bits, index_step, jnp.zeros((rows, 1), I32))
    return theta, pos


def _selected(key, idx, theta, pos):
    return (key > theta) | ((key == theta) & (idx <= pos))


def _attn_prompt_body(qi_ref, tail_ref, q_ref, kidx_ref, k_ref, v_ref, gn_ref, out_ref,
                      key_ref, bias_ref, y_ref, *, tq, tk, topk, idx_bits, n_heads, head_dim,
                      idx_heads, idx_dim):
    i = pl.program_id(1)
    nc = lax.div((i + 1) * tq + (tk - 1), tk)
    qpos = i * tq + lax.broadcasted_iota(I32, (tq, 1), 0)
    lane = lax.broadcasted_iota(I32, (tq, tk), 1)
    qi = qi_ref[0]
    w = tail_ref[0]

    def score_chunk(c, carry):
        kc = kidx_ref[0, pl.ds(pl.multiple_of(c * tk, tk), tk), :]
        acc = jnp.zeros((tq, tk), F32)
        for h in range(idx_heads):
            d = _nt_dot(qi[:, h * idx_dim:(h + 1) * idx_dim], kc)
            acc = acc + jnp.maximum(d, 0.0) * w[:, idx_dim + h:idx_dim + h + 1]
        score = jnp.where(c * tk + lane <= qpos, acc, -jnp.inf)
        key_ref[c] = _sortable_key(score)
        return carry

    lax.fori_loop(0, nc, score_chunk, 0)
    theta, pos = _select_topk(key_ref, nc, topk, idx_bits)

    def bias_chunk(c, carry):
        idx = c * tk + lane
        sel = _selected(key_ref[c], idx, theta, pos) & (idx <= qpos)
        bias_ref[c] = jnp.where(sel, 0.0, MASKED)
        return carry

    lax.fori_loop(0, nc, bias_chunk, 0)

    for h in range(n_heads):
        hs = slice(h * head_dim, (h + 1) * head_dim)
        qh = q_ref[0, :, hs]

        def kv_chunk(c, carry, hs=hs, qh=qh):
            m, l, acc = carry
            rows = pl.ds(pl.multiple_of(c * tk, tk), tk)
            s = _nt_dot(qh, k_ref[0, rows, hs]) + bias_ref[c]
            m_new = jnp.maximum(m, jnp.max(s, axis=1, keepdims=True))
            alpha = jnp.exp(m - m_new)
            p = jnp.exp(s - m_new)
            l = alpha * l + jnp.sum(p, axis=1, keepdims=True)
            acc = alpha * acc + jnp.dot(p.astype(BF16), v_ref[0, rows, hs],
                                        preferred_element_type=F32)
            return m_new, l, acc

        init = (jnp.full((tq, 1), -jnp.inf, F32), jnp.zeros((tq, 1), F32),
                jnp.zeros((tq, head_dim), F32))
        _, l, acc = lax.fori_loop(0, nc, kv_chunk, init)
        y_ref[:, hs] = acc / l
    out_ref[0] = _rms_gain(y_ref[...], gn_ref[...]).astype(BF16)


def _attn_prompt(qib, tail, qb, kidxb, kb, vb, gn_att, *, tq, tk, topk, n_heads, idx_dim):
    bsz, t, d_att = qb.shape
    nc_max = -(-t // tk)
    idx_bits = max(1, int(nc_max * tk - 1).bit_length())
    blk = lambda w: pl.BlockSpec((1, tq, w), lambda b, i: (b, i, 0))
    full = lambda w: pl.BlockSpec((1, t, w), lambda b, i: (b, 0, 0))
    body = functools.partial(
        _attn_prompt_body, tq=tq, tk=tk, topk=topk, idx_bits=idx_bits, n_heads=n_heads,
        head_dim=d_att // n_heads, idx_heads=IDX_HEADS, idx_dim=idx_dim)
    return pl.pallas_call(
        body,
        grid=(bsz, t // tq),
        in_specs=[blk(qib.shape[2]), blk(LANES), blk(d_att), full(idx_dim), full(d_att),
                  full(d_att), pl.BlockSpec((1, d_att), lambda b, i: (0, 0))],
        out_specs=blk(d_att),
        out_shape=jax.ShapeDtypeStruct((bsz, t, d_att), BF16),
        scratch_shapes=[pltpu.VMEM((nc_max, tq, tk), I32), pltpu.VMEM((nc_max, tq, tk), F32),
                        pltpu.VMEM((tq, d_att), F32)],
        compiler_params=_params("parallel", "arbitrary"),
        name="attn_prompt",
    )(qib, tail, qb, kidxb, kb, vb, gn_att)


def _score_sample_body(pt_ref, qi_ref, w_ref, *refs, n_pages, page, t_len, idx_heads, past_len):
    del pt_ref
    page_refs, new_ref, out_ref = refs[:n_pages], refs[n_pages], refs[n_pages + 1]
    qi = qi_ref[0]
    w = w_ref[0]
    for p in range(n_pages + 1):
        kc = page_refs[p][0].astype(BF16) if p < n_pages else new_ref[0]
        d = jnp.maximum(_nt_dot(qi, kc), 0.0) * w
        s = jnp.sum(d.reshape(t_len, idx_heads, page), axis=1)
        if p == n_pages:
            tpos = lax.broadcasted_iota(I32, (t_len, page), 0)
            kpos = lax.broadcasted_iota(I32, (t_len, page), 1)
            s = jnp.where(kpos <= tpos, s, -jnp.inf)
        out_ref[0, :, p * page:(p + 1) * page] = s


def _select_sample_body(s_ref, bias_ref, key_ref, *, topk, idx_bits, page):
    nc = key_ref.shape[0]
    for c in range(nc):
        key_ref[c] = _sortable_key(s_ref[:, c * page:(c + 1) * page])
    theta, pos = _select_topk(key_ref, nc, topk, idx_bits)
    lane = lax.broadcasted_iota(I32, key_ref.shape[1:], 1)
    for c in range(nc):
        s = s_ref[:, c * page:(c + 1) * page]
        sel = _selected(key_ref[c], c * page + lane, theta, pos) & (s > -jnp.inf)
        bias_ref[:, c * page:(c + 1) * page] = jnp.where(sel, 0.0, MASKED)


def _attn_sample_body(pt_ref, q_ref, bias_ref, gn_ref, *refs, n_pages, page, t_len, n_heads,
                      head_dim):
    del pt_ref
    k_refs, v_refs = refs[:n_pages], refs[n_pages:2 * n_pages]
    knew_ref, vnew_ref, out_ref, logit_ref = refs[2 * n_pages:]
    d_att = n_heads * head_dim
    q = q_ref[0].astype(F32)
    head_of_lane = lax.broadcasted_iota(I32, (n_heads, d_att), 1) // head_dim
    own = head_of_lane == lax.broadcasted_iota(I32, (n_heads, d_att), 0)
    qbd = jnp.concatenate(
        [jnp.where(own, jnp.broadcast_to(q[t:t + 1, :], (n_heads, d_att)), 0.0)
         for t in range(t_len)], axis=0).astype(BF16)
    for p in range(n_pages + 1):
        kc = k_refs[p][0].astype(BF16) if p < n_pages else knew_ref[0]
        logit_ref[:, p * page:(p + 1) * page] = _nt_dot(qbd, kc)
    outs = []
    for t in range(t_len):
        rows = slice(t * n_heads, (t + 1) * n_heads)
        s = logit_ref[rows, :] + bias_ref[0, t:t + 1, :]
        m = jnp.max(s, axis=1, keepdims=True)
        logit_ref[rows, :] = jnp.exp(s - m)
    pb = logit_ref[...]
    l = jnp.sum(pb, axis=1, keepdims=True)
    acc = jnp.zeros((t_len * n_heads, d_att), F32)
    for p in range(n_pages + 1):
        vc = v_refs[p][0].astype(BF16) if p < n_pages else vnew_ref[0]
        acc = acc + jnp.dot(pb[:, p * page:(p + 1) * page].astype(BF16), vc,
                            preferred_element_type=F32)
    acc = acc / l
    for t in range(t_len):
        o = jnp.where(own, acc[t * n_heads:(t + 1) * n_heads, :], 0.0)
        outs.append(jnp.sum(o, axis=0, keepdims=True))
    y = jnp.concatenate(outs, axis=0)
    out_ref[0] = _rms_gain(y, gn_ref[...]).astype(BF16)


def _attn_sample(qib, wi, qb, kidx_new, k_new, v_new, cache_k, cache_v, cache_kidx, page_table,
                 gn_att, *, topk, n_heads):
    dbs, t_len, d_att = qb.shape
    n_pages = page_table.shape[1]
    page = cache_k.shape[1]
    idx_dim = cache_kidx.shape[2]
    idx_heads = wi.shape[2]
    nkp = (n_pages + 1) * page
    past_len = n_pages * page
    pad = lambda a: jnp.pad(a, ((0, 0), (0, page - t_len), (0, 0)))

    def paged(width):
        return [pl.BlockSpec((1, page, width), functools.partial(
            lambda b, pt, p: (pt[b, p], 0, 0), p=p)) for p in range(n_pages)]

    per_seq = lambda shape: pl.BlockSpec((1,) + shape, lambda b, pt: (b, 0, 0))

    scores = pl.pallas_call(
        functools.partial(_score_sample_body, n_pages=n_pages, page=page, t_len=t_len,
                          idx_heads=idx_heads, past_len=past_len),
        grid_spec=pltpu.PrefetchScalarGridSpec(
            num_scalar_prefetch=1, grid=(dbs,),
            in_specs=[per_seq((t_len * idx_heads, idx_dim)), per_seq((t_len * idx_heads, 1))]
                     + paged(idx_dim) + [per_seq((page, idx_dim))],
            out_specs=per_seq((t_len, nkp))),
        out_shape=jax.ShapeDtypeStruct((dbs, t_len, nkp), F32),
        compiler_params=_params("parallel"),
        name="score_sample",
    )(page_table, qib.reshape(dbs, t_len * idx_heads, idx_dim),
      wi.reshape(dbs, t_len * idx_heads, 1), *([cache_kidx] * n_pages), pad(kidx_new))

    rows = dbs * t_len
    tr = min(rows, 128)
    bias = pl.pallas_call(
        functools.partial(_select_sample_body, topk=topk,
                          idx_bits=max(1, int(nkp - 1).bit_length()), page=page),
        grid=(rows // tr,),
        in_specs=[pl.BlockSpec((tr, nkp), lambda r: (r, 0))],
        out_specs=pl.BlockSpec((tr, nkp), lambda r: (r, 0)),
        out_shape=jax.ShapeDtypeStruct((rows, nkp), F32),
        scratch_shapes=[pltpu.VMEM((n_pages + 1, tr, page), I32)],
        compiler_params=_params("parallel"),
        name="select_sample",
    )(scores.reshape(rows, nkp)).reshape(dbs, t_len, nkp)

    return pl.pallas_call(
        functools.partial(_attn_sample_body, n_pages=n_pages, page=page, t_len=t_len,
                          n_heads=n_heads, head_dim=d_att // n_heads),
        grid_spec=pltpu.PrefetchScalarGridSpec(
            num_scalar_prefetch=1, grid=(dbs,),
            in_specs=[per_seq((t_len, d_att)), per_seq((t_len, nkp)),
                      pl.BlockSpec((1, d_att), lambda b, pt: (0, 0))]
                     + paged(d_att) + paged(d_att)
                     + [per_seq((page, d_att)), per_seq((page, d_att))],
            out_specs=per_seq((t_len, d_att)),
            scratch_shapes=[pltpu.VMEM((t_len * n_heads, nkp), F32)]),
        out_shape=jax.ShapeDtypeStruct((dbs, t_len, d_att), BF16),
        compiler_params=_params("parallel"),
        name="attn_sample",
    )(page_table, qb, bias, gn_att, *([cache_k] * n_pages), *([cache_v] * n_pages),
      pad(k_new), pad(v_new))


def _layer_norm(x, g, b):
    mu = jnp.mean(x, axis=-1, keepdims=True)
    xc = x - mu
    var = jnp.mean(xc * xc, axis=-1, keepdims=True)
    return xc * lax.rsqrt(var + LN_EPS) * g + b


def _finish_body(x_ref, ml_ref, ma_ref, wo_ref, wfi_ref, wfo_ref, p_ref, out_ref,
                 *, alpha, d_lru, d_ff, fc):
    ln1_g, ln1_b, ln2_g, ln2_b = (p_ref[r:r + 1, :] for r in range(4))
    y = (jnp.dot(ml_ref[...], wo_ref[0:d_lru, :], preferred_element_type=F32)
         + jnp.dot(ma_ref[...], wo_ref[d_lru:, :], preferred_element_type=F32))
    x1 = _layer_norm(alpha * x_ref[...] + y, ln1_g, ln1_b)
    x1b = x1.astype(BF16)
    f = jnp.zeros(x1.shape, F32)
    for c in range(d_ff // fc):
        u = jnp.dot(x1b, wfi_ref[:, c * fc:(c + 1) * fc], preferred_element_type=F32)
        g = jnp.dot(x1b, wfi_ref[:, d_ff + c * fc:d_ff + (c + 1) * fc],
                    preferred_element_type=F32)
        hidden = (g * jax.nn.sigmoid(g) * u).astype(BF16)
        f = f + jnp.dot(hidden, wfo_ref[c * fc:(c + 1) * fc, :], preferred_element_type=F32)
    out_ref[...] = _layer_norm(alpha * x1 + f, ln2_g, ln2_b)


def _finish(x2d, mix_lru, mix_att, wo, wfi, wfo, pvec, *, alpha, tm):
    n, d_model = x2d.shape
    d_lru = mix_lru.shape[1]
    d_ff = wfo.shape[0]
    fc = 2 * LANES if d_ff % (2 * LANES) == 0 else LANES
    row = lambda w: pl.BlockSpec((tm, w), lambda i: (i, 0))
    const = lambda a: pl.BlockSpec(a.shape, lambda i: (0, 0), pipeline_mode=pl.Buffered(1))
    return pl.pallas_call(
        functools.partial(_finish_body, alpha=alpha, d_lru=d_lru, d_ff=d_ff, fc=fc),
        grid=(n // tm,),
        in_specs=[row(d_model), row(d_lru), row(mix_att.shape[1]),
                  const(wo), const(wfi), const(wfo), const(pvec)],
        out_specs=row(d_model),
        out_shape=jax.ShapeDtypeStruct((n, d_model), F32),
        compiler_params=_params("parallel"),
        name="finish",
    )(x2d, mix_lru, mix_att, wo, wfi, wfo, pvec)


def _block_diag(w):
    nb, bi, bo = w.shape
    eye = jnp.eye(nb, dtype=w.dtype)
    return (w[:, :, None, :] * eye[:, None, :, None]).reshape(nb * bi, nb * bo)


def _row_tile(n, want):
    tm = min(n, want)
    while n % tm:
        tm //= 2
    return tm


def kernel(x_prompt, x_sample, cache_k, cache_v, cache_kidx, state_conv, state_h, page_table,
           w_in, conv_w, conv_b, w_a, b_a, w_x, b_x, lam, gn_lru, gn_att, w_out,
           ln1_g, ln1_b, w_ffn_in, w_ffn_out, ln2_g, ln2_b):
    depth, d_model, d_in = w_in.shape
    bsz, seq, _ = x_prompt.shape
    dbs, dseq, _ = x_sample.shape
    d_lru = conv_w.shape[2]
    n_phys, page, n_heads, head_dim = cache_k.shape[1:]
    d_att = n_heads * head_dim
    idx_dim = cache_kidx.shape[3]
    d_qi = IDX_HEADS * idx_dim
    n_pages = page_table.shape[1]
    past_len = n_pages * page
    alpha = (2.0 * depth) ** 0.25
    assert d_in == 2 * d_lru + 3 * d_att + d_qi + idx_dim + IDX_HEADS
    assert idx_dim + IDX_HEADS <= LANES and LANES % head_dim == 0 and head_dim == idx_dim
    geom = dict(d_lru=d_lru, d_att=d_att, d_qi=d_qi, idx_dim=idx_dim, head_dim=head_dim)
    d_main = d_in - idx_dim - IDX_HEADS

    pos_p = jnp.tile(jnp.arange(seq), bsz)
    pos_s = jnp.tile(past_len + jnp.arange(dseq), dbs)
    topk_p = min(TOPK_MAX, seq // 4)
    topk_s = min(TOPK_MAX, (past_len + dseq) // 4)

    xp = x_prompt.reshape(bsz * seq, d_model)
    xs = x_sample.reshape(dbs * dseq, d_model)
    outs_p, outs_s = [], []
    for l in range(depth):
        w_pad = jnp.pad(w_in[l], ((0, 0), (0, d_main + LANES - d_in))).astype(BF16)
        wa_bd = _block_diag(w_a[l]).astype(BF16)
        wx_bd = _block_diag(w_x[l]).astype(BF16)
        lru_vec = jnp.stack([conv_b[l], b_a[l], b_x[l], lam[l], gn_lru[l]]
                            + [jnp.zeros_like(lam[l])] * 3)
        fin_vec = jnp.stack([ln1_g[l], ln1_b[l], ln2_g[l], ln2_b[l]])
        wo, wfi, wfo = (w_out[l].astype(BF16), w_ffn_in[l].astype(BF16),
                        w_ffn_out[l].astype(BF16))
        gn_a = gn_att[l][None, :]

        xl, gate, qb, k, kb, v, vb, qib, tail = _project(
            xp, w_pad, pos_p, tm=_row_tile(bsz * seq, 512), **geom)
        b3 = lambda a: a.reshape(bsz, seq, a.shape[-1])
        xl3 = b3(xl)
        mix_l, h_last = _lru_prompt(
            xl3, b3(gate), jnp.zeros((bsz, CONV_W - 1, d_lru), F32), jnp.zeros((bsz, d_lru), F32),
            conv_w[l], lru_vec, wa_bd, wx_bd, tt=_row_tile(seq, 256))
        tail3 = b3(tail)
        ki = tail3[:, :, :idx_dim]
        mix_a = _attn_prompt(b3(qib), tail3, b3(qb), ki.astype(BF16), b3(kb), b3(vb), gn_a,
                             tq=_row_tile(seq, 256), tk=_row_tile(seq, 512), topk=topk_p,
                             n_heads=n_heads, idx_dim=idx_dim)
        xp = _finish(xp, mix_l.reshape(bsz * seq, d_lru), mix_a.reshape(bsz * seq, d_att),
                     wo, wfi, wfo, fin_vec, alpha=alpha, tm=_row_tile(bsz * seq, 512))
        outs_p.append((k.reshape(bsz, seq, n_heads, head_dim), v.reshape(bsz, seq, n_heads, head_dim),
                       ki, xl3[:, seq - (CONV_W - 1):], h_last[:, 0]))

        xl, gate, qb, k, kb, v, vb, qib, tail = _project(
            xs, w_pad, pos_s, tm=_row_tile(dbs * dseq, 256), **geom)
        d3 = lambda a: a.reshape(dbs, dseq, a.shape[-1])
        tm_major = lambda a: jnp.swapaxes(d3(a), 0, 1)
        xl3 = d3(xl)
        mix_l, h_last = _lru_sample(
            tm_major(xl), tm_major(gate), jnp.swapaxes(state_conv[l], 0, 1).astype(F32),
            state_h[l], conv_w[l], lru_vec, wa_bd, wx_bd)
        tail3 = d3(tail)
        ki = tail3[:, :, :idx_dim]
        wi = tail3[:, :, idx_dim:idx_dim + IDX_HEADS]
        mix_a = _attn_sample(
            d3(qib), wi, d3(qb), ki.astype(BF16), d3(kb), d3(vb),
            cache_k[l].reshape(n_phys, page, d_att), cache_v[l].reshape(n_phys, page, d_att),
            cache_kidx[l], page_table, gn_a, topk=topk_s, n_heads=n_heads)
        xs = _finish(xs, jnp.swapaxes(mix_l, 0, 1).reshape(dbs * dseq, d_lru),
                     mix_a.reshape(dbs * dseq, d_att), wo, wfi, wfo, fin_vec, alpha=alpha,
                     tm=_row_tile(dbs * dseq, 256))
        conv_new = jnp.concatenate([state_conv[l].astype(F32), xl3], axis=1)[:, -(CONV_W - 1):]
        outs_s.append((k.reshape(dbs, dseq, n_heads, head_dim), v.reshape(dbs, dseq, n_heads, head_dim),
                       ki, conv_new, h_last))

    stack = lambda outs, j: jnp.stack([o[j] for o in outs])
    return (xp.reshape(bsz, seq, d_model), xs.reshape(dbs, dseq, d_model),
            *(stack(outs_p, j) for j in range(5)), *(stack(outs_s, j) for j in range(5)))
```

```python
import functools

import jax
import jax.numpy as jnp
import numpy as np
from jax import lax
from jax.experimental import pallas as pl
from jax.experimental.pallas import tpu as pltpu

CONV_W = 4
LRU_C = 8.0
LRU_BLOCKS = 8
IDX_HEADS = 8
TOPK_MAX = 256
ROPE_FRACTION = 4
ROPE_THETA = 500000.0
RMS_EPS = 1e-6
LN_EPS = 1e-5

LANES = 128
SUBLANES = 8
VMEM_LIMIT = 56 * 1024 * 1024
MASKED = -1e30
INT_MIN = -2 ** 31

F32 = jnp.float32
BF16 = jnp.bfloat16
I32 = jnp.int32


def _params(*sem):
    return pltpu.CompilerParams(dimension_semantics=sem, vmem_limit_bytes=VMEM_LIMIT)


def _nt_dot(a, b):
    return lax.dot_general(a, b, (((1,), (1,)), ((), ())), preferred_element_type=F32)


def _proj_body(x_ref, w_ref, c_ref, s1_ref, s2_ref,
               xl_ref, gate_ref, qb_ref, k_ref, kb_ref, v_ref, vb_ref, qib_ref, tail_ref,
               *, d_lru, d_att, d_qi, idx_dim, rope_half, q_scale):
    xb = x_ref[...].astype(BF16)
    c, s1, s2 = c_ref[...], s1_ref[...], s2_ref[...]

    def proj(lo, width):
        return jnp.dot(xb, w_ref[:, lo:lo + width], preferred_element_type=F32)

    def tiled(t, width):
        reps = width // LANES
        return t if reps == 1 else jnp.concatenate([t] * reps, axis=1)

    def rope(z, cc, ss1, ss2):
        width = z.shape[1]
        return (z * cc + pltpu.roll(z, width - rope_half, 1) * ss1
                + pltpu.roll(z, rope_half, 1) * ss2)

    lo = 0
    xl_ref[...] = proj(lo, d_lru); lo += d_lru
    gate_ref[...] = proj(lo, d_lru); lo += d_lru
    q = rope(proj(lo, d_att), tiled(c, d_att), tiled(s1, d_att), tiled(s2, d_att)); lo += d_att
    qb_ref[...] = (q * q_scale).astype(BF16)
    k = rope(proj(lo, d_att), tiled(c, d_att), tiled(s1, d_att), tiled(s2, d_att)); lo += d_att
    k_ref[...] = k
    kb_ref[...] = k.astype(BF16)
    v = proj(lo, d_att); lo += d_att
    v_ref[...] = v
    vb_ref[...] = v.astype(BF16)
    qi = rope(proj(lo, d_qi), tiled(c, d_qi), tiled(s1, d_qi), tiled(s2, d_qi)); lo += d_qi
    qib_ref[...] = qi.astype(BF16)
    tail = proj(lo, LANES)
    is_key = lax.broadcasted_iota(I32, tail.shape, 1) < idx_dim
    tail_ref[...] = rope(tail, jnp.where(is_key, c, 1.0), jnp.where(is_key, s1, 0.0),
                         jnp.where(is_key, s2, 0.0))


def _rope_tables(pos, head_dim):
    rope_dim = head_dim // ROPE_FRACTION
    half = rope_dim // 2
    freqs = ROPE_THETA ** (-jnp.arange(half, dtype=F32) / half)
    ang = pos.astype(F32)[:, None] * freqs[None, :]
    cos, sin = jnp.cos(ang), jnp.sin(ang)
    n = pos.shape[0]
    rest = head_dim - rope_dim
    c = jnp.concatenate([cos, cos, jnp.ones((n, rest), F32)], 1)
    s1 = jnp.concatenate([-sin, jnp.zeros((n, half + rest), F32)], 1)
    s2 = jnp.concatenate([jnp.zeros((n, half), F32), sin, jnp.zeros((n, rest), F32)], 1)
    reps = LANES // head_dim
    return [jnp.tile(t, (1, reps)) for t in (c, s1, s2)], half


def _project(x2d, w_pad, pos, *, d_lru, d_att, d_qi, idx_dim, head_dim, tm):
    n, d_model = x2d.shape
    (c, s1, s2), half = _rope_tables(pos, head_dim)
    row = lambda w: pl.BlockSpec((tm, w), lambda i: (i, 0))
    outs = [
        (d_lru, F32), (d_lru, F32), (d_att, BF16), (d_att, F32), (d_att, BF16),
        (d_att, F32), (d_att, BF16), (d_qi, BF16), (LANES, F32)]
    body = functools.partial(_proj_body, d_lru=d_lru, d_att=d_att, d_qi=d_qi, idx_dim=idx_dim,
                             rope_half=half, q_scale=head_dim ** -0.5)
    return pl.pallas_call(
        body,
        grid=(n // tm,),
        in_specs=[row(d_model), pl.BlockSpec(w_pad.shape, lambda i: (0, 0)),
                  row(LANES), row(LANES), row(LANES)],
        out_specs=[row(w) for w, _ in outs],
        out_shape=[jax.ShapeDtypeStruct((n, w), dt) for w, dt in outs],
        compiler_params=_params("parallel"),
        name="proj",
    )(x2d, w_pad, c, s1, s2)


def _softplus(x):
    return jnp.maximum(x, 0.0) + jnp.log1p(jnp.exp(-jnp.abs(x)))


def _gelu_tanh(x):
    return 0.5 * x * (1.0 + jnp.tanh(np.sqrt(2.0 / np.pi).astype(np.float32)
                                     * (x + 0.044715 * (x * x * x))))


def _lru_gates(xc, wa_ref, wx_ref, b_a, b_x, lam):
    xcb = xc.astype(BF16)
    r = jax.nn.sigmoid(jnp.dot(xcb, wa_ref[...], preferred_element_type=F32) + b_a)
    i = jax.nn.sigmoid(jnp.dot(xcb, wx_ref[...], preferred_element_type=F32) + b_x)
    log_a = -LRU_C * r * _softplus(-lam)
    a = jnp.exp(log_a)
    t = jnp.tanh(log_a)
    b = jnp.sqrt(-2.0 * t / (1.0 - t)) * (i * xc)
    return a, b


def _rms_gain(y, g):
    return y * lax.rsqrt(jnp.mean(y * y, axis=-1, keepdims=True) + RMS_EPS) * g


def _lru_prompt_body(xl_ref, gate_ref, cprev_ref, h0_ref, cw_ref, p_ref, wa_ref, wx_ref,
                     mix_ref, hlast_ref, ext_ref, hc_ref, *, tt):
    j = pl.program_id(1)

    @pl.when(j == 0)
    def _():
        ext_ref[0:SUBLANES, :] = cprev_ref[0]
        hc_ref[0:1, :] = h0_ref[0]

    xl = xl_ref[0]
    ext_ref[SUBLANES:SUBLANES + tt, :] = xl
    conv_b, b_a, b_x, lam, gn = (p_ref[r:r + 1, :] for r in range(5))
    xc = conv_b + (cw_ref[0:1, :] * ext_ref[SUBLANES - 3:SUBLANES - 3 + tt, :]
                   + cw_ref[1:2, :] * ext_ref[SUBLANES - 2:SUBLANES - 2 + tt, :]
                   + cw_ref[2:3, :] * ext_ref[SUBLANES - 1:SUBLANES - 1 + tt, :]
                   + cw_ref[3:4, :] * xl)
    ext_ref[0:SUBLANES, :] = ext_ref[tt:tt + SUBLANES, :]

    a, b = _lru_gates(xc, wa_ref, wx_ref, b_a, b_x, lam)
    row = lax.broadcasted_iota(I32, a.shape, 0)
    d = 1
    while d < tt:
        keep = row >= d
        a_prev = jnp.where(keep, pltpu.roll(a, d, 0), 1.0)
        b_prev = jnp.where(keep, pltpu.roll(b, d, 0), 0.0)
        b = a * b_prev + b
        a = a * a_prev
        d *= 2
    h = a * hc_ref[0:1, :] + b
    hc_ref[0:1, :] = h[tt - 1:tt, :]
    hlast_ref[0] = h[tt - 1:tt, :]
    mix_ref[0] = _rms_gain(h * _gelu_tanh(gate_ref[0]), gn).astype(BF16)


def _lru_prompt(xl, gate, conv_prev, h0, conv_w, pvec, wa_bd, wx_bd, *, tt):
    bsz, t, d = xl.shape
    cprev8 = jnp.concatenate(
        [jnp.zeros((bsz, SUBLANES - (CONV_W - 1), d), F32), conv_prev.astype(F32)], axis=1)
    const = lambda shape: pl.BlockSpec(shape, lambda b, j: (0,) * len(shape))
    return pl.pallas_call(
        functools.partial(_lru_prompt_body, tt=tt),
        grid=(bsz, t // tt),
        in_specs=[pl.BlockSpec((1, tt, d), lambda b, j: (b, j, 0)),
                  pl.BlockSpec((1, tt, d), lambda b, j: (b, j, 0)),
                  pl.BlockSpec((1, SUBLANES, d), lambda b, j: (b, 0, 0)),
                  pl.BlockSpec((1, 1, d), lambda b, j: (b, 0, 0)),
                  const(conv_w.shape), const(pvec.shape), const(wa_bd.shape), const(wx_bd.shape)],
        out_specs=[pl.BlockSpec((1, tt, d), lambda b, j: (b, j, 0)),
                   pl.BlockSpec((1, 1, d), lambda b, j: (b, 0, 0))],
        out_shape=[jax.ShapeDtypeStruct((bsz, t, d), BF16),
                   jax.ShapeDtypeStruct((bsz, 1, d), F32)],
        scratch_shapes=[pltpu.VMEM((tt + SUBLANES, d), F32), pltpu.VMEM((SUBLANES, d), F32)],
        compiler_params=_params("parallel", "arbitrary"),
        name="lru_prompt",
    )(xl, gate, cprev8, h0.astype(F32)[:, None, :], conv_w, pvec, wa_bd, wx_bd)


def _lru_sample_body(xl_ref, gate_ref, cprev_ref, h0_ref, cw_ref, p_ref, wa_ref, wx_ref,
                     mix_ref, hlast_ref, *, t_len):
    conv_b, b_a, b_x, lam, gn = (p_ref[r:r + 1, :] for r in range(5))
    xp = [cprev_ref[s] for s in range(CONV_W - 1)] + [xl_ref[s] for s in range(t_len)]
    h = h0_ref[...]
    for s in range(t_len):
        xc = conv_b + (cw_ref[0:1, :] * xp[s] + cw_ref[1:2, :] * xp[s + 1]
                       + cw_ref[2:3, :] * xp[s + 2] + cw_ref[3:4, :] * xp[s + 3])
        a, b = _lru_gates(xc, wa_ref, wx_ref, b_a, b_x, lam)
        h = a * h + b
        mix_ref[s] = _rms_gain(h * _gelu_tanh(gate_ref[s]), gn).astype(BF16)
    hlast_ref[...] = h


def _lru_sample(xl_t, gate_t, cprev_t, h0, conv_w, pvec, wa_bd, wx_bd):
    t_len, dbs, d = xl_t.shape
    return pl.pallas_call(
        functools.partial(_lru_sample_body, t_len=t_len),
        out_shape=[jax.ShapeDtypeStruct((t_len, dbs, d), BF16),
                   jax.ShapeDtypeStruct((dbs, d), F32)],
        compiler_params=pltpu.CompilerParams(vmem_limit_bytes=VMEM_LIMIT),
        name="lru_sample",
    )(xl_t, gate_t, cprev_t, h0.astype(F32), conv_w, pvec, wa_bd, wx_bd)


def _sortable_key(score):
    bits = pltpu.bitcast(score, I32)
    key = jnp.where(bits >= 0, bits, bits ^ 0x7FFFFFFF)
    return jnp.where(bits == INT_MIN, 0, key)


def _threshold_search(count, total, topk, idx_bits, pos_ref):
    shape = pos_ref.shape

    def value_step(it, carry):
        base, n_base = carry
        trial = base ^ lax.shift_left(jnp.int32(1), (31 - it).astype(I32))
        n = count(lambda k, idx: k >= trial)
        ok = n >= topk
        return jnp.where(ok, trial, base), jnp.where(ok, n, n_base)

    theta, n_ge = lax.fori_loop(
        0, 32, value_step, (jnp.full(shape, INT_MIN, I32), jnp.zeros(shape, I32) + total))
    pos_ref[...] = jnp.full(shape, 2 ** idx_bits - 1, I32)

    @pl.when(jnp.max(n_ge) > topk)
    def _():
        need = topk - count(lambda k, idx: k > theta)

        def index_step(it, pos):
            trial = pos + lax.shift_left(jnp.int32(1), (idx_bits - 1 - it).astype(I32))
            below = count(lambda k, idx: (k == theta) & (idx < trial))
            return jnp.where(below < need, trial, pos)

        pos_ref[...] = lax.fori_loop(0, idx_bits, index_step, jnp.zeros(shape, I32))

    return theta


def _selected(key, idx, theta, pos):
    return (key > theta) | ((key == theta) & (idx <= pos))


def _attn_prompt_body(qit_ref, wt_ref, qt_ref, kidx_ref, k_ref, vt_ref, gn_ref, out_ref,
                      key_ref, bias_ref, y_ref, pos_ref, m_ref, l_ref, s_ref, *, tq, ch, topk,
                      idx_bits, n_heads, head_dim, idx_heads, idx_dim):
    i = pl.program_id(1)
    n_keys = (i + 1) * tq
    qpos = i * tq + lax.broadcasted_iota(I32, (1, tq), 1)
    sub = LANES
    kpos_sub = lax.broadcasted_iota(I32, (sub, tq), 0)
    kpos_ch = lax.broadcasted_iota(I32, (ch, tq), 0)
    nc = (i + 1) * (tq // ch)

    def score_chunk(c, carry):
        for j in range(ch // sub):
            start = pl.multiple_of(c * ch + j * sub, sub)
            rows = pl.ds(start, sub)
            kc = kidx_ref[0, rows, :]
            acc = jnp.zeros((sub, tq), F32)
            for h in range(idx_heads):
                d = jnp.dot(kc, qit_ref[0, h * idx_dim:(h + 1) * idx_dim, :],
                            preferred_element_type=F32)
                acc = acc + jnp.maximum(d, 0.0) * wt_ref[0, h:h + 1, :]
            score = jnp.where(start + kpos_sub <= qpos, acc, -jnp.inf)
            key_ref[rows, :] = _sortable_key(score)
        return carry

    lax.fori_loop(0, nc, score_chunk, 0)

    def count(pred):
        def body(c, acc):
            rows = pl.ds(pl.multiple_of(c * ch, ch), ch)
            hit = jnp.where(pred(key_ref[rows, :], c * ch + kpos_ch), 1, 0).astype(I32)
            return acc + jnp.sum(hit.reshape(ch // SUBLANES, SUBLANES, tq), axis=0)
        acc = lax.fori_loop(0, nc, body, jnp.zeros((SUBLANES, tq), I32))
        return jnp.sum(acc, axis=0, keepdims=True)

    theta = _threshold_search(count, n_keys, topk, idx_bits, pos_ref)
    pos = pos_ref[...]

    def bias_chunk(c, carry):
        rows = pl.ds(pl.multiple_of(c * ch, ch), ch)
        idx = c * ch + kpos_ch
        sel = _selected(key_ref[rows, :], idx, theta, pos) & (idx <= qpos)
        bias_ref[rows, :] = jnp.where(sel, 0.0, MASKED)
        return carry

    lax.fori_loop(0, nc, bias_chunk, 0)

    heads = [(h, slice(h * head_dim, (h + 1) * head_dim)) for h in range(n_heads)]
    m_ref[...] = jnp.full(m_ref.shape, -jnp.inf, F32)
    l_ref[...] = jnp.zeros(l_ref.shape, F32)
    y_ref[...] = jnp.zeros(y_ref.shape, F32)

    def kv_chunk(c, carry):
        rows = pl.ds(pl.multiple_of(c * ch, ch), ch)
        for h, hs in heads:
            s_ref[h] = jnp.dot(k_ref[0, h, rows, :], qt_ref[0, hs, :],
                               preferred_element_type=F32)
        for h, hs in heads:
            s = s_ref[h] + bias_ref[rows, :]
            m = m_ref[h:h + 1, :]
            m_new = jnp.maximum(m, jnp.max(s, axis=0, keepdims=True))
            alpha = jnp.exp(m - m_new)
            p = jnp.exp(s - m_new)
            m_ref[h:h + 1, :] = m_new
            l_ref[h:h + 1, :] = alpha * l_ref[h:h + 1, :] + jnp.sum(p, axis=0, keepdims=True)
            y_ref[hs, :] = alpha * y_ref[hs, :] + jnp.dot(
                vt_ref[0, c, hs, :], p.astype(BF16), preferred_element_type=F32)
        return carry

    lax.fori_loop(0, nc, kv_chunk, 0)
    for h, hs in heads:
        y_ref[hs, :] = y_ref[hs, :] / l_ref[h:h + 1, :]
    out_ref[0] = _rms_gain(y_ref[...].T, gn_ref[...]).astype(BF16)


def _attn_prompt(qib, wi, qb, kidxb, kb, vb, gn_att, *, tq, topk, n_heads, idx_dim):
    bsz, t, d_att = qb.shape
    head_dim = d_att // n_heads
    ch = tq
    idx_bits = max(1, int(t - 1).bit_length())
    qit, qt, wt = (jnp.swapaxes(a, 1, 2) for a in (qib, qb, wi))
    k_hm = kb.reshape(bsz, t, n_heads, head_dim).transpose(0, 2, 1, 3)
    vt = vb.reshape(bsz, t // ch, ch, d_att).transpose(0, 1, 3, 2)
    cols = lambda a: pl.BlockSpec((1, a.shape[1], tq), lambda b, i: (b, 0, i))
    full = lambda a: pl.BlockSpec((1,) + a.shape[1:], lambda b, i: (b,) + (0,) * (a.ndim - 1))
    body = functools.partial(
        _attn_prompt_body, tq=tq, ch=ch, topk=topk, idx_bits=idx_bits, n_heads=n_heads,
        head_dim=head_dim, idx_heads=wi.shape[2], idx_dim=idx_dim)
    return pl.pallas_call(
        body,
        grid=(bsz, t // tq),
        in_specs=[cols(qit), cols(wt), cols(qt), full(kidxb), full(k_hm), full(vt),
                  pl.BlockSpec((1, d_att), lambda b, i: (0, 0))],
        out_specs=pl.BlockSpec((1, tq, d_att), lambda b, i: (b, i, 0)),
        out_shape=jax.ShapeDtypeStruct((bsz, t, d_att), BF16),
        scratch_shapes=[pltpu.VMEM((t, tq), I32), pltpu.VMEM((t, tq), F32),
                        pltpu.VMEM((d_att, tq), F32), pltpu.VMEM((1, tq), I32),
                        pltpu.VMEM((n_heads, tq), F32), pltpu.VMEM((n_heads, tq), F32),
                        pltpu.VMEM((n_heads, ch, tq), F32)],
        compiler_params=_params("parallel", "arbitrary"),
        name="attn_prompt",
    )(qit, wt, qt, kidxb, k_hm, vt, gn_att)


def _score_sample_body(pt_ref, qi_ref, w_ref, *refs, n_pages, page, t_len, idx_heads, past_len):
    del pt_ref
    page_refs, new_ref, out_ref = refs[:n_pages], refs[n_pages], refs[n_pages + 1]
    qi = qi_ref[0]
    w = w_ref[0]
    for p in range(n_pages + 1):
        kc = page_refs[p][0].astype(BF16) if p < n_pages else new_ref[0]
        d = jnp.maximum(_nt_dot(qi, kc), 0.0) * w
        s = jnp.sum(d.reshape(t_len, idx_heads, page), axis=1)
        if p == n_pages:
            tpos = lax.broadcasted_iota(I32, (t_len, page), 0)
            kpos = lax.broadcasted_iota(I32, (t_len, page), 1)
            s = jnp.where(kpos <= tpos, s, -jnp.inf)
        out_ref[0, :, p * page:(p + 1) * page] = s


def _select_sample_body(s_ref, bias_ref, key_ref, pos_ref, *, topk, idx_bits, page):
    nc, rows, _ = key_ref.shape
    for c in range(nc):
        key_ref[c] = _sortable_key(s_ref[:, c * page:(c + 1) * page])
    lane = lax.broadcasted_iota(I32, (rows, page), 1)

    def count(pred):
        def body(c, acc):
            return acc + jnp.where(pred(key_ref[c], c * page + lane), 1, 0).astype(I32)
        acc = lax.fori_loop(0, nc, body, jnp.zeros((rows, page), I32))
        return jnp.sum(acc, axis=1, keepdims=True)

    theta = _threshold_search(count, nc * page, topk, idx_bits, pos_ref)
    pos = pos_ref[...]
    for c in range(nc):
        s = s_ref[:, c * page:(c + 1) * page]
        sel = _selected(key_ref[c], c * page + lane, theta, pos) & (s > -jnp.inf)
        bias_ref[:, c * page:(c + 1) * page] = jnp.where(sel, 0.0, MASKED)


def _attn_sample_body(pt_ref, q_ref, bias_ref, gn_ref, *refs, n_pages, page, t_len, n_heads,
                      head_dim):
    del pt_ref
    k_refs, v_refs = refs[:n_pages], refs[n_pages:2 * n_pages]
    knew_ref, vnew_ref, out_ref, logit_ref = refs[2 * n_pages:]
    d_att = n_heads * head_dim
    q = q_ref[0].astype(F32)
    head_of_lane = lax.broadcasted_iota(I32, (n_heads, d_att), 1) // head_dim
    own = head_of_lane == lax.broadcasted_iota(I32, (n_heads, d_att), 0)
    qbd = jnp.concatenate(
        [jnp.where(own, jnp.broadcast_to(q[t:t + 1, :], (n_heads, d_att)), 0.0)
         for t in range(t_len)], axis=0).astype(BF16)
    for p in range(n_pages + 1):
        kc = k_refs[p][0].astype(BF16) if p < n_pages else knew_ref[0]
        logit_ref[:, p * page:(p + 1) * page] = _nt_dot(qbd, kc)
    outs = []
    for t in range(t_len):
        rows = slice(t * n_heads, (t + 1) * n_heads)
        s = logit_ref[rows, :] + bias_ref[0, t:t + 1, :]
        m = jnp.max(s, axis=1, keepdims=True)
        logit_ref[rows, :] = jnp.exp(s - m)
    pb = logit_ref[...]
    l = jnp.sum(pb, axis=1, keepdims=True)
    acc = jnp.zeros((t_len * n_heads, d_att), F32)
    for p in range(n_pages + 1):
        vc = v_refs[p][0].astype(BF16) if p < n_pages else vnew_ref[0]
        acc = acc + jnp.dot(pb[:, p * page:(p + 1) * page].astype(BF16), vc,
                            preferred_element_type=F32)
    acc = acc / l
    for t in range(t_len):
        o = jnp.where(own, acc[t * n_heads:(t + 1) * n_heads, :], 0.0)
        outs.append(jnp.sum(o, axis=0, keepdims=True))
    y = jnp.concatenate(outs, axis=0)
    out_ref[0] = _rms_gain(y, gn_ref[...]).astype(BF16)


def _attn_sample(qib, wi, qb, kidx_new, k_new, v_new, cache_k, cache_v, cache_kidx, page_table,
                 gn_att, *, topk, n_heads):
    dbs, t_len, d_att = qb.shape
    n_pages = page_table.shape[1]
    page = cache_k.shape[1]
    idx_dim = cache_kidx.shape[2]
    idx_heads = wi.shape[2]
    nkp = (n_pages + 1) * page
    past_len = n_pages * page
    pad = lambda a: jnp.pad(a, ((0, 0), (0, page - t_len), (0, 0)))

    def paged(width):
        return [pl.BlockSpec((1, page, width), functools.partial(
            lambda b, pt, p: (pt[b, p], 0, 0), p=p)) for p in range(n_pages)]

    per_seq = lambda shape: pl.BlockSpec((1,) + shape, lambda b, pt: (b, 0, 0))

    scores = pl.pallas_call(
        functools.partial(_score_sample_body, n_pages=n_pages, page=page, t_len=t_len,
                          idx_heads=idx_heads, past_len=past_len),
        grid_spec=pltpu.PrefetchScalarGridSpec(
            num_scalar_prefetch=1, grid=(dbs,),
            in_specs=[per_seq((t_len * idx_heads, idx_dim)), per_seq((t_len * idx_heads, 1))]
                     + paged(idx_dim) + [per_seq((page, idx_dim))],
            out_specs=per_seq((t_len, nkp))),
        out_shape=jax.ShapeDtypeStruct((dbs, t_len, nkp), F32),
        compiler_params=_params("parallel"),
        name="score_sample",
    )(page_table, qib.reshape(dbs, t_len * idx_heads, idx_dim),
      wi.reshape(dbs, t_len * idx_heads, 1), *([cache_kidx] * n_pages), pad(kidx_new))

    rows = dbs * t_len
    tr = min(rows, 128)
    bias = pl.pallas_call(
        functools.partial(_select_sample_body, topk=topk,
                          idx_bits=max(1, int(nkp - 1).bit_length()), page=page),
        grid=(rows // tr,),
        in_specs=[pl.BlockSpec((tr, nkp), lambda r: (r, 0))],
        out_specs=pl.BlockSpec((tr, nkp), lambda r: (r, 0)),
        out_shape=jax.ShapeDtypeStruct((rows, nkp), F32),
        scratch_shapes=[pltpu.VMEM((n_pages + 1, tr, page), I32), pltpu.VMEM((tr, 1), I32)],
        compiler_params=_params("parallel"),
        name="select_sample",
    )(scores.reshape(rows, nkp)).reshape(dbs, t_len, nkp)

    return pl.pallas_call(
        functools.partial(_attn_sample_body, n_pages=n_pages, page=page, t_len=t_len,
                          n_heads=n_heads, head_dim=d_att // n_heads),
        grid_spec=pltpu.PrefetchScalarGridSpec(
            num_scalar_prefetch=1, grid=(dbs,),
            in_specs=[per_seq((t_len, d_att)), per_seq((t_len, nkp)),
                      pl.BlockSpec((1, d_att), lambda b, pt: (0, 0))]
                     + paged(d_att) + paged(d_att)
                     + [per_seq((page, d_att)), per_seq((page, d_att))],
            out_specs=per_seq((t_len, d_att)),
            scratch_shapes=[pltpu.VMEM((t_len * n_heads, nkp), F32)]),
        out_shape=jax.ShapeDtypeStruct((dbs, t_len, d_att), BF16),
        compiler_params=_params("parallel"),
        name="attn_sample",
    )(page_table, qb, bias, gn_att, *([cache_k] * n_pages), *([cache_v] * n_pages),
      pad(k_new), pad(v_new))


def _layer_norm(x, g, b):
    mu = jnp.mean(x, axis=-1, keepdims=True)
    xc = x - mu
    var = jnp.mean(xc * xc, axis=-1, keepdims=True)
    return xc * lax.rsqrt(var + LN_EPS) * g + b


def _finish_body(x_ref, ml_ref, ma_ref, wo_ref, wfi_ref, wfo_ref, p_ref, out_ref,
                 *, alpha, d_lru, d_ff, fc):
    ln1_g, ln1_b, ln2_g, ln2_b = (p_ref[r:r + 1, :] for r in range(4))
    y = (jnp.dot(ml_ref[...], wo_ref[0:d_lru, :], preferred_element_type=F32)
         + jnp.dot(ma_ref[...], wo_ref[d_lru:, :], preferred_element_type=F32))
    x1 = _layer_norm(alpha * x_ref[...] + y, ln1_g, ln1_b)
    x1b = x1.astype(BF16)
    f = jnp.zeros(x1.shape, F32)
    for c in range(d_ff // fc):
        u = jnp.dot(x1b, wfi_ref[:, c * fc:(c + 1) * fc], preferred_element_type=F32)
        g = jnp.dot(x1b, wfi_ref[:, d_ff + c * fc:d_ff + (c + 1) * fc],
                    preferred_element_type=F32)
        hidden = (g * jax.nn.sigmoid(g) * u).astype(BF16)
        f = f + jnp.dot(hidden, wfo_ref[c * fc:(c + 1) * fc, :], preferred_element_type=F32)
    out_ref[...] = _layer_norm(alpha * x1 + f, ln2_g, ln2_b)


def _finish(x2d, mix_lru, mix_att, wo, wfi, wfo, pvec, *, alpha, tm):
    n, d_model = x2d.shape
    d_lru = mix_lru.shape[1]
    d_ff = wfo.shape[0]
    fc = 2 * LANES if d_ff % (2 * LANES) == 0 else LANES
    row = lambda w: pl.BlockSpec((tm, w), lambda i: (i, 0))
    const = lambda a: pl.BlockSpec(a.shape, lambda i: (0, 0), pipeline_mode=pl.Buffered(1))
    return pl.pallas_call(
        functools.partial(_finish_body, alpha=alpha, d_lru=d_lru, d_ff=d_ff, fc=fc),
        grid=(n // tm,),
        in_specs=[row(d_model), row(d_lru), row(mix_att.shape[1]),
                  const(wo), const(wfi), const(wfo), const(pvec)],
        out_specs=row(d_model),
        out_shape=jax.ShapeDtypeStruct((n, d_model), F32),
        compiler_params=_params("parallel"),
        name="finish",
    )(x2d, mix_lru, mix_att, wo, wfi, wfo, pvec)


def _block_diag(w):
    nb, bi, bo = w.shape
    eye = jnp.eye(nb, dtype=w.dtype)
    return (w[:, :, None, :] * eye[:, None, :, None]).reshape(nb * bi, nb * bo)


def _row_tile(n, want):
    tm = min(n, want)
    while n % tm:
        tm //= 2
    return tm


def kernel(x_prompt, x_sample, cache_k, cache_v, cache_kidx, state_conv, state_h, page_table,
           w_in, conv_w, conv_b, w_a, b_a, w_x, b_x, lam, gn_lru, gn_att, w_out,
           ln1_g, ln1_b, w_ffn_in, w_ffn_out, ln2_g, ln2_b):
    depth, d_model, d_in = w_in.shape
    bsz, seq, _ = x_prompt.shape
    dbs, dseq, _ = x_sample.shape
    d_lru = conv_w.shape[2]
    n_phys, page, n_heads, head_dim = cache_k.shape[1:]
    d_att = n_heads * head_dim
    idx_dim = cache_kidx.shape[3]
    d_qi = IDX_HEADS * idx_dim
    n_pages = page_table.shape[1]
    past_len = n_pages * page
    alpha = (2.0 * depth) ** 0.25
    assert d_in == 2 * d_lru + 3 * d_att + d_qi + idx_dim + IDX_HEADS
    assert idx_dim + IDX_HEADS <= LANES and LANES % head_dim == 0 and head_dim == idx_dim
    geom = dict(d_lru=d_lru, d_att=d_att, d_qi=d_qi, idx_dim=idx_dim, head_dim=head_dim)
    d_main = d_in - idx_dim - IDX_HEADS

    pos_p = jnp.tile(jnp.arange(seq), bsz)
    pos_s = jnp.tile(past_len + jnp.arange(dseq), dbs)
    topk_p = min(TOPK_MAX, seq // 4)
    topk_s = min(TOPK_MAX, (past_len + dseq) // 4)

    xp = x_prompt.reshape(bsz * seq, d_model)
    xs = x_sample.reshape(dbs * dseq, d_model)
    outs_p, outs_s = [], []
    for l in range(depth):
        w_pad = jnp.pad(w_in[l], ((0, 0), (0, d_main + LANES - d_in))).astype(BF16)
        wa_bd = _block_diag(w_a[l]).astype(BF16)
        wx_bd = _block_diag(w_x[l]).astype(BF16)
        lru_vec = jnp.stack([conv_b[l], b_a[l], b_x[l], lam[l], gn_lru[l]]
                            + [jnp.zeros_like(lam[l])] * 3)
        fin_vec = jnp.stack([ln1_g[l], ln1_b[l], ln2_g[l], ln2_b[l]])
        wo, wfi, wfo = (w_out[l].astype(BF16), w_ffn_in[l].astype(BF16),
                        w_ffn_out[l].astype(BF16))
        gn_a = gn_att[l][None, :]

        xl, gate, qb, k, kb, v, vb, qib, tail = _project(
            xp, w_pad, pos_p, tm=_row_tile(bsz * seq, 512), **geom)
        b3 = lambda a: a.reshape(bsz, seq, a.shape[-1])
        xl3 = b3(xl)
        mix_l, h_last = _lru_prompt(
            xl3, b3(gate), jnp.zeros((bsz, CONV_W - 1, d_lru), F32), jnp.zeros((bsz, d_lru), F32),
            conv_w[l], lru_vec, wa_bd, wx_bd, tt=_row_tile(seq, 256))
        tail3 = b3(tail)
        ki = tail3[:, :, :idx_dim]
        mix_a = _attn_prompt(b3(qib), tail3[:, :, idx_dim:idx_dim + IDX_HEADS], b3(qb),
                             ki.astype(BF16), b3(kb), b3(vb), gn_a, tq=_row_tile(seq, 256),
                             topk=topk_p, n_heads=n_heads, idx_dim=idx_dim)
        xp = _finish(xp, mix_l.reshape(bsz * seq, d_lru), mix_a.reshape(bsz * seq, d_att),
                     wo, wfi, wfo, fin_vec, alpha=alpha, tm=_row_tile(bsz * seq, 512))
        outs_p.append((k.reshape(bsz, seq, n_heads, head_dim), v.reshape(bsz, seq, n_heads, head_dim),
                       ki, xl3[:, seq - (CONV_W - 1):], h_last[:, 0]))

        xl, gate, qb, k, kb, v, vb, qib, tail = _project(
            xs, w_pad, pos_s, tm=_row_tile(dbs * dseq, 256), **geom)
        d3 = lambda a: a.reshape(dbs, dseq, a.shape[-1])
        tm_major = lambda a: jnp.swapaxes(d3(a), 0, 1)
        xl3 = d3(xl)
        mix_l, h_last = _lru_sample(
            tm_major(xl), tm_major(gate), jnp.swapaxes(state_conv[l], 0, 1).astype(F32),
            state_h[l], conv_w[l], lru_vec, wa_bd, wx_bd)
        tail3 = d3(tail)
        ki = tail3[:, :, :idx_dim]
        wi = tail3[:, :, idx_dim:idx_dim + IDX_HEADS]
        mix_a = _attn_sample(
            d3(qib), wi, d3(qb), ki.astype(BF16), d3(kb), d3(vb),
            cache_k[l].reshape(n_phys, page, d_att), cache_v[l].reshape(n_phys, page, d_att),
            cache_kidx[l], page_table, gn_a, topk=topk_s, n_heads=n_heads)
        xs = _finish(xs, jnp.swapaxes(mix_l, 0, 1).reshape(dbs * dseq, d_lru),
                     mix_a.reshape(dbs * dseq, d_att), wo, wfi, wfo, fin_vec, alpha=alpha,
                     tm=_row_tile(dbs * dseq, 256))
        conv_new = jnp.concatenate([state_conv[l].astype(F32), xl3], axis=1)[:, -(CONV_W - 1):]
        outs_s.append((k.reshape(dbs, dseq, n_heads, head_dim), v.reshape(dbs, dseq, n_heads, head_dim),
                       ki, conv_new, h_last))

    stack = lambda outs, j: jnp.stack([o[j] for o in outs])
    return (xp.reshape(bsz, seq, d_model), xs.reshape(dbs, dseq, d_model),
            *(stack(outs_p, j) for j in range(5)), *(stack(outs_s, j) for j in range(5)))
```

```python
import functools

import jax
import jax.numpy as jnp
import numpy as np
from jax import lax
from jax.experimental import pallas as pl
from jax.experimental.pallas import tpu as pltpu

CONV_W = 4
LRU_C = 8.0
LRU_BLOCKS = 8
IDX_HEADS = 8
TOPK_MAX = 256
ROPE_FRACTION = 4
ROPE_THETA = 500000.0
RMS_EPS = 1e-6
LN_EPS = 1e-5

LANES = 128
SUBLANES = 8
VMEM_LIMIT = 56 * 1024 * 1024
MASKED = -1e30
INT_MIN = -2 ** 31

F32 = jnp.float32
BF16 = jnp.bfloat16
I32 = jnp.int32


def _params(*sem):
    return pltpu.CompilerParams(dimension_semantics=sem, vmem_limit_bytes=VMEM_LIMIT)


def _nt_dot(a, b):
    return lax.dot_general(a, b, (((1,), (1,)), ((), ())), preferred_element_type=F32)


def _proj_body(x_ref, w_ref, c_ref, s1_ref, s2_ref,
               xl_ref, gate_ref, qb_ref, k_ref, kb_ref, v_ref, vb_ref, qib_ref, tail_ref,
               *, d_lru, d_att, d_qi, idx_dim, rope_half, q_scale):
    xb = x_ref[...].astype(BF16)
    c, s1, s2 = c_ref[...], s1_ref[...], s2_ref[...]

    def proj(lo, width):
        return jnp.dot(xb, w_ref[:, lo:lo + width], preferred_element_type=F32)

    def tiled(t, width):
        reps = width // LANES
        return t if reps == 1 else jnp.concatenate([t] * reps, axis=1)

    def rope(z, cc, ss1, ss2):
        width = z.shape[1]
        return (z * cc + pltpu.roll(z, width - rope_half, 1) * ss1
                + pltpu.roll(z, rope_half, 1) * ss2)

    lo = 0
    xl_ref[...] = proj(lo, d_lru); lo += d_lru
    gate_ref[...] = proj(lo, d_lru); lo += d_lru
    q = rope(proj(lo, d_att), tiled(c, d_att), tiled(s1, d_att), tiled(s2, d_att)); lo += d_att
    qb_ref[...] = (q * q_scale).astype(BF16)
    k = rope(proj(lo, d_att), tiled(c, d_att), tiled(s1, d_att), tiled(s2, d_att)); lo += d_att
    k_ref[...] = k
    kb_ref[...] = k.astype(BF16)
    v = proj(lo, d_att); lo += d_att
    v_ref[...] = v
    vb_ref[...] = v.astype(BF16)
    qi = rope(proj(lo, d_qi), tiled(c, d_qi), tiled(s1, d_qi), tiled(s2, d_qi)); lo += d_qi
    qib_ref[...] = qi.astype(BF16)
    tail = proj(lo, LANES)
    is_key = lax.broadcasted_iota(I32, tail.shape, 1) < idx_dim
    tail_ref[...] = rope(tail, jnp.where(is_key, c, 1.0), jnp.where(is_key, s1, 0.0),
                         jnp.where(is_key, s2, 0.0))


def _rope_tables(pos, head_dim):
    rope_dim = head_dim // ROPE_FRACTION
    half = rope_dim // 2
    freqs = ROPE_THETA ** (-jnp.arange(half, dtype=F32) / half)
    ang = pos.astype(F32)[:, None] * freqs[None, :]
    cos, sin = jnp.cos(ang), jnp.sin(ang)
    n = pos.shape[0]
    rest = head_dim - rope_dim
    c = jnp.concatenate([cos, cos, jnp.ones((n, rest), F32)], 1)
    s1 = jnp.concatenate([-sin, jnp.zeros((n, half + rest), F32)], 1)
    s2 = jnp.concatenate([jnp.zeros((n, half), F32), sin, jnp.zeros((n, rest), F32)], 1)
    reps = LANES // head_dim
    return [jnp.tile(t, (1, reps)) for t in (c, s1, s2)], half


def _project(x2d, w_pad, pos, *, d_lru, d_att, d_qi, idx_dim, head_dim, tm):
    n, d_model = x2d.shape
    (c, s1, s2), half = _rope_tables(pos, head_dim)
    row = lambda w: pl.BlockSpec((tm, w), lambda i: (i, 0))
    outs = [
        (d_lru, F32), (d_lru, F32), (d_att, BF16), (d_att, F32), (d_att, BF16),
        (d_att, F32), (d_att, BF16), (d_qi, BF16), (LANES, F32)]
    body = functools.partial(_proj_body, d_lru=d_lru, d_att=d_att, d_qi=d_qi, idx_dim=idx_dim,
                             rope_half=half, q_scale=head_dim ** -0.5)
    return pl.pallas_call(
        body,
        grid=(n // tm,),
        in_specs=[row(d_model), pl.BlockSpec(w_pad.shape, lambda i: (0, 0)),
                  row(LANES), row(LANES), row(LANES)],
        out_specs=[row(w) for w, _ in outs],
        out_shape=[jax.ShapeDtypeStruct((n, w), dt) for w, dt in outs],
        compiler_params=_params("parallel"),
        name="proj",
    )(x2d, w_pad, c, s1, s2)


def _softplus(x):
    return jnp.maximum(x, 0.0) + jnp.log1p(jnp.exp(-jnp.abs(x)))


def _gelu_tanh(x):
    return 0.5 * x * (1.0 + jnp.tanh(np.sqrt(2.0 / np.pi).astype(np.float32)
                                     * (x + 0.044715 * (x * x * x))))


def _lru_gates(xc, wa_ref, wx_ref, b_a, b_x, lam):
    xcb = xc.astype(BF16)
    r = jax.nn.sigmoid(jnp.dot(xcb, wa_ref[...], preferred_element_type=F32) + b_a)
    i = jax.nn.sigmoid(jnp.dot(xcb, wx_ref[...], preferred_element_type=F32) + b_x)
    log_a = -LRU_C * r * _softplus(-lam)
    a = jnp.exp(log_a)
    t = jnp.tanh(log_a)
    b = jnp.sqrt(-2.0 * t / (1.0 - t)) * (i * xc)
    return a, b


def _rms_gain(y, g):
    return y * lax.rsqrt(jnp.mean(y * y, axis=-1, keepdims=True) + RMS_EPS) * g


def _lru_prompt_body(xl_ref, gate_ref, cprev_ref, h0_ref, cw_ref, p_ref, wa_ref, wx_ref,
                     mix_ref, hlast_ref, ext_ref, hc_ref, *, tt):
    j = pl.program_id(1)

    @pl.when(j == 0)
    def _():
        ext_ref[0:SUBLANES, :] = cprev_ref[0]
        hc_ref[0:1, :] = h0_ref[0]

    xl = xl_ref[0]
    ext_ref[SUBLANES:SUBLANES + tt, :] = xl
    conv_b, b_a, b_x, lam, gn = (p_ref[r:r + 1, :] for r in range(5))
    xc = conv_b + (cw_ref[0:1, :] * ext_ref[SUBLANES - 3:SUBLANES - 3 + tt, :]
                   + cw_ref[1:2, :] * ext_ref[SUBLANES - 2:SUBLANES - 2 + tt, :]
                   + cw_ref[2:3, :] * ext_ref[SUBLANES - 1:SUBLANES - 1 + tt, :]
                   + cw_ref[3:4, :] * xl)
    ext_ref[0:SUBLANES, :] = ext_ref[tt:tt + SUBLANES, :]

    a, b = _lru_gates(xc, wa_ref, wx_ref, b_a, b_x, lam)
    row = lax.broadcasted_iota(I32, a.shape, 0)
    d = 1
    while d < tt:
        keep = row >= d
        a_prev = jnp.where(keep, pltpu.roll(a, d, 0), 1.0)
        b_prev = jnp.where(keep, pltpu.roll(b, d, 0), 0.0)
        b = a * b_prev + b
        a = a * a_prev
        d *= 2
    h = a * hc_ref[0:1, :] + b
    hc_ref[0:1, :] = h[tt - 1:tt, :]
    hlast_ref[0] = h[tt - 1:tt, :]
    mix_ref[0] = _rms_gain(h * _gelu_tanh(gate_ref[0]), gn).astype(BF16)


def _lru_prompt(xl, gate, conv_prev, h0, conv_w, pvec, wa_bd, wx_bd, *, tt):
    bsz, t, d = xl.shape
    cprev8 = jnp.concatenate(
        [jnp.zeros((bsz, SUBLANES - (CONV_W - 1), d), F32), conv_prev.astype(F32)], axis=1)
    const = lambda shape: pl.BlockSpec(shape, lambda b, j: (0,) * len(shape))
    return pl.pallas_call(
        functools.partial(_lru_prompt_body, tt=tt),
        grid=(bsz, t // tt),
        in_specs=[pl.BlockSpec((1, tt, d), lambda b, j: (b, j, 0)),
                  pl.BlockSpec((1, tt, d), lambda b, j: (b, j, 0)),
                  pl.BlockSpec((1, SUBLANES, d), lambda b, j: (b, 0, 0)),
                  pl.BlockSpec((1, 1, d), lambda b, j: (b, 0, 0)),
                  const(conv_w.shape), const(pvec.shape), const(wa_bd.shape), const(wx_bd.shape)],
        out_specs=[pl.BlockSpec((1, tt, d), lambda b, j: (b, j, 0)),
                   pl.BlockSpec((1, 1, d), lambda b, j: (b, 0, 0))],
        out_shape=[jax.ShapeDtypeStruct((bsz, t, d), BF16),
                   jax.ShapeDtypeStruct((bsz, 1, d), F32)],
        scratch_shapes=[pltpu.VMEM((tt + SUBLANES, d), F32), pltpu.VMEM((SUBLANES, d), F32)],
        compiler_params=_params("parallel", "arbitrary"),
        name="lru_prompt",
    )(xl, gate, cprev8, h0.astype(F32)[:, None, :], conv_w, pvec, wa_bd, wx_bd)


def _lru_sample_body(xl_ref, gate_ref, cprev_ref, h0_ref, cw_ref, p_ref, wa_ref, wx_ref,
                     mix_ref, hlast_ref, *, t_len):
    conv_b, b_a, b_x, lam, gn = (p_ref[r:r + 1, :] for r in range(5))
    xp = [cprev_ref[s] for s in range(CONV_W - 1)] + [xl_ref[s] for s in range(t_len)]
    h = h0_ref[...]
    for s in range(t_len):
        xc = conv_b + (cw_ref[0:1, :] * xp[s] + cw_ref[1:2, :] * xp[s + 1]
                       + cw_ref[2:3, :] * xp[s + 2] + cw_ref[3:4, :] * xp[s + 3])
        a, b = _lru_gates(xc, wa_ref, wx_ref, b_a, b_x, lam)
        h = a * h + b
        mix_ref[s] = _rms_gain(h * _gelu_tanh(gate_ref[s]), gn).astype(BF16)
    hlast_ref[...] = h


def _lru_sample(xl_t, gate_t, cprev_t, h0, conv_w, pvec, wa_bd, wx_bd):
    t_len, dbs, d = xl_t.shape
    return pl.pallas_call(
        functools.partial(_lru_sample_body, t_len=t_len),
        out_shape=[jax.ShapeDtypeStruct((t_len, dbs, d), BF16),
                   jax.ShapeDtypeStruct((dbs, d), F32)],
        compiler_params=pltpu.CompilerParams(vmem_limit_bytes=VMEM_LIMIT),
        name="lru_sample",
    )(xl_t, gate_t, cprev_t, h0.astype(F32), conv_w, pvec, wa_bd, wx_bd)


def _sortable_key(score):
    bits = pltpu.bitcast(score, I32)
    key = jnp.where(bits >= 0, bits, bits ^ 0x7FFFFFFF)
    return jnp.where(bits == INT_MIN, 0, key)


def _threshold_search(count, total, topk, idx_bits, pos_ref):
    shape = pos_ref.shape

    def value_step(it, carry):
        base, n_base = carry
        trial = base ^ lax.shift_left(jnp.int32(1), jnp.int32(31) - it)
        n = count(lambda k, idx: k >= trial)
        ok = n >= topk
        return jnp.where(ok, trial, base), jnp.where(ok, n, n_base)

    theta, n_ge = lax.fori_loop(
        0, 32, value_step, (jnp.full(shape, INT_MIN, I32), jnp.zeros(shape, I32) + total))
    pos_ref[...] = jnp.full(shape, 2 ** idx_bits - 1, I32)

    @pl.when(jnp.max(n_ge) > topk)
    def _():
        need = topk - count(lambda k, idx: k > theta)

        def index_step(it, pos):
            trial = pos + lax.shift_left(jnp.int32(1), jnp.int32(idx_bits - 1) - it)
            below = count(lambda k, idx: (k == theta) & (idx < trial))
            return jnp.where(below < need, trial, pos)

        pos_ref[...] = lax.fori_loop(0, idx_bits, index_step, jnp.zeros(shape, I32))

    return theta


def _selected(key, idx, theta, pos):
    return (key > theta) | ((key == theta) & (idx <= pos))


def _attn_prompt_body(qit_ref, wt_ref, qt_ref, kidx_ref, k_ref, vt_ref, gn_ref, out_ref,
                      key_ref, bias_ref, y_ref, pos_ref, m_ref, l_ref, s_ref, *, tq, ch, topk,
                      idx_bits, n_heads, head_dim, idx_heads, idx_dim):
    i = pl.program_id(1)
    n_keys = (i + 1) * tq
    qpos = i * tq + lax.broadcasted_iota(I32, (1, tq), 1)
    sub = LANES
    kpos_sub = lax.broadcasted_iota(I32, (sub, tq), 0)
    kpos_ch = lax.broadcasted_iota(I32, (ch, tq), 0)
    nc = (i + 1) * (tq // ch)

    def score_chunk(c, carry):
        for j in range(ch // sub):
            start = pl.multiple_of(c * ch + j * sub, sub)
            rows = pl.ds(start, sub)
            kc = kidx_ref[0, rows, :]
            acc = jnp.zeros((sub, tq), F32)
            for h in range(idx_heads):
                d = jnp.dot(kc, qit_ref[0, h * idx_dim:(h + 1) * idx_dim, :],
                            preferred_element_type=F32)
                acc = acc + jnp.maximum(d, 0.0) * wt_ref[0, h:h + 1, :]
            score = jnp.where(start + kpos_sub <= qpos, acc, -jnp.inf)
            key_ref[rows, :] = _sortable_key(score)
        return carry

    lax.fori_loop(0, nc, score_chunk, 0)

    def count(pred):
        def body(c, acc):
            rows = pl.ds(pl.multiple_of(c * ch, ch), ch)
            hit = jnp.where(pred(key_ref[rows, :], c * ch + kpos_ch), 1, 0).astype(I32)
            return acc + jnp.sum(hit.reshape(ch // SUBLANES, SUBLANES, tq), axis=0)
        acc = lax.fori_loop(0, nc, body, jnp.zeros((SUBLANES, tq), I32))
        return jnp.sum(acc, axis=0, keepdims=True)

    theta = _threshold_search(count, n_keys, topk, idx_bits, pos_ref)
    pos = pos_ref[...]

    def bias_chunk(c, carry):
        rows = pl.ds(pl.multiple_of(c * ch, ch), ch)
        idx = c * ch + kpos_ch
        sel = _selected(key_ref[rows, :], idx, theta, pos) & (idx <= qpos)
        bias_ref[rows, :] = jnp.where(sel, 0.0, MASKED)
        return carry

    lax.fori_loop(0, nc, bias_chunk, 0)

    heads = [(h, slice(h * head_dim, (h + 1) * head_dim)) for h in range(n_heads)]
    m_ref[...] = jnp.full(m_ref.shape, -jnp.inf, F32)
    l_ref[...] = jnp.zeros(l_ref.shape, F32)
    y_ref[...] = jnp.zeros(y_ref.shape, F32)

    def kv_chunk(c, carry):
        rows = pl.ds(pl.multiple_of(c * ch, ch), ch)
        for h, hs in heads:
            s_ref[h] = jnp.dot(k_ref[0, h, rows, :], qt_ref[0, hs, :],
                               preferred_element_type=F32)
        for h, hs in heads:
            s = s_ref[h] + bias_ref[rows, :]
            m = m_ref[h:h + 1, :]
            m_new = jnp.maximum(m, jnp.max(s, axis=0, keepdims=True))
            alpha = jnp.exp(m - m_new)
            p = jnp.exp(s - m_new)
            m_ref[h:h + 1, :] = m_new
            l_ref[h:h + 1, :] = alpha * l_ref[h:h + 1, :] + jnp.sum(p, axis=0, keepdims=True)
            y_ref[hs, :] = alpha * y_ref[hs, :] + jnp.dot(
                vt_ref[0, c, hs, :], p.astype(BF16), preferred_element_type=F32)
        return carry

    lax.fori_loop(0, nc, kv_chunk, 0)
    for h, hs in heads:
        y_ref[hs, :] = y_ref[hs, :] / l_ref[h:h + 1, :]
    out_ref[0] = _rms_gain(y_ref[...].T, gn_ref[...]).astype(BF16)


def _attn_prompt(qib, wi, qb, kidxb, kb, vb, gn_att, *, tq, topk, n_heads, idx_dim):
    bsz, t, d_att = qb.shape
    head_dim = d_att // n_heads
    ch = tq
    idx_bits = max(1, int(t - 1).bit_length())
    qit, qt, wt = (jnp.swapaxes(a, 1, 2) for a in (qib, qb, wi))
    k_hm = kb.reshape(bsz, t, n_heads, head_dim).transpose(0, 2, 1, 3)
    vt = vb.reshape(bsz, t // ch, ch, d_att).transpose(0, 1, 3, 2)
    cols = lambda a: pl.BlockSpec((1, a.shape[1], tq), lambda b, i: (b, 0, i))
    full = lambda a: pl.BlockSpec((1,) + a.shape[1:], lambda b, i: (b,) + (0,) * (a.ndim - 1))
    body = functools.partial(
        _attn_prompt_body, tq=tq, ch=ch, topk=topk, idx_bits=idx_bits, n_heads=n_heads,
        head_dim=head_dim, idx_heads=wi.shape[2], idx_dim=idx_dim)
    return pl.pallas_call(
        body,
        grid=(bsz, t // tq),
        in_specs=[cols(qit), cols(wt), cols(qt), full(kidxb), full(k_hm), full(vt),
                  pl.BlockSpec((1, d_att), lambda b, i: (0, 0))],
        out_specs=pl.BlockSpec((1, tq, d_att), lambda b, i: (b, i, 0)),
        out_shape=jax.ShapeDtypeStruct((bsz, t, d_att), BF16),
        scratch_shapes=[pltpu.VMEM((t, tq), I32), pltpu.VMEM((t, tq), F32),
                        pltpu.VMEM((d_att, tq), F32), pltpu.VMEM((1, tq), I32),
                        pltpu.VMEM((n_heads, tq), F32), pltpu.VMEM((n_heads, tq), F32),
                        pltpu.VMEM((n_heads, ch, tq), F32)],
        compiler_params=_params("parallel", "arbitrary"),
        name="attn_prompt",
    )(qit, wt, qt, kidxb, k_hm, vt, gn_att)


def _score_sample_body(pt_ref, qi_ref, w_ref, *refs, n_pages, page, t_len, idx_heads, past_len):
    del pt_ref
    page_refs, new_ref, out_ref = refs[:n_pages], refs[n_pages], refs[n_pages + 1]
    qi = qi_ref[0]
    w = w_ref[0]
    for p in range(n_pages + 1):
        kc = page_refs[p][0].astype(BF16) if p < n_pages else new_ref[0]
        d = jnp.maximum(_nt_dot(qi, kc), 0.0) * w
        s = jnp.sum(d.reshape(t_len, idx_heads, page), axis=1)
        if p == n_pages:
            tpos = lax.broadcasted_iota(I32, (t_len, page), 0)
            kpos = lax.broadcasted_iota(I32, (t_len, page), 1)
            s = jnp.where(kpos <= tpos, s, -jnp.inf)
        out_ref[0, :, p * page:(p + 1) * page] = s


def _select_sample_body(s_ref, bias_ref, key_ref, pos_ref, *, topk, idx_bits, page):
    nc, rows, _ = key_ref.shape
    for c in range(nc):
        key_ref[c] = _sortable_key(s_ref[:, c * page:(c + 1) * page])
    lane = lax.broadcasted_iota(I32, (rows, page), 1)

    def count(pred):
        def body(c, acc):
            return acc + jnp.where(pred(key_ref[c], c * page + lane), 1, 0).astype(I32)
        acc = lax.fori_loop(0, nc, body, jnp.zeros((rows, page), I32))
        return jnp.sum(acc, axis=1, keepdims=True)

    theta = _threshold_search(count, nc * page, topk, idx_bits, pos_ref)
    pos = pos_ref[...]
    for c in range(nc):
        s = s_ref[:, c * page:(c + 1) * page]
        sel = _selected(key_ref[c], c * page + lane, theta, pos) & (s > -jnp.inf)
        bias_ref[:, c * page:(c + 1) * page] = jnp.where(sel, 0.0, MASKED)


def _attn_sample_body(pt_ref, q_ref, bias_ref, gn_ref, *refs, n_pages, page, t_len, n_heads,
                      head_dim):
    del pt_ref
    k_refs, v_refs = refs[:n_pages], refs[n_pages:2 * n_pages]
    knew_ref, vnew_ref, out_ref = refs[2 * n_pages:]
    past = n_pages * page
    bias = bias_ref[0]
    outs = []
    for h in range(n_heads):
        head_rows = pl.ds(h, page, stride=n_heads)
        kh = jnp.concatenate([r[0, head_rows, :] for r in k_refs], axis=0).astype(BF16)
        vh = jnp.concatenate([r[0, head_rows, :] for r in v_refs], axis=0).astype(BF16)
        qh = q_ref[0, h]
        s_past = _nt_dot(qh, kh) + bias[:, :past]
        s_new = _nt_dot(qh, knew_ref[0, h]) + bias[:, past:]
        m = jnp.maximum(jnp.max(s_past, axis=1, keepdims=True),
                        jnp.max(s_new, axis=1, keepdims=True))
        p_past = jnp.exp(s_past - m)
        p_new = jnp.exp(s_new - m)
        l = jnp.sum(p_past, axis=1, keepdims=True) + jnp.sum(p_new, axis=1, keepdims=True)
        o = (jnp.dot(p_past.astype(BF16), vh, preferred_element_type=F32)
             + jnp.dot(p_new.astype(BF16), vnew_ref[0, h], preferred_element_type=F32))
        outs.append(o / l)
    y = jnp.concatenate(outs, axis=1)
    out_ref[0] = _rms_gain(y, gn_ref[...]).astype(BF16)


def _attn_sample(qib, wi, qb, kidx_new, k_new, v_new, cache_k, cache_v, cache_kidx, page_table,
                 gn_att, *, topk, n_heads):
    dbs, t_len, d_att = qb.shape
    n_pages = page_table.shape[1]
    page = cache_k.shape[1]
    idx_dim = cache_kidx.shape[2]
    idx_heads = wi.shape[2]
    nkp = (n_pages + 1) * page
    past_len = n_pages * page
    pad = lambda a: jnp.pad(a, ((0, 0), (0, page - t_len), (0, 0)))

    def paged(width):
        return [pl.BlockSpec((1, page, width), functools.partial(
            lambda b, pt, p: (pt[b, p], 0, 0), p=p)) for p in range(n_pages)]

    per_seq = lambda shape: pl.BlockSpec((1,) + shape, lambda b, pt: (b, 0, 0))

    scores = pl.pallas_call(
        functools.partial(_score_sample_body, n_pages=n_pages, page=page, t_len=t_len,
                          idx_heads=idx_heads, past_len=past_len),
        grid_spec=pltpu.PrefetchScalarGridSpec(
            num_scalar_prefetch=1, grid=(dbs,),
            in_specs=[per_seq((t_len * idx_heads, idx_dim)), per_seq((t_len * idx_heads, 1))]
                     + paged(idx_dim) + [per_seq((page, idx_dim))],
            out_specs=per_seq((t_len, nkp))),
        out_shape=jax.ShapeDtypeStruct((dbs, t_len, nkp), F32),
        compiler_params=_params("parallel"),
        name="score_sample",
    )(page_table, qib.reshape(dbs, t_len * idx_heads, idx_dim),
      wi.reshape(dbs, t_len * idx_heads, 1), *([cache_kidx] * n_pages), pad(kidx_new))

    rows = dbs * t_len
    tr = min(rows, 128)
    bias = pl.pallas_call(
        functools.partial(_select_sample_body, topk=topk,
                          idx_bits=max(1, int(nkp - 1).bit_length()), page=page),
        grid=(rows // tr,),
        in_specs=[pl.BlockSpec((tr, nkp), lambda r: (r, 0))],
        out_specs=pl.BlockSpec((tr, nkp), lambda r: (r, 0)),
        out_shape=jax.ShapeDtypeStruct((rows, nkp), F32),
        scratch_shapes=[pltpu.VMEM((n_pages + 1, tr, page), I32), pltpu.VMEM((tr, 1), I32)],
        compiler_params=_params("parallel"),
        name="select_sample",
    )(scores.reshape(rows, nkp)).reshape(dbs, t_len, nkp)

    head_dim = d_att // n_heads
    head_major = lambda a: a.reshape(dbs, -1, n_heads, head_dim).transpose(0, 2, 1, 3)
    flat_pages = lambda c: c.reshape(c.shape[0], page * n_heads, head_dim)
    kv_pages = [pl.BlockSpec((1, page * n_heads, head_dim), functools.partial(
        lambda b, pt, p: (pt[b, p], 0, 0), p=p)) for p in range(n_pages)]
    per_seq4 = lambda rows: pl.BlockSpec((1, n_heads, rows, head_dim),
                                         lambda b, pt: (b, 0, 0, 0))
    return pl.pallas_call(
        functools.partial(_attn_sample_body, n_pages=n_pages, page=page, t_len=t_len,
                          n_heads=n_heads, head_dim=head_dim),
        grid_spec=pltpu.PrefetchScalarGridSpec(
            num_scalar_prefetch=1, grid=(dbs,),
            in_specs=[per_seq4(t_len), per_seq((t_len, nkp)),
                      pl.BlockSpec((1, d_att), lambda b, pt: (0, 0))]
                     + kv_pages + kv_pages + [per_seq4(page), per_seq4(page)],
            out_specs=per_seq((t_len, d_att))),
        out_shape=jax.ShapeDtypeStruct((dbs, t_len, d_att), BF16),
        compiler_params=_params("parallel"),
        name="attn_sample",
    )(page_table, head_major(qb), bias, gn_att, *([flat_pages(cache_k)] * n_pages),
      *([flat_pages(cache_v)] * n_pages), head_major(pad(k_new)), head_major(pad(v_new)))


def _layer_norm(x, g, b):
    mu = jnp.mean(x, axis=-1, keepdims=True)
    xc = x - mu
    var = jnp.mean(xc * xc, axis=-1, keepdims=True)
    return xc * lax.rsqrt(var + LN_EPS) * g + b


def _finish_body(x_ref, ml_ref, ma_ref, wo_ref, wfi_ref, wfo_ref, p_ref, out_ref,
                 *, alpha, d_lru, d_ff, fc):
    ln1_g, ln1_b, ln2_g, ln2_b = (p_ref[r:r + 1, :] for r in range(4))
    y = (jnp.dot(ml_ref[...], wo_ref[0:d_lru, :], preferred_element_type=F32)
         + jnp.dot(ma_ref[...], wo_ref[d_lru:, :], preferred_element_type=F32))
    x1 = _layer_norm(alpha * x_ref[...] + y, ln1_g, ln1_b)
    x1b = x1.astype(BF16)
    f = jnp.zeros(x1.shape, F32)
    for c in range(d_ff // fc):
        u = jnp.dot(x1b, wfi_ref[:, c * fc:(c + 1) * fc], preferred_element_type=F32)
        g = jnp.dot(x1b, wfi_ref[:, d_ff + c * fc:d_ff + (c + 1) * fc],
                    preferred_element_type=F32)
        hidden = (g * jax.nn.sigmoid(g) * u).astype(BF16)
        f = f + jnp.dot(hidden, wfo_ref[c * fc:(c + 1) * fc, :], preferred_element_type=F32)
    out_ref[...] = _layer_norm(alpha * x1 + f, ln2_g, ln2_b)


def _finish(x2d, mix_lru, mix_att, wo, wfi, wfo, pvec, *, alpha, tm):
    n, d_model = x2d.shape
    d_lru = mix_lru.shape[1]
    d_ff = wfo.shape[0]
    fc = 2 * LANES if d_ff % (2 * LANES) == 0 else LANES
    row = lambda w: pl.BlockSpec((tm, w), lambda i: (i, 0))
    const = lambda a: pl.BlockSpec(a.shape, lambda i: (0, 0), pipeline_mode=pl.Buffered(1))
    return pl.pallas_call(
        functools.partial(_finish_body, alpha=alpha, d_lru=d_lru, d_ff=d_ff, fc=fc),
        grid=(n // tm,),
        in_specs=[row(d_model), row(d_lru), row(mix_att.shape[1]),
                  const(wo), const(wfi), const(wfo), const(pvec)],
        out_specs=row(d_model),
        out_shape=jax.ShapeDtypeStruct((n, d_model), F32),
        compiler_params=_params("parallel"),
        name="finish",
    )(x2d, mix_lru, mix_att, wo, wfi, wfo, pvec)


def _block_diag(w):
    nb, bi, bo = w.shape
    eye = jnp.eye(nb, dtype=w.dtype)
    return (w[:, :, None, :] * eye[:, None, :, None]).reshape(nb * bi, nb * bo)


def _row_tile(n, want):
    tm = min(n, want)
    while n % tm:
        tm //= 2
    return tm


def kernel(x_prompt, x_sample, cache_k, cache_v, cache_kidx, state_conv, state_h, page_table,
           w_in, conv_w, conv_b, w_a, b_a, w_x, b_x, lam, gn_lru, gn_att, w_out,
           ln1_g, ln1_b, w_ffn_in, w_ffn_out, ln2_g, ln2_b):
    depth, d_model, d_in = w_in.shape
    bsz, seq, _ = x_prompt.shape
    dbs, dseq, _ = x_sample.shape
    d_lru = conv_w.shape[2]
    n_phys, page, n_heads, head_dim = cache_k.shape[1:]
    d_att = n_heads * head_dim
    idx_dim = cache_kidx.shape[3]
    d_qi = IDX_HEADS * idx_dim
    n_pages = page_table.shape[1]
    past_len = n_pages * page
    alpha = (2.0 * depth) ** 0.25
    assert d_in == 2 * d_lru + 3 * d_att + d_qi + idx_dim + IDX_HEADS
    assert idx_dim + IDX_HEADS <= LANES and LANES % head_dim == 0 and head_dim == idx_dim
    geom = dict(d_lru=d_lru, d_att=d_att, d_qi=d_qi, idx_dim=idx_dim, head_dim=head_dim)
    d_main = d_in - idx_dim - IDX_HEADS

    pos_p = jnp.tile(jnp.arange(seq), bsz)
    pos_s = jnp.tile(past_len + jnp.arange(dseq), dbs)
    topk_p = min(TOPK_MAX, seq // 4)
    topk_s = min(TOPK_MAX, (past_len + dseq) // 4)

    xp = x_prompt.reshape(bsz * seq, d_model)
    xs = x_sample.reshape(dbs * dseq, d_model)
    outs_p, outs_s = [], []
    for l in range(depth):
        w_pad = jnp.pad(w_in[l], ((0, 0), (0, d_main + LANES - d_in))).astype(BF16)
        wa_bd = _block_diag(w_a[l]).astype(BF16)
        wx_bd = _block_diag(w_x[l]).astype(BF16)
        lru_vec = jnp.stack([conv_b[l], b_a[l], b_x[l], lam[l], gn_lru[l]]
                            + [jnp.zeros_like(lam[l])] * 3)
        fin_vec = jnp.stack([ln1_g[l], ln1_b[l], ln2_g[l], ln2_b[l]])
        wo, wfi, wfo = (w_out[l].astype(BF16), w_ffn_in[l].astype(BF16),
                        w_ffn_out[l].astype(BF16))
        gn_a = gn_att[l][None, :]

        xl, gate, qb, k, kb, v, vb, qib, tail = _project(
            xp, w_pad, pos_p, tm=_row_tile(bsz * seq, 512), **geom)
        b3 = lambda a: a.reshape(bsz, seq, a.shape[-1])
        xl3 = b3(xl)
        mix_l, h_last = _lru_prompt(
            xl3, b3(gate), jnp.zeros((bsz, CONV_W - 1, d_lru), F32), jnp.zeros((bsz, d_lru), F32),
            conv_w[l], lru_vec, wa_bd, wx_bd, tt=_row_tile(seq, 256))
        tail3 = b3(tail)
        ki = tail3[:, :, :idx_dim]
        mix_a = _attn_prompt(b3(qib), tail3[:, :, idx_dim:idx_dim + IDX_HEADS], b3(qb),
                             ki.astype(BF16), b3(kb), b3(vb), gn_a, tq=_row_tile(seq, 256),
                             topk=topk_p, n_heads=n_heads, idx_dim=idx_dim)
        xp = _finish(xp, mix_l.reshape(bsz * seq, d_lru), mix_a.reshape(bsz * seq, d_att),
                     wo, wfi, wfo, fin_vec, alpha=alpha, tm=_row_tile(bsz * seq, 512))
        outs_p.append((k.reshape(bsz, seq, n_heads, head_dim), v.reshape(bsz, seq, n_heads, head_dim),
                       ki, xl3[:, seq - (CONV_W - 1):], h_last[:, 0]))

        xl, gate, qb, k, kb, v, vb, qib, tail = _project(
            xs, w_pad, pos_s, tm=_row_tile(dbs * dseq, 256), **geom)
        d3 = lambda a: a.reshape(dbs, dseq, a.shape[-1])
        tm_major = lambda a: jnp.swapaxes(d3(a), 0, 1)
        xl3 = d3(xl)
        mix_l, h_last = _lru_sample(
            tm_major(xl), tm_major(gate), jnp.swapaxes(state_conv[l], 0, 1).astype(F32),
            state_h[l], conv_w[l], lru_vec, wa_bd, wx_bd)
        tail3 = d3(tail)
        ki = tail3[:, :, :idx_dim]
        wi = tail3[:, :, idx_dim:idx_dim + IDX_HEADS]
        mix_a = _attn_sample(
            d3(qib), wi, d3(qb), ki.astype(BF16), d3(kb), d3(vb),
            cache_k[l], cache_v[l], cache_kidx[l], page_table, gn_a, topk=topk_s,
            n_heads=n_heads)
        xs = _finish(xs, jnp.swapaxes(mix_l, 0, 1).reshape(dbs * dseq, d_lru),
                     mix_a.reshape(dbs * dseq, d_att), wo, wfi, wfo, fin_vec, alpha=alpha,
                     tm=_row_tile(dbs * dseq, 256))
        conv_new = jnp.concatenate([state_conv[l].astype(F32), xl3], axis=1)[:, -(CONV_W - 1):]
        outs_s.append((k.reshape(dbs, dseq, n_heads, head_dim), v.reshape(dbs, dseq, n_heads, head_dim),
                       ki, conv_new, h_last))

    stack = lambda outs, j: jnp.stack([o[j] for o in outs])
    return (xp.reshape(bsz, seq, d_model), xs.reshape(dbs, dseq, d_model),
            *(stack(outs_p, j) for j in range(5)), *(stack(outs_s, j) for j in range(5)))
```

```python
import functools

import jax
import jax.numpy as jnp
import numpy as np
from jax import lax
from jax.experimental import pallas as pl
from jax.experimental.pallas import tpu as pltpu

CONV_W = 4
LRU_C = 8.0
LRU_BLOCKS = 8
IDX_HEADS = 8
TOPK_MAX = 256
ROPE_FRACTION = 4
ROPE_THETA = 500000.0
RMS_EPS = 1e-6
LN_EPS = 1e-5

LANES = 128
SUBLANES = 8
VMEM_LIMIT = 56 * 1024 * 1024
MASKED = -1e30
INT_MIN = -2 ** 31

F32 = jnp.float32
BF16 = jnp.bfloat16
I32 = jnp.int32


def _params(*sem):
    return pltpu.CompilerParams(dimension_semantics=sem, vmem_limit_bytes=VMEM_LIMIT)


def _nt_dot(a, b):
    return lax.dot_general(a, b, (((1,), (1,)), ((), ())), preferred_element_type=F32)


def _proj_body(x_ref, w_ref, c_ref, s1_ref, s2_ref,
               xl_ref, gate_ref, qb_ref, k_ref, kb_ref, v_ref, vb_ref, qib_ref, tail_ref,
               *, d_lru, d_att, d_qi, idx_dim, rope_half, q_scale):
    xb = x_ref[...].astype(BF16)
    c, s1, s2 = c_ref[...], s1_ref[...], s2_ref[...]

    def proj(lo, width):
        return jnp.dot(xb, w_ref[:, lo:lo + width], preferred_element_type=F32)

    def tiled(t, width):
        reps = width // LANES
        return t if reps == 1 else jnp.concatenate([t] * reps, axis=1)

    def rope(z, cc, ss1, ss2):
        width = z.shape[1]
        return (z * cc + pltpu.roll(z, width - rope_half, 1) * ss1
                + pltpu.roll(z, rope_half, 1) * ss2)

    lo = 0
    xl_ref[...] = proj(lo, d_lru); lo += d_lru
    gate_ref[...] = proj(lo, d_lru); lo += d_lru
    q = rope(proj(lo, d_att), tiled(c, d_att), tiled(s1, d_att), tiled(s2, d_att)); lo += d_att
    qb_ref[...] = (q * q_scale).astype(BF16)
    k = rope(proj(lo, d_att), tiled(c, d_att), tiled(s1, d_att), tiled(s2, d_att)); lo += d_att
    k_ref[...] = k
    kb_ref[...] = k.astype(BF16)
    v = proj(lo, d_att); lo += d_att
    v_ref[...] = v
    vb_ref[...] = v.astype(BF16)
    qi = rope(proj(lo, d_qi), tiled(c, d_qi), tiled(s1, d_qi), tiled(s2, d_qi)); lo += d_qi
    qib_ref[...] = qi.astype(BF16)
    tail = proj(lo, LANES)
    is_key = lax.broadcasted_iota(I32, tail.shape, 1) < idx_dim
    tail_ref[...] = rope(tail, jnp.where(is_key, c, 1.0), jnp.where(is_key, s1, 0.0),
                         jnp.where(is_key, s2, 0.0))


def _rope_tables(pos, head_dim):
    rope_dim = head_dim // ROPE_FRACTION
    half = rope_dim // 2
    freqs = ROPE_THETA ** (-jnp.arange(half, dtype=F32) / half)
    ang = pos.astype(F32)[:, None] * freqs[None, :]
    cos, sin = jnp.cos(ang), jnp.sin(ang)
    n = pos.shape[0]
    rest = head_dim - rope_dim
    c = jnp.concatenate([cos, cos, jnp.ones((n, rest), F32)], 1)
    s1 = jnp.concatenate([-sin, jnp.zeros((n, half + rest), F32)], 1)
    s2 = jnp.concatenate([jnp.zeros((n, half), F32), sin, jnp.zeros((n, rest), F32)], 1)
    reps = LANES // head_dim
    return [jnp.tile(t, (1, reps)) for t in (c, s1, s2)], half


def _project(x2d, w_pad, pos, *, d_lru, d_att, d_qi, idx_dim, head_dim, tm):
    n, d_model = x2d.shape
    (c, s1, s2), half = _rope_tables(pos, head_dim)
    row = lambda w: pl.BlockSpec((tm, w), lambda i: (i, 0))
    outs = [
        (d_lru, F32), (d_lru, F32), (d_att, BF16), (d_att, F32), (d_att, BF16),
        (d_att, F32), (d_att, BF16), (d_qi, BF16), (LANES, F32)]
    body = functools.partial(_proj_body, d_lru=d_lru, d_att=d_att, d_qi=d_qi, idx_dim=idx_dim,
                             rope_half=half, q_scale=head_dim ** -0.5)
    return pl.pallas_call(
        body,
        grid=(n // tm,),
        in_specs=[row(d_model), pl.BlockSpec(w_pad.shape, lambda i: (0, 0)),
                  row(LANES), row(LANES), row(LANES)],
        out_specs=[row(w) for w, _ in outs],
        out_shape=[jax.ShapeDtypeStruct((n, w), dt) for w, dt in outs],
        compiler_params=_params("parallel"),
        name="proj",
    )(x2d, w_pad, c, s1, s2)


def _softplus(x):
    return jnp.maximum(x, 0.0) + jnp.log1p(jnp.exp(-jnp.abs(x)))


def _gelu_tanh(x):
    return 0.5 * x * (1.0 + jnp.tanh(np.sqrt(2.0 / np.pi).astype(np.float32)
                                     * (x + 0.044715 * (x * x * x))))


def _lru_gates(xc, wa_ref, wx_ref, b_a, b_x, lam):
    xcb = xc.astype(BF16)
    r = jax.nn.sigmoid(jnp.dot(xcb, wa_ref[...], preferred_element_type=F32) + b_a)
    i = jax.nn.sigmoid(jnp.dot(xcb, wx_ref[...], preferred_element_type=F32) + b_x)
    log_a = -LRU_C * r * _softplus(-lam)
    a = jnp.exp(log_a)
    t = jnp.tanh(log_a)
    b = jnp.sqrt(-2.0 * t / (1.0 - t)) * (i * xc)
    return a, b


def _rms_gain(y, g):
    return y * lax.rsqrt(jnp.mean(y * y, axis=-1, keepdims=True) + RMS_EPS) * g


def _lru_prompt_body(xl_ref, gate_ref, cprev_ref, h0_ref, cw_ref, p_ref, wa_ref, wx_ref,
                     mix_ref, hlast_ref, ext_ref, hc_ref, *, tt):
    j = pl.program_id(1)

    @pl.when(j == 0)
    def _():
        ext_ref[0:SUBLANES, :] = cprev_ref[0]
        hc_ref[0:1, :] = h0_ref[0]

    xl = xl_ref[0]
    ext_ref[SUBLANES:SUBLANES + tt, :] = xl
    conv_b, b_a, b_x, lam, gn = (p_ref[r:r + 1, :] for r in range(5))
    xc = conv_b + (cw_ref[0:1, :] * ext_ref[SUBLANES - 3:SUBLANES - 3 + tt, :]
                   + cw_ref[1:2, :] * ext_ref[SUBLANES - 2:SUBLANES - 2 + tt, :]
                   + cw_ref[2:3, :] * ext_ref[SUBLANES - 1:SUBLANES - 1 + tt, :]
                   + cw_ref[3:4, :] * xl)
    ext_ref[0:SUBLANES, :] = ext_ref[tt:tt + SUBLANES, :]

    a, b = _lru_gates(xc, wa_ref, wx_ref, b_a, b_x, lam)
    row = lax.broadcasted_iota(I32, a.shape, 0)
    d = 1
    while d < tt:
        keep = row >= d
        a_prev = jnp.where(keep, pltpu.roll(a, d, 0), 1.0)
        b_prev = jnp.where(keep, pltpu.roll(b, d, 0), 0.0)
        b = a * b_prev + b
        a = a * a_prev
        d *= 2
    h = a * hc_ref[0:1, :] + b
    hc_ref[0:1, :] = h[tt - 1:tt, :]
    hlast_ref[0] = h[tt - 1:tt, :]
    mix_ref[0] = _rms_gain(h * _gelu_tanh(gate_ref[0]), gn).astype(BF16)


def _lru_prompt(xl, gate, conv_prev, h0, conv_w, pvec, wa_bd, wx_bd, *, tt):
    bsz, t, d = xl.shape
    cprev8 = jnp.concatenate(
        [jnp.zeros((bsz, SUBLANES - (CONV_W - 1), d), F32), conv_prev.astype(F32)], axis=1)
    const = lambda shape: pl.BlockSpec(shape, lambda b, j: (0,) * len(shape))
    return pl.pallas_call(
        functools.partial(_lru_prompt_body, tt=tt),
        grid=(bsz, t // tt),
        in_specs=[pl.BlockSpec((1, tt, d), lambda b, j: (b, j, 0)),
                  pl.BlockSpec((1, tt, d), lambda b, j: (b, j, 0)),
                  pl.BlockSpec((1, SUBLANES, d), lambda b, j: (b, 0, 0)),
                  pl.BlockSpec((1, 1, d), lambda b, j: (b, 0, 0)),
                  const(conv_w.shape), const(pvec.shape), const(wa_bd.shape), const(wx_bd.shape)],
        out_specs=[pl.BlockSpec((1, tt, d), lambda b, j: (b, j, 0)),
                   pl.BlockSpec((1, 1, d), lambda b, j: (b, 0, 0))],
        out_shape=[jax.ShapeDtypeStruct((bsz, t, d), BF16),
                   jax.ShapeDtypeStruct((bsz, 1, d), F32)],
        scratch_shapes=[pltpu.VMEM((tt + SUBLANES, d), F32), pltpu.VMEM((SUBLANES, d), F32)],
        compiler_params=_params("parallel", "arbitrary"),
        name="lru_prompt",
    )(xl, gate, cprev8, h0.astype(F32)[:, None, :], conv_w, pvec, wa_bd, wx_bd)


def _lru_sample_body(xl_ref, gate_ref, cprev_ref, h0_ref, cw_ref, p_ref, wa_ref, wx_ref,
                     mix_ref, hlast_ref, *, t_len):
    conv_b, b_a, b_x, lam, gn = (p_ref[r:r + 1, :] for r in range(5))
    xp = [cprev_ref[s] for s in range(CONV_W - 1)] + [xl_ref[s] for s in range(t_len)]
    h = h0_ref[...]
    for s in range(t_len):
        xc = conv_b + (cw_ref[0:1, :] * xp[s] + cw_ref[1:2, :] * xp[s + 1]
                       + cw_ref[2:3, :] * xp[s + 2] + cw_ref[3:4, :] * xp[s + 3])
        a, b = _lru_gates(xc, wa_ref, wx_ref, b_a, b_x, lam)
        h = a * h + b
        mix_ref[s] = _rms_gain(h * _gelu_tanh(gate_ref[s]), gn).astype(BF16)
    hlast_ref[...] = h


def _lru_sample(xl_t, gate_t, cprev_t, h0, conv_w, pvec, wa_bd, wx_bd):
    t_len, dbs, d = xl_t.shape
    return pl.pallas_call(
        functools.partial(_lru_sample_body, t_len=t_len),
        out_shape=[jax.ShapeDtypeStruct((t_len, dbs, d), BF16),
                   jax.ShapeDtypeStruct((dbs, d), F32)],
        compiler_params=pltpu.CompilerParams(vmem_limit_bytes=VMEM_LIMIT),
        name="lru_sample",
    )(xl_t, gate_t, cprev_t, h0.astype(F32), conv_w, pvec, wa_bd, wx_bd)


KEY_NEG_INF = INT_MIN + 0x7FFFFF
REFINE_STEPS = 16


def _key_to_float(key):
    return pltpu.bitcast(jnp.where(key >= 0, key, key ^ 0x7FFFFFFF), F32)


def _threshold_search(count, total, topk, idx_bits, theta_ref, pos_ref):
    shape = pos_ref.shape

    def value_step(it, carry):
        base, n_base = carry
        trial = base ^ lax.shift_left(jnp.int32(1), jnp.int32(31) - it)
        trial_f = _key_to_float(trial)
        n = count(lambda s, idx: s >= trial_f)
        ok = n >= topk
        return jnp.where(ok, trial, base), jnp.where(ok, n, n_base)

    theta_key, n_ge = lax.fori_loop(
        0, 32, value_step, (jnp.full(shape, INT_MIN, I32), jnp.zeros(shape, I32) + total))
    theta = jnp.where(theta_key < KEY_NEG_INF, -jnp.inf, _key_to_float(theta_key))
    theta_ref[...] = theta
    pos_ref[...] = jnp.full(shape, 2 ** idx_bits - 1, I32)

    @pl.when(jnp.max(n_ge) > topk)
    def _():
        def refine_step(_, carry):
            lo, hi = carry
            mid = lo + 0.5 * (hi - lo)
            ok = count(lambda s, idx: s >= mid) >= topk
            return jnp.where(ok, mid, lo), jnp.where(ok, hi, mid)

        theta_fine, _ = lax.fori_loop(
            0, REFINE_STEPS, refine_step,
            (theta, _key_to_float(jnp.maximum(theta_key, KEY_NEG_INF) + 1)))
        theta_ref[...] = theta_fine
        need = topk - count(lambda s, idx: s > theta_fine)

        def index_step(it, pos):
            trial = pos + lax.shift_left(jnp.int32(1), jnp.int32(idx_bits - 1) - it)
            below = count(lambda s, idx: (s == theta_fine) & (idx < trial))
            return jnp.where(below < need, trial, pos)

        pos_ref[...] = lax.fori_loop(0, idx_bits, index_step, jnp.zeros(shape, I32))


def _selected(score, idx, theta, pos):
    return (score > theta) | ((score == theta) & (idx <= pos))


def _attn_prompt_body(qit_ref, wt_ref, qt_ref, kidx_ref, k_ref, vt_ref, gn_ref, out_ref,
                      sc_ref, y_ref, theta_ref, pos_ref, m_ref, l_ref, s_ref, *, tq, ch, topk,
                      idx_bits, n_heads, head_dim, idx_heads, idx_dim):
    i = pl.program_id(1)
    n_keys = (i + 1) * tq
    qpos = i * tq + lax.broadcasted_iota(I32, (1, tq), 1)
    sub = LANES
    kpos_sub = lax.broadcasted_iota(I32, (sub, tq), 0)
    kpos_ch = lax.broadcasted_iota(I32, (ch, tq), 0)
    nc = (i + 1) * (tq // ch)

    def score_chunk(c, carry):
        for j in range(ch // sub):
            start = pl.multiple_of(c * ch + j * sub, sub)
            rows = pl.ds(start, sub)
            kc = kidx_ref[0, rows, :]
            acc = jnp.zeros((sub, tq), F32)
            for h in range(idx_heads):
                d = jnp.dot(kc, qit_ref[0, h * idx_dim:(h + 1) * idx_dim, :],
                            preferred_element_type=F32)
                acc = acc + jnp.maximum(d, 0.0) * wt_ref[0, h:h + 1, :]
            sc_ref[rows, :] = jnp.where(start + kpos_sub <= qpos, acc, -jnp.inf)
        return carry

    lax.fori_loop(0, nc, score_chunk, 0)

    def count(pred):
        def body(c, acc):
            rows = pl.ds(pl.multiple_of(c * ch, ch), ch)
            hit = jnp.where(pred(sc_ref[rows, :], c * ch + kpos_ch), 1, 0).astype(I32)
            return acc + jnp.sum(hit.reshape(ch // SUBLANES, SUBLANES, tq), axis=0)
        acc = lax.fori_loop(0, nc, body, jnp.zeros((SUBLANES, tq), I32))
        return jnp.sum(acc, axis=0, keepdims=True)

    _threshold_search(count, n_keys, topk, idx_bits, theta_ref, pos_ref)
    theta, pos = theta_ref[...], pos_ref[...]

    def bias_chunk(c, carry):
        rows = pl.ds(pl.multiple_of(c * ch, ch), ch)
        idx = c * ch + kpos_ch
        sel = _selected(sc_ref[rows, :], idx, theta, pos) & (idx <= qpos)
        sc_ref[rows, :] = jnp.where(sel, 0.0, MASKED)
        return carry

    lax.fori_loop(0, nc, bias_chunk, 0)

    heads = [(h, slice(h * head_dim, (h + 1) * head_dim)) for h in range(n_heads)]
    m_ref[...] = jnp.full(m_ref.shape, -jnp.inf, F32)
    l_ref[...] = jnp.zeros(l_ref.shape, F32)
    y_ref[...] = jnp.zeros(y_ref.shape, F32)

    def kv_chunk(c, carry):
        rows = pl.ds(pl.multiple_of(c * ch, ch), ch)
        for h, hs in heads:
            s_ref[h] = jnp.dot(k_ref[0, h, rows, :], qt_ref[0, hs, :],
                               preferred_element_type=F32)
        for h, hs in heads:
            s = s_ref[h] + sc_ref[rows, :]
            m = m_ref[h:h + 1, :]
            m_new = jnp.maximum(m, jnp.max(s, axis=0, keepdims=True))
            alpha = jnp.exp(m - m_new)
            p = jnp.exp(s - m_new)
            m_ref[h:h + 1, :] = m_new
            l_ref[h:h + 1, :] = alpha * l_ref[h:h + 1, :] + jnp.sum(p, axis=0, keepdims=True)
            y_ref[hs, :] = alpha * y_ref[hs, :] + jnp.dot(
                vt_ref[0, c, hs, :], p.astype(BF16), preferred_element_type=F32)
        return carry

    lax.fori_loop(0, nc, kv_chunk, 0)
    for h, hs in heads:
        y_ref[hs, :] = y_ref[hs, :] / l_ref[h:h + 1, :]
    out_ref[0] = _rms_gain(y_ref[...].T, gn_ref[...]).astype(BF16)


def _attn_prompt(qib, wi, qb, kidxb, kb, vb, gn_att, *, tq, topk, n_heads, idx_dim):
    bsz, t, d_att = qb.shape
    head_dim = d_att // n_heads
    ch = tq
    idx_bits = max(1, int(t - 1).bit_length())
    qit, qt, wt = (jnp.swapaxes(a, 1, 2) for a in (qib, qb, wi))
    k_hm = kb.reshape(bsz, t, n_heads, head_dim).transpose(0, 2, 1, 3)
    vt = vb.reshape(bsz, t // ch, ch, d_att).transpose(0, 1, 3, 2)
    cols = lambda a: pl.BlockSpec((1, a.shape[1], tq), lambda b, i: (b, 0, i))
    full = lambda a: pl.BlockSpec((1,) + a.shape[1:], lambda b, i: (b,) + (0,) * (a.ndim - 1))
    body = functools.partial(
        _attn_prompt_body, tq=tq, ch=ch, topk=topk, idx_bits=idx_bits, n_heads=n_heads,
        head_dim=head_dim, idx_heads=wi.shape[2], idx_dim=idx_dim)
    return pl.pallas_call(
        body,
        grid=(bsz, t // tq),
        in_specs=[cols(qit), cols(wt), cols(qt), full(kidxb), full(k_hm), full(vt),
                  pl.BlockSpec((1, d_att), lambda b, i: (0, 0))],
        out_specs=pl.BlockSpec((1, tq, d_att), lambda b, i: (b, i, 0)),
        out_shape=jax.ShapeDtypeStruct((bsz, t, d_att), BF16),
        scratch_shapes=[pltpu.VMEM((t, tq), F32), pltpu.VMEM((d_att, tq), F32),
                        pltpu.VMEM((1, tq), F32), pltpu.VMEM((1, tq), I32),
                        pltpu.VMEM((n_heads, tq), F32), pltpu.VMEM((n_heads, tq), F32),
                        pltpu.VMEM((n_heads, ch, tq), F32)],
        compiler_params=_params("parallel", "arbitrary"),
        name="attn_prompt",
    )(qit, wt, qt, kidxb, k_hm, vt, gn_att)


def _score_sample_body(pt_ref, qi_ref, w_ref, *refs, n_pages, page, t_len, idx_heads, past_len):
    del pt_ref
    page_refs, new_ref, out_ref = refs[:n_pages], refs[n_pages], refs[n_pages + 1]
    qi = qi_ref[0]
    w = w_ref[0]
    for p in range(n_pages + 1):
        kc = page_refs[p][0].astype(BF16) if p < n_pages else new_ref[0]
        d = jnp.maximum(_nt_dot(qi, kc), 0.0) * w
        s = jnp.sum(d.reshape(t_len, idx_heads, page), axis=1)
        if p == n_pages:
            tpos = lax.broadcasted_iota(I32, (t_len, page), 0)
            kpos = lax.broadcasted_iota(I32, (t_len, page), 1)
            s = jnp.where(kpos <= tpos, s, -jnp.inf)
        out_ref[0, :, p * page:(p + 1) * page] = s


def _select_sample_body(s_ref, bias_ref, sc_ref, theta_ref, pos_ref, *, topk, idx_bits, page):
    nc, rows, _ = sc_ref.shape
    for c in range(nc):
        sc_ref[c] = s_ref[:, c * page:(c + 1) * page]
    lane = lax.broadcasted_iota(I32, (rows, page), 1)

    def count(pred):
        def body(c, acc):
            return acc + jnp.where(pred(sc_ref[c], c * page + lane), 1, 0).astype(I32)
        acc = lax.fori_loop(0, nc, body, jnp.zeros((rows, page), I32))
        return jnp.sum(acc, axis=1, keepdims=True)

    _threshold_search(count, nc * page, topk, idx_bits, theta_ref, pos_ref)
    theta, pos = theta_ref[...], pos_ref[...]
    for c in range(nc):
        s = sc_ref[c]
        sel = _selected(s, c * page + lane, theta, pos) & (s > -jnp.inf)
        bias_ref[:, c * page:(c + 1) * page] = jnp.where(sel, 0.0, MASKED)


def _attn_sample_body(pt_ref, q_ref, bias_ref, gn_ref, *refs, n_pages, page, t_len, n_heads,
                      head_dim):
    del pt_ref
    k_refs, v_refs = refs[:n_pages], refs[n_pages:2 * n_pages]
    knew_ref, vnew_ref, out_ref = refs[2 * n_pages:]
    past = n_pages * page
    bias = bias_ref[0]
    outs = []
    for h in range(n_heads):
        head_rows = pl.ds(h, page, stride=n_heads)
        rows_of = lambda r: r.reshape(page * n_heads, head_dim)[head_rows, :]
        kh = jnp.concatenate([rows_of(r) for r in k_refs], axis=0).astype(BF16)
        vh = jnp.concatenate([rows_of(r) for r in v_refs], axis=0).astype(BF16)
        qh = q_ref[0, h]
        s_past = _nt_dot(qh, kh) + bias[:, :past]
        s_new = _nt_dot(qh, knew_ref[0, h]) + bias[:, past:]
        m = jnp.maximum(jnp.max(s_past, axis=1, keepdims=True),
                        jnp.max(s_new, axis=1, keepdims=True))
        p_past = jnp.exp(s_past - m)
        p_new = jnp.exp(s_new - m)
        l = jnp.sum(p_past, axis=1, keepdims=True) + jnp.sum(p_new, axis=1, keepdims=True)
        o = (jnp.dot(p_past.astype(BF16), vh, preferred_element_type=F32)
             + jnp.dot(p_new.astype(BF16), vnew_ref[0, h], preferred_element_type=F32))
        outs.append(o / l)
    y = jnp.concatenate(outs, axis=1)
    out_ref[0] = _rms_gain(y, gn_ref[...]).astype(BF16)


def _attn_sample(qib, wi, qb, kidx_new, k_new, v_new, cache_k, cache_v, cache_kidx, page_table,
                 gn_att, *, topk, n_heads):
    dbs, t_len, d_att = qb.shape
    n_pages = page_table.shape[1]
    page = cache_k.shape[1]
    idx_dim = cache_kidx.shape[2]
    idx_heads = wi.shape[2]
    nkp = (n_pages + 1) * page
    past_len = n_pages * page
    pad = lambda a: jnp.pad(a, ((0, 0), (0, page - t_len), (0, 0)))

    def paged(width):
        return [pl.BlockSpec((1, page, width), functools.partial(
            lambda b, pt, p: (pt[b, p], 0, 0), p=p)) for p in range(n_pages)]

    per_seq = lambda shape: pl.BlockSpec((1,) + shape, lambda b, pt: (b, 0, 0))

    scores = pl.pallas_call(
        functools.partial(_score_sample_body, n_pages=n_pages, page=page, t_len=t_len,
                          idx_heads=idx_heads, past_len=past_len),
        grid_spec=pltpu.PrefetchScalarGridSpec(
            num_scalar_prefetch=1, grid=(dbs,),
            in_specs=[per_seq((t_len * idx_heads, idx_dim)), per_seq((t_len * idx_heads, 1))]
                     + paged(idx_dim) + [per_seq((page, idx_dim))],
            out_specs=per_seq((t_len, nkp))),
        out_shape=jax.ShapeDtypeStruct((dbs, t_len, nkp), F32),
        compiler_params=_params("parallel"),
        name="score_sample",
    )(page_table, qib.reshape(dbs, t_len * idx_heads, idx_dim),
      wi.reshape(dbs, t_len * idx_heads, 1), *([cache_kidx] * n_pages), pad(kidx_new))

    rows = dbs * t_len
    tr = min(rows, 128)
    bias = pl.pallas_call(
        functools.partial(_select_sample_body, topk=topk,
                          idx_bits=max(1, int(nkp - 1).bit_length()), page=page),
        grid=(rows // tr,),
        in_specs=[pl.BlockSpec((tr, nkp), lambda r: (r, 0))],
        out_specs=pl.BlockSpec((tr, nkp), lambda r: (r, 0)),
        out_shape=jax.ShapeDtypeStruct((rows, nkp), F32),
        scratch_shapes=[pltpu.VMEM((n_pages + 1, tr, page), F32), pltpu.VMEM((tr, 1), F32),
                        pltpu.VMEM((tr, 1), I32)],
        compiler_params=_params("parallel"),
        name="select_sample",
    )(scores.reshape(rows, nkp)).reshape(dbs, t_len, nkp)

    head_dim = d_att // n_heads
    head_major = lambda a: a.reshape(dbs, -1, n_heads, head_dim).transpose(0, 2, 1, 3)
    kv_pages = [pl.BlockSpec((None, page, n_heads, head_dim), functools.partial(
        lambda b, pt, p: (pt[b, p], 0, 0, 0), p=p)) for p in range(n_pages)]
    per_seq4 = lambda rows: pl.BlockSpec((1, n_heads, rows, head_dim),
                                         lambda b, pt: (b, 0, 0, 0))
    return pl.pallas_call(
        functools.partial(_attn_sample_body, n_pages=n_pages, page=page, t_len=t_len,
                          n_heads=n_heads, head_dim=head_dim),
        grid_spec=pltpu.PrefetchScalarGridSpec(
            num_scalar_prefetch=1, grid=(dbs,),
            in_specs=[per_seq4(t_len), per_seq((t_len, nkp)),
                      pl.BlockSpec((1, d_att), lambda b, pt: (0, 0))]
                     + kv_pages + kv_pages + [per_seq4(page), per_seq4(page)],
            out_specs=per_seq((t_len, d_att))),
        out_shape=jax.ShapeDtypeStruct((dbs, t_len, d_att), BF16),
        compiler_params=_params("parallel"),
        name="attn_sample",
    )(page_table, head_major(qb), bias, gn_att, *([cache_k] * n_pages), *([cache_v] * n_pages),
      head_major(pad(k_new)), head_major(pad(v_new)))


def _layer_norm(x, g, b):
    mu = jnp.mean(x, axis=-1, keepdims=True)
    xc = x - mu
    var = jnp.mean(xc * xc, axis=-1, keepdims=True)
    return xc * lax.rsqrt(var + LN_EPS) * g + b


def _finish_body(x_ref, ml_ref, ma_ref, wo_ref, wfi_ref, wfo_ref, p_ref, out_ref,
                 *, alpha, d_lru, d_ff, fc):
    ln1_g, ln1_b, ln2_g, ln2_b = (p_ref[r:r + 1, :] for r in range(4))
    y = (jnp.dot(ml_ref[...], wo_ref[0:d_lru, :], preferred_element_type=F32)
         + jnp.dot(ma_ref[...], wo_ref[d_lru:, :], preferred_element_type=F32))
    x1 = _layer_norm(alpha * x_ref[...] + y, ln1_g, ln1_b)
    x1b = x1.astype(BF16)
    f = jnp.zeros(x1.shape, F32)
    for c in range(d_ff // fc):
        u = jnp.dot(x1b, wfi_ref[:, c * fc:(c + 1) * fc], preferred_element_type=F32)
        g = jnp.dot(x1b, wfi_ref[:, d_ff + c * fc:d_ff + (c + 1) * fc],
                    preferred_element_type=F32)
        hidden = (g * jax.nn.sigmoid(g) * u).astype(BF16)
        f = f + jnp.dot(hidden, wfo_ref[c * fc:(c + 1) * fc, :], preferred_element_type=F32)
    out_ref[...] = _layer_norm(alpha * x1 + f, ln2_g, ln2_b)


def _finish(x2d, mix_lru, mix_att, wo, wfi, wfo, pvec, *, alpha, tm):
    n, d_model = x2d.shape
    d_lru = mix_lru.shape[1]
    d_ff = wfo.shape[0]
    fc = 2 * LANES if d_ff % (2 * LANES) == 0 else LANES
    row = lambda w: pl.BlockSpec((tm, w), lambda i: (i, 0))
    const = lambda a: pl.BlockSpec(a.shape, lambda i: (0, 0), pipeline_mode=pl.Buffered(1))
    return pl.pallas_call(
        functools.partial(_finish_body, alpha=alpha, d_lru=d_lru, d_ff=d_ff, fc=fc),
        grid=(n // tm,),
        in_specs=[row(d_model), row(d_lru), row(mix_att.shape[1]),
                  const(wo), const(wfi), const(wfo), const(pvec)],
        out_specs=row(d_model),
        out_shape=jax.ShapeDtypeStruct((n, d_model), F32),
        compiler_params=_params("parallel"),
        name="finish",
    )(x2d, mix_lru, mix_att, wo, wfi, wfo, pvec)


def _block_diag(w):
    nb, bi, bo = w.shape
    eye = jnp.eye(nb, dtype=w.dtype)
    return (w[:, :, None, :] * eye[:, None, :, None]).reshape(nb * bi, nb * bo)


def _row_tile(n, want):
    tm = min(n, want)
    while n % tm:
        tm //= 2
    return tm


def kernel(x_prompt, x_sample, cache_k, cache_v, cache_kidx, state_conv, state_h, page_table,
           w_in, conv_w, conv_b, w_a, b_a, w_x, b_x, lam, gn_lru, gn_att, w_out,
           ln1_g, ln1_b, w_ffn_in, w_ffn_out, ln2_g, ln2_b):
    depth, d_model, d_in = w_in.shape
    bsz, seq, _ = x_prompt.shape
    dbs, dseq, _ = x_sample.shape
    d_lru = conv_w.shape[2]
    n_phys, page, n_heads, head_dim = cache_k.shape[1:]
    d_att = n_heads * head_dim
    idx_dim = cache_kidx.shape[3]
    d_qi = IDX_HEADS * idx_dim
    n_pages = page_table.shape[1]
    past_len = n_pages * page
    alpha = (2.0 * depth) ** 0.25
    assert d_in == 2 * d_lru + 3 * d_att + d_qi + idx_dim + IDX_HEADS
    assert idx_dim + IDX_HEADS <= LANES and LANES % head_dim == 0 and head_dim == idx_dim
    geom = dict(d_lru=d_lru, d_att=d_att, d_qi=d_qi, idx_dim=idx_dim, head_dim=head_dim)
    d_main = d_in - idx_dim - IDX_HEADS

    pos_p = jnp.tile(jnp.arange(seq), bsz)
    pos_s = jnp.tile(past_len + jnp.arange(dseq), dbs)
    topk_p = min(TOPK_MAX, seq // 4)
    topk_s = min(TOPK_MAX, (past_len + dseq) // 4)

    xp = x_prompt.reshape(bsz * seq, d_model)
    xs = x_sample.reshape(dbs * dseq, d_model)
    outs_p, outs_s = [], []
    for l in range(depth):
        w_pad = jnp.pad(w_in[l], ((0, 0), (0, d_main + LANES - d_in))).astype(BF16)
        wa_bd = _block_diag(w_a[l]).astype(BF16)
        wx_bd = _block_diag(w_x[l]).astype(BF16)
        lru_vec = jnp.stack([conv_b[l], b_a[l], b_x[l], lam[l], gn_lru[l]]
                            + [jnp.zeros_like(lam[l])] * 3)
        fin_vec = jnp.stack([ln1_g[l], ln1_b[l], ln2_g[l], ln2_b[l]])
        wo, wfi, wfo = (w_out[l].astype(BF16), w_ffn_in[l].astype(BF16),
                        w_ffn_out[l].astype(BF16))
        gn_a = gn_att[l][None, :]

        xl, gate, qb, k, kb, v, vb, qib, tail = _project(
            xp, w_pad, pos_p, tm=_row_tile(bsz * seq, 512), **geom)
        b3 = lambda a: a.reshape(bsz, seq, a.shape[-1])
        xl3 = b3(xl)
        mix_l, h_last = _lru_prompt(
            xl3, b3(gate), jnp.zeros((bsz, CONV_W - 1, d_lru), F32), jnp.zeros((bsz, d_lru), F32),
            conv_w[l], lru_vec, wa_bd, wx_bd, tt=_row_tile(seq, 256))
        tail3 = b3(tail)
        ki = tail3[:, :, :idx_dim]
        mix_a = _attn_prompt(b3(qib), tail3[:, :, idx_dim:idx_dim + IDX_HEADS], b3(qb),
                             ki.astype(BF16), b3(kb), b3(vb), gn_a, tq=_row_tile(seq, 256),
                             topk=topk_p, n_heads=n_heads, idx_dim=idx_dim)
        xp = _finish(xp, mix_l.reshape(bsz * seq, d_lru), mix_a.reshape(bsz * seq, d_att),
                     wo, wfi, wfo, fin_vec, alpha=alpha, tm=_row_tile(bsz * seq, 512))
        outs_p.append((k.reshape(bsz, seq, n_heads, head_dim), v.reshape(bsz, seq, n_heads, head_dim),
                       ki, xl3[:, seq - (CONV_W - 1):], h_last[:, 0]))

        xl, gate, qb, k, kb, v, vb, qib, tail = _project(
            xs, w_pad, pos_s, tm=_row_tile(dbs * dseq, 256), **geom)
        d3 = lambda a: a.reshape(dbs, dseq, a.shape[-1])
        tm_major = lambda a: jnp.swapaxes(d3(a), 0, 1)
        xl3 = d3(xl)
        mix_l, h_last = _lru_sample(
            tm_major(xl), tm_major(gate), jnp.swapaxes(state_conv[l], 0, 1).astype(F32),
            state_h[l], conv_w[l], lru_vec, wa_bd, wx_bd)
        tail3 = d3(tail)
        ki = tail3[:, :, :idx_dim]
        wi = tail3[:, :, idx_dim:idx_dim + IDX_HEADS]
        mix_a = _attn_sample(
            d3(qib), wi, d3(qb), ki.astype(BF16), d3(kb), d3(vb),
            cache_k[l], cache_v[l], cache_kidx[l], page_table, gn_a, topk=topk_s,
            n_heads=n_heads)
        xs = _finish(xs, jnp.swapaxes(mix_l, 0, 1).reshape(dbs * dseq, d_lru),
                     mix_a.reshape(dbs * dseq, d_att), wo, wfi, wfo, fin_vec, alpha=alpha,
                     tm=_row_tile(dbs * dseq, 256))
        conv_new = jnp.concatenate([state_conv[l].astype(F32), xl3], axis=1)[:, -(CONV_W - 1):]
        outs_s.append((k.reshape(dbs, dseq, n_heads, head_dim), v.reshape(dbs, dseq, n_heads, head_dim),
                       ki, conv_new, h_last))

    stack = lambda outs, j: jnp.stack([o[j] for o in outs])
    return (xp.reshape(bsz, seq, d_model), xs.reshape(dbs, dseq, d_model),
            *(stack(outs_p, j) for j in range(5)), *(stack(outs_s, j) for j in range(5)))
```

```python
import functools

import jax
import jax.numpy as jnp
import numpy as np
from jax import lax
from jax.experimental import pallas as pl
from jax.experimental.pallas import tpu as pltpu

CONV_W = 4
LRU_C = 8.0
LRU_BLOCKS = 8
IDX_HEADS = 8
TOPK_MAX = 256
ROPE_FRACTION = 4
ROPE_THETA = 500000.0
RMS_EPS = 1e-6
LN_EPS = 1e-5

LANES = 128
SUBLANES = 8
VMEM_LIMIT = 56 * 1024 * 1024
MASKED = -1e30
INT_MIN = -2 ** 31

F32 = jnp.float32
BF16 = jnp.bfloat16
I32 = jnp.int32


def _params(*sem):
    return pltpu.CompilerParams(dimension_semantics=sem, vmem_limit_bytes=VMEM_LIMIT)


def _nt_dot(a, b):
    return lax.dot_general(a, b, (((1,), (1,)), ((), ())), preferred_element_type=F32)


def _proj_body(x_ref, w_ref, c_ref, s1_ref, s2_ref,
               xl_ref, gate_ref, qb_ref, k_ref, kb_ref, v_ref, vb_ref, qib_ref, tail_ref,
               *, d_lru, d_att, d_qi, idx_dim, rope_half, q_scale):
    xb = x_ref[...].astype(BF16)
    c, s1, s2 = c_ref[...], s1_ref[...], s2_ref[...]

    def proj(lo, width):
        return jnp.dot(xb, w_ref[:, lo:lo + width], preferred_element_type=F32)

    def tiled(t, width):
        reps = width // LANES
        return t if reps == 1 else jnp.concatenate([t] * reps, axis=1)

    def rope(z, cc, ss1, ss2):
        width = z.shape[1]
        return (z * cc + pltpu.roll(z, width - rope_half, 1) * ss1
                + pltpu.roll(z, rope_half, 1) * ss2)

    lo = 0
    xl_ref[...] = proj(lo, d_lru); lo += d_lru
    gate_ref[...] = proj(lo, d_lru); lo += d_lru
    q = rope(proj(lo, d_att), tiled(c, d_att), tiled(s1, d_att), tiled(s2, d_att)); lo += d_att
    qb_ref[...] = (q * q_scale).astype(BF16)
    k = rope(proj(lo, d_att), tiled(c, d_att), tiled(s1, d_att), tiled(s2, d_att)); lo += d_att
    k_ref[...] = k
    kb_ref[...] = k.astype(BF16)
    v = proj(lo, d_att); lo += d_att
    v_ref[...] = v
    vb_ref[...] = v.astype(BF16)
    qi = rope(proj(lo, d_qi), tiled(c, d_qi), tiled(s1, d_qi), tiled(s2, d_qi)); lo += d_qi
    qib_ref[...] = qi.astype(BF16)
    tail = proj(lo, LANES)
    is_key = lax.broadcasted_iota(I32, tail.shape, 1) < idx_dim
    tail_ref[...] = rope(tail, jnp.where(is_key, c, 1.0), jnp.where(is_key, s1, 0.0),
                         jnp.where(is_key, s2, 0.0))


def _rope_tables(pos, head_dim):
    rope_dim = head_dim // ROPE_FRACTION
    half = rope_dim // 2
    freqs = ROPE_THETA ** (-jnp.arange(half, dtype=F32) / half)
    ang = pos.astype(F32)[:, None] * freqs[None, :]
    cos, sin = jnp.cos(ang), jnp.sin(ang)
    n = pos.shape[0]
    rest = head_dim - rope_dim
    c = jnp.concatenate([cos, cos, jnp.ones((n, rest), F32)], 1)
    s1 = jnp.concatenate([-sin, jnp.zeros((n, half + rest), F32)], 1)
    s2 = jnp.concatenate([jnp.zeros((n, half), F32), sin, jnp.zeros((n, rest), F32)], 1)
    reps = LANES // head_dim
    return [jnp.tile(t, (1, reps)) for t in (c, s1, s2)], half


def _project(x2d, w_pad, pos, *, d_lru, d_att, d_qi, idx_dim, head_dim, tm):
    n, d_model = x2d.shape
    (c, s1, s2), half = _rope_tables(pos, head_dim)
    row = lambda w: pl.BlockSpec((tm, w), lambda i: (i, 0))
    outs = [
        (d_lru, F32), (d_lru, F32), (d_att, BF16), (d_att, F32), (d_att, BF16),
        (d_att, F32), (d_att, BF16), (d_qi, BF16), (LANES, F32)]
    body = functools.partial(_proj_body, d_lru=d_lru, d_att=d_att, d_qi=d_qi, idx_dim=idx_dim,
                             rope_half=half, q_scale=head_dim ** -0.5)
    return pl.pallas_call(
        body,
        grid=(n // tm,),
        in_specs=[row(d_model), pl.BlockSpec(w_pad.shape, lambda i: (0, 0)),
                  row(LANES), row(LANES), row(LANES)],
        out_specs=[row(w) for w, _ in outs],
        out_shape=[jax.ShapeDtypeStruct((n, w), dt) for w, dt in outs],
        compiler_params=_params("parallel"),
        name="proj",
    )(x2d, w_pad, c, s1, s2)


def _softplus(x):
    return jnp.maximum(x, 0.0) + jnp.log1p(jnp.exp(-jnp.abs(x)))


def _gelu_tanh(x):
    return 0.5 * x * (1.0 + jnp.tanh(np.sqrt(2.0 / np.pi).astype(np.float32)
                                     * (x + 0.044715 * (x * x * x))))


def _lru_gates(xc, wa_ref, wx_ref, b_a, b_x, lam):
    xcb = xc.astype(BF16)
    r = jax.nn.sigmoid(jnp.dot(xcb, wa_ref[...], preferred_element_type=F32) + b_a)
    i = jax.nn.sigmoid(jnp.dot(xcb, wx_ref[...], preferred_element_type=F32) + b_x)
    log_a = -LRU_C * r * _softplus(-lam)
    a = jnp.exp(log_a)
    t = jnp.tanh(log_a)
    b = jnp.sqrt(-2.0 * t / (1.0 - t)) * (i * xc)
    return a, b


def _rms_gain(y, g):
    return y * lax.rsqrt(jnp.mean(y * y, axis=-1, keepdims=True) + RMS_EPS) * g


def _lru_prompt_body(xl_ref, gate_ref, cprev_ref, h0_ref, cw_ref, p_ref, wa_ref, wx_ref,
                     mix_ref, hlast_ref, ext_ref, hc_ref, *, tt):
    j = pl.program_id(1)

    @pl.when(j == 0)
    def _():
        ext_ref[0:SUBLANES, :] = cprev_ref[0]
        hc_ref[0:1, :] = h0_ref[0]

    xl = xl_ref[0]
    ext_ref[SUBLANES:SUBLANES + tt, :] = xl
    conv_b, b_a, b_x, lam, gn = (p_ref[r:r + 1, :] for r in range(5))
    xc = conv_b + (cw_ref[0:1, :] * ext_ref[SUBLANES - 3:SUBLANES - 3 + tt, :]
                   + cw_ref[1:2, :] * ext_ref[SUBLANES - 2:SUBLANES - 2 + tt, :]
                   + cw_ref[2:3, :] * ext_ref[SUBLANES - 1:SUBLANES - 1 + tt, :]
                   + cw_ref[3:4, :] * xl)
    ext_ref[0:SUBLANES, :] = ext_ref[tt:tt + SUBLANES, :]

    a, b = _lru_gates(xc, wa_ref, wx_ref, b_a, b_x, lam)
    row = lax.broadcasted_iota(I32, a.shape, 0)
    d = 1
    while d < tt:
        keep = row >= d
        a_prev = jnp.where(keep, pltpu.roll(a, d, 0), 1.0)
        b_prev = jnp.where(keep, pltpu.roll(b, d, 0), 0.0)
        b = a * b_prev + b
        a = a * a_prev
        d *= 2
    h = a * hc_ref[0:1, :] + b
    hc_ref[0:1, :] = h[tt - 1:tt, :]
    hlast_ref[0] = h[tt - 1:tt, :]
    mix_ref[0] = _rms_gain(h * _gelu_tanh(gate_ref[0]), gn).astype(BF16)


def _lru_prompt(xl, gate, conv_prev, h0, conv_w, pvec, wa_bd, wx_bd, *, tt):
    bsz, t, d = xl.shape
    cprev8 = jnp.concatenate(
        [jnp.zeros((bsz, SUBLANES - (CONV_W - 1), d), F32), conv_prev.astype(F32)], axis=1)
    const = lambda shape: pl.BlockSpec(shape, lambda b, j: (0,) * len(shape))
    return pl.pallas_call(
        functools.partial(_lru_prompt_body, tt=tt),
        grid=(bsz, t // tt),
        in_specs=[pl.BlockSpec((1, tt, d), lambda b, j: (b, j, 0)),
                  pl.BlockSpec((1, tt, d), lambda b, j: (b, j, 0)),
                  pl.BlockSpec((1, SUBLANES, d), lambda b, j: (b, 0, 0)),
                  pl.BlockSpec((1, 1, d), lambda b, j: (b, 0, 0)),
                  const(conv_w.shape), const(pvec.shape), const(wa_bd.shape), const(wx_bd.shape)],
        out_specs=[pl.BlockSpec((1, tt, d), lambda b, j: (b, j, 0)),
                   pl.BlockSpec((1, 1, d), lambda b, j: (b, 0, 0))],
        out_shape=[jax.ShapeDtypeStruct((bsz, t, d), BF16),
                   jax.ShapeDtypeStruct((bsz, 1, d), F32)],
        scratch_shapes=[pltpu.VMEM((tt + SUBLANES, d), F32), pltpu.VMEM((SUBLANES, d), F32)],
        compiler_params=_params("parallel", "arbitrary"),
        name="lru_prompt",
    )(xl, gate, cprev8, h0.astype(F32)[:, None, :], conv_w, pvec, wa_bd, wx_bd)


def _lru_sample_body(xl_ref, gate_ref, cprev_ref, h0_ref, cw_ref, p_ref, wa_ref, wx_ref,
                     mix_ref, hlast_ref, *, t_len):
    conv_b, b_a, b_x, lam, gn = (p_ref[r:r + 1, :] for r in range(5))
    xp = [cprev_ref[s] for s in range(CONV_W - 1)] + [xl_ref[s] for s in range(t_len)]
    h = h0_ref[...]
    for s in range(t_len):
        xc = conv_b + (cw_ref[0:1, :] * xp[s] + cw_ref[1:2, :] * xp[s + 1]
                       + cw_ref[2:3, :] * xp[s + 2] + cw_ref[3:4, :] * xp[s + 3])
        a, b = _lru_gates(xc, wa_ref, wx_ref, b_a, b_x, lam)
        h = a * h + b
        mix_ref[s] = _rms_gain(h * _gelu_tanh(gate_ref[s]), gn).astype(BF16)
    hlast_ref[...] = h


def _lru_sample(xl_t, gate_t, cprev_t, h0, conv_w, pvec, wa_bd, wx_bd):
    t_len, dbs, d = xl_t.shape
    return pl.pallas_call(
        functools.partial(_lru_sample_body, t_len=t_len),
        out_shape=[jax.ShapeDtypeStruct((t_len, dbs, d), BF16),
                   jax.ShapeDtypeStruct((dbs, d), F32)],
        compiler_params=pltpu.CompilerParams(vmem_limit_bytes=VMEM_LIMIT),
        name="lru_sample",
    )(xl_t, gate_t, cprev_t, h0.astype(F32), conv_w, pvec, wa_bd, wx_bd)


KEY_NEG_INF = INT_MIN + 0x7FFFFF
REFINE_STEPS = 16


def _key_to_float(key):
    return pltpu.bitcast(jnp.where(key >= 0, key, key ^ 0x7FFFFFFF), F32)


def _threshold_search(count, total, topk, idx_bits, theta_ref, pos_ref):
    shape = pos_ref.shape

    def value_step(it, carry):
        base, n_base = carry
        trial = base ^ lax.shift_left(jnp.int32(1), jnp.int32(31) - it)
        trial_f = _key_to_float(trial)
        n = count(lambda s, idx: s >= trial_f)
        ok = n >= topk
        return jnp.where(ok, trial, base), jnp.where(ok, n, n_base)

    theta_key, n_ge = lax.fori_loop(
        0, 32, value_step, (jnp.full(shape, INT_MIN, I32), jnp.zeros(shape, I32) + total))
    theta = jnp.where(theta_key < KEY_NEG_INF, -jnp.inf, _key_to_float(theta_key))
    theta_ref[...] = theta
    pos_ref[...] = jnp.full(shape, 2 ** idx_bits - 1, I32)

    @pl.when(jnp.max(n_ge) > topk)
    def _():
        def refine_step(_, carry):
            lo, hi = carry
            mid = lo + 0.5 * (hi - lo)
            ok = count(lambda s, idx: s >= mid) >= topk
            return jnp.where(ok, mid, lo), jnp.where(ok, hi, mid)

        theta_fine, _ = lax.fori_loop(
            0, REFINE_STEPS, refine_step,
            (theta, _key_to_float(jnp.maximum(theta_key, KEY_NEG_INF) + 1)))
        theta_ref[...] = theta_fine
        need = topk - count(lambda s, idx: s > theta_fine)

        def index_step(it, pos):
            trial = pos + lax.shift_left(jnp.int32(1), jnp.int32(idx_bits - 1) - it)
            below = count(lambda s, idx: (s == theta_fine) & (idx < trial))
            return jnp.where(below < need, trial, pos)

        pos_ref[...] = lax.fori_loop(0, idx_bits, index_step, jnp.zeros(shape, I32))


def _selected(score, idx, theta, pos):
    return (score > theta) | ((score == theta) & (idx <= pos))


def _attn_prompt_body(qit_ref, wt_ref, qt_ref, kidx_ref, k_ref, vt_ref, gn_ref, out_ref,
                      sc_ref, y_ref, theta_ref, pos_ref, m_ref, l_ref, s_ref, *, tq, ch, topk,
                      idx_bits, n_heads, head_dim, idx_heads, idx_dim):
    i = pl.program_id(1)
    n_keys = (i + 1) * tq
    qpos = i * tq + lax.broadcasted_iota(I32, (1, tq), 1)
    sub = LANES
    kpos_sub = lax.broadcasted_iota(I32, (sub, tq), 0)
    kpos_ch = lax.broadcasted_iota(I32, (ch, tq), 0)
    nc = (i + 1) * (tq // ch)

    def score_chunk(c, carry):
        for j in range(ch // sub):
            start = pl.multiple_of(c * ch + j * sub, sub)
            rows = pl.ds(start, sub)
            kc = kidx_ref[0, rows, :]
            acc = jnp.zeros((sub, tq), F32)
            for h in range(idx_heads):
                d = jnp.dot(kc, qit_ref[0, h * idx_dim:(h + 1) * idx_dim, :],
                            preferred_element_type=F32)
                acc = acc + jnp.maximum(d, 0.0) * wt_ref[0, h:h + 1, :]
            sc_ref[rows, :] = jnp.where(start + kpos_sub <= qpos, acc, -jnp.inf)
        return carry

    lax.fori_loop(0, nc, score_chunk, 0)

    def count(pred):
        def body(c, acc):
            rows = pl.ds(pl.multiple_of(c * ch, ch), ch)
            hit = jnp.where(pred(sc_ref[rows, :], c * ch + kpos_ch), 1, 0).astype(I32)
            return acc + jnp.sum(hit.reshape(ch // SUBLANES, SUBLANES, tq), axis=0)
        acc = lax.fori_loop(0, nc, body, jnp.zeros((SUBLANES, tq), I32))
        return jnp.sum(acc, axis=0, keepdims=True)

    _threshold_search(count, n_keys, topk, idx_bits, theta_ref, pos_ref)
    theta, pos = theta_ref[...], pos_ref[...]

    def bias_chunk(c, carry):
        rows = pl.ds(pl.multiple_of(c * ch, ch), ch)
        idx = c * ch + kpos_ch
        sel = _selected(sc_ref[rows, :], idx, theta, pos) & (idx <= qpos)
        sc_ref[rows, :] = jnp.where(sel, 0.0, MASKED)
        return carry

    lax.fori_loop(0, nc, bias_chunk, 0)

    heads = [(h, slice(h * head_dim, (h + 1) * head_dim)) for h in range(n_heads)]
    m_ref[...] = jnp.full(m_ref.shape, -jnp.inf, F32)
    l_ref[...] = jnp.zeros(l_ref.shape, F32)
    y_ref[...] = jnp.zeros(y_ref.shape, F32)

    def kv_chunk(c, carry):
        rows = pl.ds(pl.multiple_of(c * ch, ch), ch)
        for h, hs in heads:
            s_ref[h] = jnp.dot(k_ref[0, h, rows, :], qt_ref[0, hs, :],
                               preferred_element_type=F32)
        for h, hs in heads:
            s = s_ref[h] + sc_ref[rows, :]
            m = m_ref[h:h + 1, :]
            m_new = jnp.maximum(m, jnp.max(s, axis=0, keepdims=True))
            alpha = jnp.exp(m - m_new)
            p = jnp.exp(s - m_new)
            m_ref[h:h + 1, :] = m_new
            l_ref[h:h + 1, :] = alpha * l_ref[h:h + 1, :] + jnp.sum(p, axis=0, keepdims=True)
            y_ref[hs, :] = alpha * y_ref[hs, :] + jnp.dot(
                vt_ref[0, c, hs, :], p.astype(BF16), preferred_element_type=F32)
        return carry

    lax.fori_loop(0, nc, kv_chunk, 0)
    for h, hs in heads:
        y_ref[hs, :] = y_ref[hs, :] / l_ref[h:h + 1, :]
    out_ref[0] = _rms_gain(y_ref[...].T, gn_ref[...]).astype(BF16)


def _attn_prompt(qib, wi, qb, kidxb, kb, vb, gn_att, *, tq, topk, n_heads, idx_dim):
    bsz, t, d_att = qb.shape
    head_dim = d_att // n_heads
    ch = tq
    idx_bits = max(1, int(t - 1).bit_length())
    qit, qt, wt = (jnp.swapaxes(a, 1, 2) for a in (qib, qb, wi))
    k_hm = kb.reshape(bsz, t, n_heads, head_dim).transpose(0, 2, 1, 3)
    vt = vb.reshape(bsz, t // ch, ch, d_att).transpose(0, 1, 3, 2)
    cols = lambda a: pl.BlockSpec((1, a.shape[1], tq), lambda b, i: (b, 0, i))
    full = lambda a: pl.BlockSpec((1,) + a.shape[1:], lambda b, i: (b,) + (0,) * (a.ndim - 1))
    body = functools.partial(
        _attn_prompt_body, tq=tq, ch=ch, topk=topk, idx_bits=idx_bits, n_heads=n_heads,
        head_dim=head_dim, idx_heads=wi.shape[2], idx_dim=idx_dim)
    return pl.pallas_call(
        body,
        grid=(bsz, t // tq),
        in_specs=[cols(qit), cols(wt), cols(qt), full(kidxb), full(k_hm), full(vt),
                  pl.BlockSpec((1, d_att), lambda b, i: (0, 0))],
        out_specs=pl.BlockSpec((1, tq, d_att), lambda b, i: (b, i, 0)),
        out_shape=jax.ShapeDtypeStruct((bsz, t, d_att), BF16),
        scratch_shapes=[pltpu.VMEM((t, tq), F32), pltpu.VMEM((d_att, tq), F32),
                        pltpu.VMEM((1, tq), F32), pltpu.VMEM((1, tq), I32),
                        pltpu.VMEM((n_heads, tq), F32), pltpu.VMEM((n_heads, tq), F32),
                        pltpu.VMEM((n_heads, ch, tq), F32)],
        compiler_params=_params("parallel", "arbitrary"),
        name="attn_prompt",
    )(qit, wt, qt, kidxb, k_hm, vt, gn_att)


def _score_sample_body(pt_ref, qi_ref, w_ref, *refs, n_pages, page, t_len, idx_heads, past_len):
    del pt_ref
    page_refs, new_ref, out_ref = refs[:n_pages], refs[n_pages], refs[n_pages + 1]
    qi = qi_ref[0]
    w = w_ref[0]
    kt = jnp.concatenate([r[0].astype(BF16) for r in page_refs] + [new_ref[0]], axis=1)
    d = jnp.maximum(jnp.dot(qi, kt, preferred_element_type=F32), 0.0) * w
    s = jnp.sum(d.reshape(t_len, idx_heads, past_len + page), axis=1)
    tpos = past_len + lax.broadcasted_iota(I32, s.shape, 0)
    kpos = lax.broadcasted_iota(I32, s.shape, 1)
    out_ref[0] = jnp.where(kpos <= tpos, s, -jnp.inf)


def _select_sample_body(s_ref, bias_ref, sc_ref, theta_ref, pos_ref, *, topk, idx_bits, page):
    nc, rows, _ = sc_ref.shape
    for c in range(nc):
        sc_ref[c] = s_ref[:, c * page:(c + 1) * page]
    lane = lax.broadcasted_iota(I32, (rows, page), 1)

    def count(pred):
        def body(c, acc):
            return acc + jnp.where(pred(sc_ref[c], c * page + lane), 1, 0).astype(I32)
        acc = lax.fori_loop(0, nc, body, jnp.zeros((rows, page), I32))
        return jnp.sum(acc, axis=1, keepdims=True)

    _threshold_search(count, nc * page, topk, idx_bits, theta_ref, pos_ref)
    theta, pos = theta_ref[...], pos_ref[...]
    for c in range(nc):
        s = sc_ref[c]
        sel = _selected(s, c * page + lane, theta, pos) & (s > -jnp.inf)
        bias_ref[:, c * page:(c + 1) * page] = jnp.where(sel, 0.0, MASKED)


def _attn_sample_body(pt_ref, q_ref, bias_ref, gn_ref, *refs, n_pages, page, t_len, n_heads,
                      head_dim):
    del pt_ref
    k_refs, v_refs = refs[:n_pages], refs[n_pages:2 * n_pages]
    knew_ref, vnew_ref, out_ref = refs[2 * n_pages:]
    past = n_pages * page
    bias = bias_ref[0]
    outs = []
    del past
    for h in range(n_heads):
        kt = jnp.concatenate([r[0, h].astype(BF16) for r in k_refs] + [knew_ref[0, h]], axis=1)
        vt = jnp.concatenate([r[0, h].astype(BF16) for r in v_refs] + [vnew_ref[0, h]], axis=1)
        s = jnp.dot(q_ref[0, h], kt, preferred_element_type=F32) + bias
        m = jnp.max(s, axis=1, keepdims=True)
        p = jnp.exp(s - m)
        l = jnp.sum(p, axis=1, keepdims=True)
        outs.append(_nt_dot(p.astype(BF16), vt) / l)
    y = jnp.concatenate(outs, axis=1)
    out_ref[0] = _rms_gain(y, gn_ref[...]).astype(BF16)


def _attn_sample(qib, wi, qb, kidx_new, k_new, v_new, cache_k, cache_v, cache_kidx, page_table,
                 gn_att, *, topk, n_heads):
    dbs, t_len, d_att = qb.shape
    n_pages = page_table.shape[1]
    page = cache_k.shape[1]
    idx_dim = cache_kidx.shape[2]
    idx_heads = wi.shape[2]
    head_dim = d_att // n_heads
    nkp = (n_pages + 1) * page
    past_len = n_pages * page
    pad = lambda a: jnp.pad(a, ((0, 0), (0, page - t_len), (0, 0)))
    kidx_t = jnp.transpose(cache_kidx, (0, 2, 1))
    k_t = jnp.transpose(cache_k, (0, 2, 3, 1))
    v_t = jnp.transpose(cache_v, (0, 2, 3, 1))
    new_t = lambda a: jnp.transpose(pad(a).reshape(dbs, page, n_heads, head_dim), (0, 2, 3, 1))

    def paged(shape):
        return [pl.BlockSpec((1,) + shape, functools.partial(
            lambda b, pt, p: (pt[b, p],) + (0,) * len(shape), p=p)) for p in range(n_pages)]

    per_seq = lambda shape: pl.BlockSpec((1,) + shape, lambda b, pt: (b,) + (0,) * len(shape))

    scores = pl.pallas_call(
        functools.partial(_score_sample_body, n_pages=n_pages, page=page, t_len=t_len,
                          idx_heads=idx_heads, past_len=past_len),
        grid_spec=pltpu.PrefetchScalarGridSpec(
            num_scalar_prefetch=1, grid=(dbs,),
            in_specs=[per_seq((t_len * idx_heads, idx_dim)), per_seq((t_len * idx_heads, 1))]
                     + paged((idx_dim, page)) + [per_seq((idx_dim, page))],
            out_specs=per_seq((t_len, nkp))),
        out_shape=jax.ShapeDtypeStruct((dbs, t_len, nkp), F32),
        compiler_params=_params("parallel"),
        name="score_sample",
    )(page_table, qib.reshape(dbs, t_len * idx_heads, idx_dim),
      wi.reshape(dbs, t_len * idx_heads, 1), *([kidx_t] * n_pages),
      jnp.transpose(pad(kidx_new), (0, 2, 1)))

    rows = dbs * t_len
    tr = min(rows, 128)
    bias = pl.pallas_call(
        functools.partial(_select_sample_body, topk=topk,
                          idx_bits=max(1, int(nkp - 1).bit_length()), page=page),
        grid=(rows // tr,),
        in_specs=[pl.BlockSpec((tr, nkp), lambda r: (r, 0))],
        out_specs=pl.BlockSpec((tr, nkp), lambda r: (r, 0)),
        out_shape=jax.ShapeDtypeStruct((rows, nkp), F32),
        scratch_shapes=[pltpu.VMEM((n_pages + 1, tr, page), F32), pltpu.VMEM((tr, 1), F32),
                        pltpu.VMEM((tr, 1), I32)],
        compiler_params=_params("parallel"),
        name="select_sample",
    )(scores.reshape(rows, nkp)).reshape(dbs, t_len, nkp)

    kv_pages = paged((n_heads, head_dim, page))
    return pl.pallas_call(
        functools.partial(_attn_sample_body, n_pages=n_pages, page=page, t_len=t_len,
                          n_heads=n_heads, head_dim=head_dim),
        grid_spec=pltpu.PrefetchScalarGridSpec(
            num_scalar_prefetch=1, grid=(dbs,),
            in_specs=[per_seq((n_heads, t_len, head_dim)), per_seq((t_len, nkp)),
                      pl.BlockSpec((1, d_att), lambda b, pt: (0, 0))]
                     + kv_pages + kv_pages
                     + [per_seq((n_heads, head_dim, page)), per_seq((n_heads, head_dim, page))],
            out_specs=per_seq((t_len, d_att))),
        out_shape=jax.ShapeDtypeStruct((dbs, t_len, d_att), BF16),
        compiler_params=_params("parallel"),
        name="attn_sample",
    )(page_table, qb.reshape(dbs, t_len, n_heads, head_dim).transpose(0, 2, 1, 3), bias, gn_att,
      *([k_t] * n_pages), *([v_t] * n_pages), new_t(k_new), new_t(v_new))


def _layer_norm(x, g, b):
    mu = jnp.mean(x, axis=-1, keepdims=True)
    xc = x - mu
    var = jnp.mean(xc * xc, axis=-1, keepdims=True)
    return xc * lax.rsqrt(var + LN_EPS) * g + b


def _finish_body(x_ref, ml_ref, ma_ref, wo_ref, wfi_ref, wfo_ref, p_ref, out_ref,
                 *, alpha, d_lru, d_ff, fc):
    ln1_g, ln1_b, ln2_g, ln2_b = (p_ref[r:r + 1, :] for r in range(4))
    y = (jnp.dot(ml_ref[...], wo_ref[0:d_lru, :], preferred_element_type=F32)
         + jnp.dot(ma_ref[...], wo_ref[d_lru:, :], preferred_element_type=F32))
    x1 = _layer_norm(alpha * x_ref[...] + y, ln1_g, ln1_b)
    x1b = x1.astype(BF16)
    f = jnp.zeros(x1.shape, F32)
    for c in range(d_ff // fc):
        u = jnp.dot(x1b, wfi_ref[:, c * fc:(c + 1) * fc], preferred_element_type=F32)
        g = jnp.dot(x1b, wfi_ref[:, d_ff + c * fc:d_ff + (c + 1) * fc],
                    preferred_element_type=F32)
        hidden = (g * jax.nn.sigmoid(g) * u).astype(BF16)
        f = f + jnp.dot(hidden, wfo_ref[c * fc:(c + 1) * fc, :], preferred_element_type=F32)
    out_ref[...] = _layer_norm(alpha * x1 + f, ln2_g, ln2_b)


def _finish(x2d, mix_lru, mix_att, wo, wfi, wfo, pvec, *, alpha, tm):
    n, d_model = x2d.shape
    d_lru = mix_lru.shape[1]
    d_ff = wfo.shape[0]
    fc = 2 * LANES if d_ff % (2 * LANES) == 0 else LANES
    row = lambda w: pl.BlockSpec((tm, w), lambda i: (i, 0))
    const = lambda a: pl.BlockSpec(a.shape, lambda i: (0, 0), pipeline_mode=pl.Buffered(1))
    return pl.pallas_call(
        functools.partial(_finish_body, alpha=alpha, d_lru=d_lru, d_ff=d_ff, fc=fc),
        grid=(n // tm,),
        in_specs=[row(d_model), row(d_lru), row(mix_att.shape[1]),
                  const(wo), const(wfi), const(wfo), const(pvec)],
        out_specs=row(d_model),
        out_shape=jax.ShapeDtypeStruct((n, d_model), F32),
        compiler_params=_params("parallel"),
        name="finish",
    )(x2d, mix_lru, mix_att, wo, wfi, wfo, pvec)


def _block_diag(w):
    nb, bi, bo = w.shape
    eye = jnp.eye(nb, dtype=w.dtype)
    return (w[:, :, None, :] * eye[:, None, :, None]).reshape(nb * bi, nb * bo)


def _row_tile(n, want):
    tm = min(n, want)
    while n % tm:
        tm //= 2
    return tm


def kernel(x_prompt, x_sample, cache_k, cache_v, cache_kidx, state_conv, state_h, page_table,
           w_in, conv_w, conv_b, w_a, b_a, w_x, b_x, lam, gn_lru, gn_att, w_out,
           ln1_g, ln1_b, w_ffn_in, w_ffn_out, ln2_g, ln2_b):
    depth, d_model, d_in = w_in.shape
    bsz, seq, _ = x_prompt.shape
    dbs, dseq, _ = x_sample.shape
    d_lru = conv_w.shape[2]
    n_phys, page, n_heads, head_dim = cache_k.shape[1:]
    d_att = n_heads * head_dim
    idx_dim = cache_kidx.shape[3]
    d_qi = IDX_HEADS * idx_dim
    n_pages = page_table.shape[1]
    past_len = n_pages * page
    alpha = (2.0 * depth) ** 0.25
    assert d_in == 2 * d_lru + 3 * d_att + d_qi + idx_dim + IDX_HEADS
    assert idx_dim + IDX_HEADS <= LANES and LANES % head_dim == 0 and head_dim == idx_dim
    geom = dict(d_lru=d_lru, d_att=d_att, d_qi=d_qi, idx_dim=idx_dim, head_dim=head_dim)
    d_main = d_in - idx_dim - IDX_HEADS

    pos_p = jnp.tile(jnp.arange(seq), bsz)
    pos_s = jnp.tile(past_len + jnp.arange(dseq), dbs)
    topk_p = min(TOPK_MAX, seq // 4)
    topk_s = min(TOPK_MAX, (past_len + dseq) // 4)

    xp = x_prompt.reshape(bsz * seq, d_model)
    xs = x_sample.reshape(dbs * dseq, d_model)
    outs_p, outs_s = [], []
    for l in range(depth):
        w_pad = jnp.pad(w_in[l], ((0, 0), (0, d_main + LANES - d_in))).astype(BF16)
        wa_bd = _block_diag(w_a[l]).astype(BF16)
        wx_bd = _block_diag(w_x[l]).astype(BF16)
        lru_vec = jnp.stack([conv_b[l], b_a[l], b_x[l], lam[l], gn_lru[l]]
                            + [jnp.zeros_like(lam[l])] * 3)
        fin_vec = jnp.stack([ln1_g[l], ln1_b[l], ln2_g[l], ln2_b[l]])
        wo, wfi, wfo = (w_out[l].astype(BF16), w_ffn_in[l].astype(BF16),
                        w_ffn_out[l].astype(BF16))
        gn_a = gn_att[l][None, :]

        xl, gate, qb, k, kb, v, vb, qib, tail = _project(
            xp, w_pad, pos_p, tm=_row_tile(bsz * seq, 512), **geom)
        b3 = lambda a: a.reshape(bsz, seq, a.shape[-1])
        xl3 = b3(xl)
        mix_l, h_last = _lru_prompt(
            xl3, b3(gate), jnp.zeros((bsz, CONV_W - 1, d_lru), F32), jnp.zeros((bsz, d_lru), F32),
            conv_w[l], lru_vec, wa_bd, wx_bd, tt=_row_tile(seq, 256))
        tail3 = b3(tail)
        ki = tail3[:, :, :idx_dim]
        mix_a = _attn_prompt(b3(qib), tail3[:, :, idx_dim:idx_dim + IDX_HEADS], b3(qb),
                             ki.astype(BF16), b3(kb), b3(vb), gn_a, tq=_row_tile(seq, 256),
                             topk=topk_p, n_heads=n_heads, idx_dim=idx_dim)
        xp = _finish(xp, mix_l.reshape(bsz * seq, d_lru), mix_a.reshape(bsz * seq, d_att),
                     wo, wfi, wfo, fin_vec, alpha=alpha, tm=_row_tile(bsz * seq, 512))
        outs_p.append((k.reshape(bsz, seq, n_heads, head_dim), v.reshape(bsz, seq, n_heads, head_dim),
                       ki, xl3[:, seq - (CONV_W - 1):], h_last[:, 0]))

        xl, gate, qb, k, kb, v, vb, qib, tail = _project(
            xs, w_pad, pos_s, tm=_row_tile(dbs * dseq, 256), **geom)
        d3 = lambda a: a.reshape(dbs, dseq, a.shape[-1])
        tm_major = lambda a: jnp.swapaxes(d3(a), 0, 1)
        xl3 = d3(xl)
        mix_l, h_last = _lru_sample(
            tm_major(xl), tm_major(gate), jnp.swapaxes(state_conv[l], 0, 1).astype(F32),
            state_h[l], conv_w[l], lru_vec, wa_bd, wx_bd)
        tail3 = d3(tail)
        ki = tail3[:, :, :idx_dim]
        wi = tail3[:, :, idx_dim:idx_dim + IDX_HEADS]
        mix_a = _attn_sample(
            d3(qib), wi, d3(qb), ki.astype(BF16), d3(kb), d3(vb),
            cache_k[l], cache_v[l], cache_kidx[l], page_table, gn_a, topk=topk_s,
            n_heads=n_heads)
        xs = _finish(xs, jnp.swapaxes(mix_l, 0, 1).reshape(dbs * dseq, d_lru),
                     mix_a.reshape(dbs * dseq, d_att), wo, wfi, wfo, fin_vec, alpha=alpha,
                     tm=_row_tile(dbs * dseq, 256))
        conv_new = jnp.concatenate([state_conv[l].astype(F32), xl3], axis=1)[:, -(CONV_W - 1):]
        outs_s.append((k.reshape(dbs, dseq, n_heads, head_dim), v.reshape(dbs, dseq, n_heads, head_dim),
                       ki, conv_new, h_last))

    stack = lambda outs, j: jnp.stack([o[j] for o in outs])
    return (xp.reshape(bsz, seq, d_model), xs.reshape(dbs, dseq, d_model),
            *(stack(outs_p, j) for j in range(5)), *(stack(outs_s, j) for j in range(5)))
```

```python
import functools

import jax
import jax.numpy as jnp
import numpy as np
from jax import lax
from jax.experimental import pallas as pl
from jax.experimental.pallas import tpu as pltpu

CONV_W = 4
LRU_C = 8.0
LRU_BLOCKS = 8
IDX_HEADS = 8
TOPK_MAX = 256
ROPE_FRACTION = 4
ROPE_THETA = 500000.0
RMS_EPS = 1e-6
LN_EPS = 1e-5

LANES = 128
SUBLANES = 8
VMEM_LIMIT = 56 * 1024 * 1024
MASKED = -1e30
INT_MIN = -2 ** 31

F32 = jnp.float32
BF16 = jnp.bfloat16
I32 = jnp.int32


def _params(*sem):
    return pltpu.CompilerParams(dimension_semantics=sem, vmem_limit_bytes=VMEM_LIMIT)


def _nt_dot(a, b):
    return lax.dot_general(a, b, (((1,), (1,)), ((), ())), preferred_element_type=F32)


def _proj_streams(x_ref, w_ref, c_ref, s1_ref, s2_ref, *, d_lru, d_att, d_qi, idx_dim, rope_half,
                  q_scale):
    xb = x_ref[...].astype(BF16)
    c, s1, s2 = c_ref[...], s1_ref[...], s2_ref[...]

    def proj(lo, width):
        return jnp.dot(xb, w_ref[:, lo:lo + width], preferred_element_type=F32)

    def tiled(t, width):
        reps = width // LANES
        return t if reps == 1 else jnp.concatenate([t] * reps, axis=1)

    def rope(z, cc, ss1, ss2):
        width = z.shape[1]
        return (z * cc + pltpu.roll(z, width - rope_half, 1) * ss1
                + pltpu.roll(z, rope_half, 1) * ss2)

    lo = 0
    xl = proj(lo, d_lru); lo += d_lru
    gate = proj(lo, d_lru); lo += d_lru
    q = rope(proj(lo, d_att), tiled(c, d_att), tiled(s1, d_att), tiled(s2, d_att)); lo += d_att
    k = rope(proj(lo, d_att), tiled(c, d_att), tiled(s1, d_att), tiled(s2, d_att)); lo += d_att
    v = proj(lo, d_att); lo += d_att
    qi = rope(proj(lo, d_qi), tiled(c, d_qi), tiled(s1, d_qi), tiled(s2, d_qi)); lo += d_qi
    tail = proj(lo, LANES)
    is_key = lax.broadcasted_iota(I32, tail.shape, 1) < idx_dim
    tail = rope(tail, jnp.where(is_key, c, 1.0), jnp.where(is_key, s1, 0.0),
                jnp.where(is_key, s2, 0.0))
    return xl, gate, q * q_scale, k, v, qi, tail


def _proj_rows_body(x_ref, w_ref, c_ref, s1_ref, s2_ref,
                    xl_ref, gate_ref, qb_ref, k_ref, kb_ref, v_ref, vb_ref, qib_ref, tail_ref,
                    **geom):
    xl, gate, q, k, v, qi, tail = _proj_streams(x_ref, w_ref, c_ref, s1_ref, s2_ref, **geom)
    xl_ref[...] = xl
    gate_ref[...] = gate
    qb_ref[...] = q.astype(BF16)
    k_ref[...] = k
    kb_ref[...] = k.astype(BF16)
    v_ref[...] = v
    vb_ref[...] = v.astype(BF16)
    qib_ref[...] = qi.astype(BF16)
    tail_ref[...] = tail


def _proj_cols_body(x_ref, w_ref, c_ref, s1_ref, s2_ref,
                    xl_ref, gate_ref, qt_ref, qit_ref, kt_ref, khm_ref, vt_ref, vtc_ref,
                    kidxt_ref, kidx_ref, wt_ref, *, cha, n_heads, idx_heads, **geom):
    xl, gate, q, k, v, qi, tail = _proj_streams(x_ref, w_ref, c_ref, s1_ref, s2_ref, **geom)
    idx_dim = geom["idx_dim"]
    head_dim = k.shape[1] // n_heads
    xl_ref[...] = xl
    gate_ref[...] = gate
    qt_ref[0] = q.T.astype(BF16)
    qit_ref[0] = qi.T.astype(BF16)
    kt_ref[0] = k.T
    for h in range(n_heads):
        khm_ref[0, h] = k[:, h * head_dim:(h + 1) * head_dim].astype(BF16)
    vt = v.T
    vt_ref[0] = vt
    for j in range(vt.shape[1] // cha):
        vtc_ref[0, j] = vt[:, j * cha:(j + 1) * cha].astype(BF16)
    tail_t = tail.T
    kidxt_ref[0] = tail_t[:idx_dim]
    wt_ref[0] = tail_t[idx_dim:idx_dim + idx_heads]
    kidx_ref[0] = tail[:, :idx_dim].astype(BF16)


def _rope_tables(pos, head_dim):
    rope_dim = head_dim // ROPE_FRACTION
    half = rope_dim // 2
    freqs = ROPE_THETA ** (-jnp.arange(half, dtype=F32) / half)
    ang = pos.astype(F32)[:, None] * freqs[None, :]
    cos, sin = jnp.cos(ang), jnp.sin(ang)
    n = pos.shape[0]
    rest = head_dim - rope_dim
    c = jnp.concatenate([cos, cos, jnp.ones((n, rest), F32)], 1)
    s1 = jnp.concatenate([-sin, jnp.zeros((n, half + rest), F32)], 1)
    s2 = jnp.concatenate([jnp.zeros((n, half), F32), sin, jnp.zeros((n, rest), F32)], 1)
    reps = LANES // head_dim
    return [jnp.tile(t, (1, reps)) for t in (c, s1, s2)], half


def _project_rows(x2d, w_pad, pos, *, d_lru, d_att, d_qi, idx_dim, head_dim, tm):
    n, d_model = x2d.shape
    (c, s1, s2), half = _rope_tables(pos, head_dim)
    row = lambda w: pl.BlockSpec((tm, w), lambda i: (i, 0))
    outs = [
        (d_lru, F32), (d_lru, F32), (d_att, BF16), (d_att, F32), (d_att, BF16),
        (d_att, F32), (d_att, BF16), (d_qi, BF16), (LANES, F32)]
    body = functools.partial(_proj_rows_body, d_lru=d_lru, d_att=d_att, d_qi=d_qi,
                             idx_dim=idx_dim, rope_half=half, q_scale=head_dim ** -0.5)
    return pl.pallas_call(
        body,
        grid=(n // tm,),
        in_specs=[row(d_model), pl.BlockSpec(w_pad.shape, lambda i: (0, 0)),
                  row(LANES), row(LANES), row(LANES)],
        out_specs=[row(w) for w, _ in outs],
        out_shape=[jax.ShapeDtypeStruct((n, w), dt) for w, dt in outs],
        compiler_params=_params("parallel"),
        name="proj",
    )(x2d, w_pad, c, s1, s2)


def _project_cols(x3d, w_pad, *, d_lru, d_att, d_qi, idx_dim, head_dim, tm, cha, n_heads,
                  idx_heads):
    bsz, t, d_model = x3d.shape
    n, nt = bsz * t, t // tm
    (c, s1, s2), half = _rope_tables(jnp.arange(t), head_dim)
    row = lambda w: pl.BlockSpec((tm, w), lambda i: (i, 0))
    table = pl.BlockSpec((tm, LANES), lambda i: (i % nt, 0))
    col = lambda w: pl.BlockSpec((1, w, tm), lambda i: (i // nt, 0, i % nt))
    col_shape = lambda w, dt: jax.ShapeDtypeStruct((bsz, w, t), dt)
    body = functools.partial(
        _proj_cols_body, d_lru=d_lru, d_att=d_att, d_qi=d_qi, idx_dim=idx_dim, rope_half=half,
        q_scale=head_dim ** -0.5, cha=cha, n_heads=n_heads, idx_heads=idx_heads)
    return pl.pallas_call(
        body,
        grid=(n // tm,),
        in_specs=[row(d_model), pl.BlockSpec(w_pad.shape, lambda i: (0, 0)), table, table, table],
        out_specs=[
            row(d_lru), row(d_lru), col(d_att), col(d_qi), col(d_att),
            pl.BlockSpec((1, n_heads, tm, head_dim), lambda i: (i // nt, 0, i % nt, 0)),
            col(d_att),
            pl.BlockSpec((1, tm // cha, d_att, cha), lambda i: (i // nt, i % nt, 0, 0)),
            col(idx_dim), pl.BlockSpec((1, tm, idx_dim), lambda i: (i // nt, i % nt, 0)),
            col(idx_heads)],
        out_shape=[
            jax.ShapeDtypeStruct((n, d_lru), F32), jax.ShapeDtypeStruct((n, d_lru), F32),
            col_shape(d_att, BF16), col_shape(d_qi, BF16), col_shape(d_att, F32),
            jax.ShapeDtypeStruct((bsz, n_heads, t, head_dim), BF16),
            col_shape(d_att, F32),
            jax.ShapeDtypeStruct((bsz, t // cha, d_att, cha), BF16),
            col_shape(idx_dim, F32), jax.ShapeDtypeStruct((bsz, t, idx_dim), BF16),
            col_shape(idx_heads, F32)],
        compiler_params=_params("parallel"),
        name="proj_prompt",
    )(x3d.reshape(n, d_model), w_pad, c, s1, s2)


def _softplus(x):
    return jnp.maximum(x, 0.0) + jnp.log1p(jnp.exp(-jnp.abs(x)))


def _gelu_tanh(x):
    return 0.5 * x * (1.0 + jnp.tanh(np.sqrt(2.0 / np.pi).astype(np.float32)
                                     * (x + 0.044715 * (x * x * x))))


def _lru_gates(xc, wa_ref, wx_ref, b_a, b_x, lam):
    xcb = xc.astype(BF16)
    r = jax.nn.sigmoid(jnp.dot(xcb, wa_ref[...], preferred_element_type=F32) + b_a)
    i = jax.nn.sigmoid(jnp.dot(xcb, wx_ref[...], preferred_element_type=F32) + b_x)
    log_a = -LRU_C * r * _softplus(-lam)
    a = jnp.exp(log_a)
    t = jnp.tanh(log_a)
    b = jnp.sqrt(-2.0 * t / (1.0 - t)) * (i * xc)
    return a, b


def _rms_gain(y, g):
    return y * lax.rsqrt(jnp.mean(y * y, axis=-1, keepdims=True) + RMS_EPS) * g


def _lru_prompt_body(xl_ref, gate_ref, cprev_ref, h0_ref, cw_ref, p_ref, wa_ref, wx_ref,
                     mix_ref, hlast_ref, ext_ref, hc_ref, *, tt):
    j = pl.program_id(1)

    @pl.when(j == 0)
    def _():
        ext_ref[0:SUBLANES, :] = cprev_ref[0]
        hc_ref[0:1, :] = h0_ref[0]

    xl = xl_ref[0]
    ext_ref[SUBLANES:SUBLANES + tt, :] = xl
    conv_b, b_a, b_x, lam, gn = (p_ref[r:r + 1, :] for r in range(5))
    xc = conv_b + (cw_ref[0:1, :] * ext_ref[SUBLANES - 3:SUBLANES - 3 + tt, :]
                   + cw_ref[1:2, :] * ext_ref[SUBLANES - 2:SUBLANES - 2 + tt, :]
                   + cw_ref[2:3, :] * ext_ref[SUBLANES - 1:SUBLANES - 1 + tt, :]
                   + cw_ref[3:4, :] * xl)
    ext_ref[0:SUBLANES, :] = ext_ref[tt:tt + SUBLANES, :]

    a, b = _lru_gates(xc, wa_ref, wx_ref, b_a, b_x, lam)
    row = lax.broadcasted_iota(I32, a.shape, 0)
    d = 1
    while d < tt:
        keep = row >= d
        a_prev = jnp.where(keep, pltpu.roll(a, d, 0), 1.0)
        b_prev = jnp.where(keep, pltpu.roll(b, d, 0), 0.0)
        b = a * b_prev + b
        a = a * a_prev
        d *= 2
    h = a * hc_ref[0:1, :] + b
    hc_ref[0:1, :] = h[tt - 1:tt, :]
    hlast_ref[0] = h[tt - 1:tt, :]
    mix_ref[0] = _rms_gain(h * _gelu_tanh(gate_ref[0]), gn).astype(BF16)


def _lru_prompt(xl, gate, conv_prev, h0, conv_w, pvec, wa_bd, wx_bd, *, tt):
    bsz, t, d = xl.shape
    cprev8 = jnp.concatenate(
        [jnp.zeros((bsz, SUBLANES - (CONV_W - 1), d), F32), conv_prev.astype(F32)], axis=1)
    const = lambda shape: pl.BlockSpec(shape, lambda b, j: (0,) * len(shape))
    return pl.pallas_call(
        functools.partial(_lru_prompt_body, tt=tt),
        grid=(bsz, t // tt),
        in_specs=[pl.BlockSpec((1, tt, d), lambda b, j: (b, j, 0)),
                  pl.BlockSpec((1, tt, d), lambda b, j: (b, j, 0)),
                  pl.BlockSpec((1, SUBLANES, d), lambda b, j: (b, 0, 0)),
                  pl.BlockSpec((1, 1, d), lambda b, j: (b, 0, 0)),
                  const(conv_w.shape), const(pvec.shape), const(wa_bd.shape), const(wx_bd.shape)],
        out_specs=[pl.BlockSpec((1, tt, d), lambda b, j: (b, j, 0)),
                   pl.BlockSpec((1, 1, d), lambda b, j: (b, 0, 0))],
        out_shape=[jax.ShapeDtypeStruct((bsz, t, d), BF16),
                   jax.ShapeDtypeStruct((bsz, 1, d), F32)],
        scratch_shapes=[pltpu.VMEM((tt + SUBLANES, d), F32), pltpu.VMEM((SUBLANES, d), F32)],
        compiler_params=_params("parallel", "arbitrary"),
        name="lru_prompt",
    )(xl, gate, cprev8, h0.astype(F32)[:, None, :], conv_w, pvec, wa_bd, wx_bd)


def _lru_sample_body(xl_ref, gate_ref, cprev_ref, h0_ref, cw_ref, p_ref, wa_ref, wx_ref,
                     mix_ref, hlast_ref, *, t_len):
    conv_b, b_a, b_x, lam, gn = (p_ref[r:r + 1, :] for r in range(5))
    xp = [cprev_ref[s] for s in range(CONV_W - 1)] + [xl_ref[s] for s in range(t_len)]
    h = h0_ref[...]
    for s in range(t_len):
        xc = conv_b + (cw_ref[0:1, :] * xp[s] + cw_ref[1:2, :] * xp[s + 1]
                       + cw_ref[2:3, :] * xp[s + 2] + cw_ref[3:4, :] * xp[s + 3])
        a, b = _lru_gates(xc, wa_ref, wx_ref, b_a, b_x, lam)
        h = a * h + b
        mix_ref[s] = _rms_gain(h * _gelu_tanh(gate_ref[s]), gn).astype(BF16)
    hlast_ref[...] = h


def _lru_sample(xl_t, gate_t, cprev_t, h0, conv_w, pvec, wa_bd, wx_bd):
    t_len, dbs, d = xl_t.shape
    return pl.pallas_call(
        functools.partial(_lru_sample_body, t_len=t_len),
        out_shape=[jax.ShapeDtypeStruct((t_len, dbs, d), BF16),
                   jax.ShapeDtypeStruct((dbs, d), F32)],
        compiler_params=pltpu.CompilerParams(vmem_limit_bytes=VMEM_LIMIT),
        name="lru_sample",
    )(xl_t, gate_t, cprev_t, h0.astype(F32), conv_w, pvec, wa_bd, wx_bd)


KEY_NEG_INF = INT_MIN + 0x7FFFFF
REFINE_STEPS = 16


def _key_to_float(key):
    return pltpu.bitcast(jnp.where(key >= 0, key, key ^ 0x7FFFFFFF), F32)


def _threshold_search(count, total, topk, idx_bits, theta_ref, pos_ref):
    shape = pos_ref.shape

    def value_step(it, carry):
        base, n_base = carry
        trial = base ^ lax.shift_left(jnp.int32(1), jnp.int32(31) - it)
        trial_f = _key_to_float(trial)
        n = count(lambda s, idx: s >= trial_f)
        ok = n >= topk
        return jnp.where(ok, trial, base), jnp.where(ok, n, n_base)

    theta_key, n_ge = lax.fori_loop(
        0, 32, value_step, (jnp.full(shape, INT_MIN, I32), jnp.zeros(shape, I32) + total))
    theta = jnp.where(theta_key < KEY_NEG_INF, -jnp.inf, _key_to_float(theta_key))
    theta_ref[...] = theta
    pos_ref[...] = jnp.full(shape, 2 ** idx_bits - 1, I32)

    @pl.when(jnp.max(n_ge) > topk)
    def _():
        def refine_step(_, carry):
            lo, hi = carry
            mid = lo + 0.5 * (hi - lo)
            ok = count(lambda s, idx: s >= mid) >= topk
            return jnp.where(ok, mid, lo), jnp.where(ok, hi, mid)

        theta_fine, _ = lax.fori_loop(
            0, REFINE_STEPS, refine_step,
            (theta, _key_to_float(jnp.maximum(theta_key, KEY_NEG_INF) + 1)))
        theta_ref[...] = theta_fine
        need = topk - count(lambda s, idx: s > theta_fine)

        def index_step(it, pos):
            trial = pos + lax.shift_left(jnp.int32(1), jnp.int32(idx_bits - 1) - it)
            below = count(lambda s, idx: (s == theta_fine) & (idx < trial))
            return jnp.where(below < need, trial, pos)

        pos_ref[...] = lax.fori_loop(0, idx_bits, index_step, jnp.zeros(shape, I32))


def _selected(score, idx, theta, pos):
    return (score > theta) | ((score == theta) & (idx <= pos))


def _attn_prompt_body(qit_ref, wt_ref, qt_ref, kidx_ref, k_ref, vt_ref, gn_ref, out_ref,
                      sc_ref, y_ref, theta_ref, pos_ref, m_ref, l_ref, s_ref, mx_ref, *, tq, ch,
                      cha, topk, idx_bits, n_heads, head_dim, idx_heads, idx_dim):
    i = pl.program_id(1)
    n_keys = (i + 1) * tq
    qpos = i * tq + lax.broadcasted_iota(I32, (1, tq), 1)
    sub = LANES
    kpos_sub = lax.broadcasted_iota(I32, (sub, tq), 0)
    kpos_ch = lax.broadcasted_iota(I32, (ch, tq), 0)
    nc = (i + 1) * (tq // ch)

    def score_chunk(c, carry):
        for j in range(ch // sub):
            start = pl.multiple_of(c * ch + j * sub, sub)
            rows = pl.ds(start, sub)
            kc = kidx_ref[0, rows, :]
            acc = jnp.zeros((sub, tq), F32)
            for h in range(idx_heads):
                d = jnp.dot(kc, qit_ref[0, h * idx_dim:(h + 1) * idx_dim, :],
                            preferred_element_type=F32)
                acc = acc + jnp.maximum(d, 0.0) * wt_ref[0, h:h + 1, :]
            sc_ref[rows, :] = jnp.where(start + kpos_sub <= qpos, acc, -jnp.inf)
        return carry

    lax.fori_loop(0, nc, score_chunk, 0)

    def count(pred):
        def body(c, acc):
            rows = pl.ds(pl.multiple_of(c * ch, ch), ch)
            hit = jnp.where(pred(sc_ref[rows, :], c * ch + kpos_ch), 1, 0).astype(I32)
            return acc + jnp.sum(hit.reshape(ch // SUBLANES, SUBLANES, tq), axis=0)
        acc = lax.fori_loop(0, nc, body, jnp.zeros((SUBLANES, tq), I32))
        return jnp.sum(acc, axis=0, keepdims=True)

    _threshold_search(count, n_keys, topk, idx_bits, theta_ref, pos_ref)
    theta, pos = theta_ref[...], pos_ref[...]

    def bias_chunk(c, carry):
        rows = pl.ds(pl.multiple_of(c * ch, ch), ch)
        idx = c * ch + kpos_ch
        sel = _selected(sc_ref[rows, :], idx, theta, pos) & (idx <= qpos)
        sc_ref[rows, :] = jnp.where(sel, 0.0, MASKED)
        return carry

    lax.fori_loop(0, nc, bias_chunk, 0)

    heads = [(h, slice(h * head_dim, (h + 1) * head_dim)) for h in range(n_heads)]
    m_ref[...] = jnp.full(m_ref.shape, -jnp.inf, F32)
    l_ref[...] = jnp.zeros(l_ref.shape, F32)
    y_ref[...] = jnp.zeros(y_ref.shape, F32)

    nca = (i + 1) * (tq // cha)

    def logits(c, slot):
        rows = pl.ds(pl.multiple_of(c * cha, cha), cha)
        bias = sc_ref[rows, :]
        for h, hs in heads:
            s = jnp.dot(k_ref[0, h, rows, :], qt_ref[0, hs, :],
                        preferred_element_type=F32) + bias
            s_ref[slot, h] = s
            mx_ref[slot, h:h + 1, :] = jnp.max(s, axis=0, keepdims=True)

    def update(c, slot):
        for h, hs in heads:
            m = m_ref[h:h + 1, :]
            m_new = jnp.maximum(m, mx_ref[slot, h:h + 1, :])
            alpha = jnp.exp(m - m_new)
            p = jnp.exp(s_ref[slot, h] - m_new)
            m_ref[h:h + 1, :] = m_new
            l_ref[h:h + 1, :] = alpha * l_ref[h:h + 1, :] + jnp.sum(p, axis=0, keepdims=True)
            y_ref[hs, :] = alpha * y_ref[hs, :] + jnp.dot(
                vt_ref[0, c, hs, :], p.astype(BF16), preferred_element_type=F32)

    logits(0, 0)

    def chunk_pair(j, carry):
        c0 = 2 * j
        logits(c0 + 1, 1)
        update(c0, 0)
        logits(jnp.minimum(c0 + 2, nca - 1), 0)
        update(c0 + 1, 1)
        return carry

    lax.fori_loop(0, nca // 2, chunk_pair, 0)
    for h, hs in heads:
        y_ref[hs, :] = y_ref[hs, :] / l_ref[h:h + 1, :]
    out_ref[0] = _rms_gain(y_ref[...].T, gn_ref[...]).astype(BF16)


def _attn_prompt(qit, wt, qt, kidxb, k_hm, vt, gn_att, *, tq, topk, idx_dim):
    bsz, d_att, t = qt.shape
    n_heads, head_dim = k_hm.shape[1], k_hm.shape[3]
    ch = tq
    cha = vt.shape[3]
    assert (tq // cha) % 2 == 0
    idx_bits = max(1, int(t - 1).bit_length())
    cols = lambda a: pl.BlockSpec((1, a.shape[1], tq), lambda b, i: (b, 0, i))
    full = lambda a: pl.BlockSpec((1,) + a.shape[1:], lambda b, i: (b,) + (0,) * (a.ndim - 1))
    body = functools.partial(
        _attn_prompt_body, tq=tq, ch=ch, cha=cha, topk=topk, idx_bits=idx_bits, n_heads=n_heads,
        head_dim=head_dim, idx_heads=wt.shape[1], idx_dim=idx_dim)
    return pl.pallas_call(
        body,
        grid=(bsz, t // tq),
        in_specs=[cols(qit), cols(wt), cols(qt), full(kidxb), full(k_hm), full(vt),
                  pl.BlockSpec((1, d_att), lambda b, i: (0, 0))],
        out_specs=pl.BlockSpec((1, tq, d_att), lambda b, i: (b, i, 0)),
        out_shape=jax.ShapeDtypeStruct((bsz, t, d_att), BF16),
        scratch_shapes=[pltpu.VMEM((t, tq), F32), pltpu.VMEM((d_att, tq), F32),
                        pltpu.VMEM((1, tq), F32), pltpu.VMEM((1, tq), I32),
                        pltpu.VMEM((n_heads, tq), F32), pltpu.VMEM((n_heads, tq), F32),
                        pltpu.VMEM((2, n_heads, cha, tq), F32),
                        pltpu.VMEM((2, n_heads, tq), F32)],
        compiler_params=_params("parallel", "arbitrary"),
        name="attn_prompt",
    )(qit, wt, qt, kidxb, k_hm, vt, gn_att)


def _score_sample_body(pt_ref, qi_ref, w_ref, *refs, n_pages, page, t_len, idx_heads, past_len):
    del pt_ref
    page_refs, new_ref, out_ref = refs[:n_pages], refs[n_pages], refs[n_pages + 1]
    qi = qi_ref[0]
    w = w_ref[0]
    kt = jnp.concatenate([r[0].astype(BF16) for r in page_refs] + [new_ref[0]], axis=1)
    d = jnp.maximum(jnp.dot(qi, kt, preferred_element_type=F32), 0.0) * w
    s = jnp.sum(d.reshape(t_len, idx_heads, past_len + page), axis=1)
    tpos = past_len + lax.broadcasted_iota(I32, s.shape, 0)
    kpos = lax.broadcasted_iota(I32, s.shape, 1)
    out_ref[0] = jnp.where(kpos <= tpos, s, -jnp.inf)


def _select_sample_body(s_ref, bias_ref, sc_ref, theta_ref, pos_ref, *, topk, idx_bits, page):
    nc, rows, _ = sc_ref.shape
    for c in range(nc):
        sc_ref[c] = s_ref[:, c * page:(c + 1) * page]
    lane = lax.broadcasted_iota(I32, (rows, page), 1)

    def count(pred):
        def body(c, acc):
            return acc + jnp.where(pred(sc_ref[c], c * page + lane), 1, 0).astype(I32)
        acc = lax.fori_loop(0, nc, body, jnp.zeros((rows, page), I32))
        return jnp.sum(acc, axis=1, keepdims=True)

    _threshold_search(count, nc * page, topk, idx_bits, theta_ref, pos_ref)
    theta, pos = theta_ref[...], pos_ref[...]
    for c in range(nc):
        s = sc_ref[c]
        sel = _selected(s, c * page + lane, theta, pos) & (s > -jnp.inf)
        bias_ref[:, c * page:(c + 1) * page] = jnp.where(sel, 0.0, MASKED)


def _attn_sample_body(pt_ref, q_ref, bias_ref, gn_ref, *refs, n_pages, page, t_len, n_heads,
                      head_dim):
    del pt_ref
    k_refs, v_refs = refs[:n_pages], refs[n_pages:2 * n_pages]
    knew_ref, vnew_ref, out_ref = refs[2 * n_pages:]
    past = n_pages * page
    bias = bias_ref[0]
    outs = []
    del past
    for h in range(n_heads):
        kt = jnp.concatenate([r[0, h].astype(BF16) for r in k_refs] + [knew_ref[0, h]], axis=1)
        vt = jnp.concatenate([r[0, h].astype(BF16) for r in v_refs] + [vnew_ref[0, h]], axis=1)
        s = jnp.dot(q_ref[0, h], kt, preferred_element_type=F32) + bias
        m = jnp.max(s, axis=1, keepdims=True)
        p = jnp.exp(s - m)
        l = jnp.sum(p, axis=1, keepdims=True)
        outs.append(_nt_dot(p.astype(BF16), vt) / l)
    y = jnp.concatenate(outs, axis=1)
    out_ref[0] = _rms_gain(y, gn_ref[...]).astype(BF16)


def _attn_sample(qib, wi, qb, kidx_new, k_new, v_new, cache_k, cache_v, cache_kidx, page_table,
                 gn_att, *, topk, n_heads):
    dbs, t_len, d_att = qb.shape
    n_pages = page_table.shape[1]
    page = cache_k.shape[1]
    idx_dim = cache_kidx.shape[2]
    idx_heads = wi.shape[2]
    head_dim = d_att // n_heads
    nkp = (n_pages + 1) * page
    past_len = n_pages * page
    pad = lambda a: jnp.pad(a, ((0, 0), (0, page - t_len), (0, 0)))
    kidx_t = jnp.transpose(cache_kidx, (0, 2, 1))
    k_t = jnp.transpose(cache_k, (0, 2, 3, 1))
    v_t = jnp.transpose(cache_v, (0, 2, 3, 1))
    new_t = lambda a: jnp.transpose(pad(a).reshape(dbs, page, n_heads, head_dim), (0, 2, 3, 1))

    def paged(shape):
        return [pl.BlockSpec((1,) + shape, functools.partial(
            lambda b, pt, p: (pt[b, p],) + (0,) * len(shape), p=p)) for p in range(n_pages)]

    per_seq = lambda shape: pl.BlockSpec((1,) + shape, lambda b, pt: (b,) + (0,) * len(shape))

    scores = pl.pallas_call(
        functools.partial(_score_sample_body, n_pages=n_pages, page=page, t_len=t_len,
                          idx_heads=idx_heads, past_len=past_len),
        grid_spec=pltpu.PrefetchScalarGridSpec(
            num_scalar_prefetch=1, grid=(dbs,),
            in_specs=[per_seq((t_len * idx_heads, idx_dim)), per_seq((t_len * idx_heads, 1))]
                     + paged((idx_dim, page)) + [per_seq((idx_dim, page))],
            out_specs=per_seq((t_len, nkp))),
        out_shape=jax.ShapeDtypeStruct((dbs, t_len, nkp), F32),
        compiler_params=_params("parallel"),
        name="score_sample",
    )(page_table, qib.reshape(dbs, t_len * idx_heads, idx_dim),
      wi.reshape(dbs, t_len * idx_heads, 1), *([kidx_t] * n_pages),
      jnp.transpose(pad(kidx_new), (0, 2, 1)))

    rows = dbs * t_len
    tr = min(rows, 128)
    bias = pl.pallas_call(
        functools.partial(_select_sample_body, topk=topk,
                          idx_bits=max(1, int(nkp - 1).bit_length()), page=page),
        grid=(rows // tr,),
        in_specs=[pl.BlockSpec((tr, nkp), lambda r: (r, 0))],
        out_specs=pl.BlockSpec((tr, nkp), lambda r: (r, 0)),
        out_shape=jax.ShapeDtypeStruct((rows, nkp), F32),
        scratch_shapes=[pltpu.VMEM((n_pages + 1, tr, page), F32), pltpu.VMEM((tr, 1), F32),
                        pltpu.VMEM((tr, 1), I32)],
        compiler_params=_params("parallel"),
        name="select_sample",
    )(scores.reshape(rows, nkp)).reshape(dbs, t_len, nkp)

    kv_pages = paged((n_heads, head_dim, page))
    return pl.pallas_call(
        functools.partial(_attn_sample_body, n_pages=n_pages, page=page, t_len=t_len,
                          n_heads=n_heads, head_dim=head_dim),
        grid_spec=pltpu.PrefetchScalarGridSpec(
            num_scalar_prefetch=1, grid=(dbs,),
            in_specs=[per_seq((n_heads, t_len, head_dim)), per_seq((t_len, nkp)),
                      pl.BlockSpec((1, d_att), lambda b, pt: (0, 0))]
                     + kv_pages + kv_pages
                     + [per_seq((n_heads, head_dim, page)), per_seq((n_heads, head_dim, page))],
            out_specs=per_seq((t_len, d_att))),
        out_shape=jax.ShapeDtypeStruct((dbs, t_len, d_att), BF16),
        compiler_params=_params("parallel"),
        name="attn_sample",
    )(page_table, qb.reshape(dbs, t_len, n_heads, head_dim).transpose(0, 2, 1, 3), bias, gn_att,
      *([k_t] * n_pages), *([v_t] * n_pages), new_t(k_new), new_t(v_new))


def _layer_norm(x, g, b):
    mu = jnp.mean(x, axis=-1, keepdims=True)
    xc = x - mu
    var = jnp.mean(xc * xc, axis=-1, keepdims=True)
    return xc * lax.rsqrt(var + LN_EPS) * g + b


def _finish_body(x_ref, ml_ref, ma_ref, wo_ref, wfi_ref, wfo_ref, p_ref, out_ref,
                 *, alpha, d_lru, d_ff, fc):
    ln1_g, ln1_b, ln2_g, ln2_b = (p_ref[r:r + 1, :] for r in range(4))
    y = (jnp.dot(ml_ref[...], wo_ref[0:d_lru, :], preferred_element_type=F32)
         + jnp.dot(ma_ref[...], wo_ref[d_lru:, :], preferred_element_type=F32))
    x1 = _layer_norm(alpha * x_ref[...] + y, ln1_g, ln1_b)
    x1b = x1.astype(BF16)
    f = jnp.zeros(x1.shape, F32)
    for c in range(d_ff // fc):
        u = jnp.dot(x1b, wfi_ref[:, c * fc:(c + 1) * fc], preferred_element_type=F32)
        g = jnp.dot(x1b, wfi_ref[:, d_ff + c * fc:d_ff + (c + 1) * fc],
                    preferred_element_type=F32)
        hidden = (g * jax.nn.sigmoid(g) * u).astype(BF16)
        f = f + jnp.dot(hidden, wfo_ref[c * fc:(c + 1) * fc, :], preferred_element_type=F32)
    out_ref[...] = _layer_norm(alpha * x1 + f, ln2_g, ln2_b)


def _finish(x2d, mix_lru, mix_att, wo, wfi, wfo, pvec, *, alpha, tm):
    n, d_model = x2d.shape
    d_lru = mix_lru.shape[1]
    d_ff = wfo.shape[0]
    fc = 2 * LANES if d_ff % (2 * LANES) == 0 else LANES
    row = lambda w: pl.BlockSpec((tm, w), lambda i: (i, 0))
    const = lambda a: pl.BlockSpec(a.shape, lambda i: (0, 0), pipeline_mode=pl.Buffered(1))
    return pl.pallas_call(
        functools.partial(_finish_body, alpha=alpha, d_lru=d_lru, d_ff=d_ff, fc=fc),
        grid=(n // tm,),
        in_specs=[row(d_model), row(d_lru), row(mix_att.shape[1]),
                  const(wo), const(wfi), const(wfo), const(pvec)],
        out_specs=row(d_model),
        out_shape=jax.ShapeDtypeStruct((n, d_model), F32),
        compiler_params=_params("parallel"),
        name="finish",
    )(x2d, mix_lru, mix_att, wo, wfi, wfo, pvec)


def _block_diag(w):
    nb, bi, bo = w.shape
    eye = jnp.eye(nb, dtype=w.dtype)
    return (w[:, :, None, :] * eye[:, None, :, None]).reshape(nb * bi, nb * bo)


def _row_tile(n, want):
    tm = min(n, want)
    while n % tm:
        tm //= 2
    return tm


def kernel(x_prompt, x_sample, cache_k, cache_v, cache_kidx, state_conv, state_h, page_table,
           w_in, conv_w, conv_b, w_a, b_a, w_x, b_x, lam, gn_lru, gn_att, w_out,
           ln1_g, ln1_b, w_ffn_in, w_ffn_out, ln2_g, ln2_b):
    depth, d_model, d_in = w_in.shape
    bsz, seq, _ = x_prompt.shape
    dbs, dseq, _ = x_sample.shape
    d_lru = conv_w.shape[2]
    n_phys, page, n_heads, head_dim = cache_k.shape[1:]
    d_att = n_heads * head_dim
    idx_dim = cache_kidx.shape[3]
    d_qi = IDX_HEADS * idx_dim
    n_pages = page_table.shape[1]
    past_len = n_pages * page
    alpha = (2.0 * depth) ** 0.25
    assert d_in == 2 * d_lru + 3 * d_att + d_qi + idx_dim + IDX_HEADS
    assert idx_dim + IDX_HEADS <= LANES and LANES % head_dim == 0 and head_dim == idx_dim
    geom = dict(d_lru=d_lru, d_att=d_att, d_qi=d_qi, idx_dim=idx_dim, head_dim=head_dim)
    d_main = d_in - idx_dim - IDX_HEADS

    pos_p = jnp.tile(jnp.arange(seq), bsz)
    pos_s = jnp.tile(past_len + jnp.arange(dseq), dbs)
    topk_p = min(TOPK_MAX, seq // 4)
    topk_s = min(TOPK_MAX, (past_len + dseq) // 4)

    xp = x_prompt.reshape(bsz * seq, d_model)
    xs = x_sample.reshape(dbs * dseq, d_model)
    outs_p, outs_s = [], []
    for l in range(depth):
        w_pad = jnp.pad(w_in[l], ((0, 0), (0, d_main + LANES - d_in))).astype(BF16)
        wa_bd = _block_diag(w_a[l]).astype(BF16)
        wx_bd = _block_diag(w_x[l]).astype(BF16)
        lru_vec = jnp.stack([conv_b[l], b_a[l], b_x[l], lam[l], gn_lru[l]]
                            + [jnp.zeros_like(lam[l])] * 3)
        fin_vec = jnp.stack([ln1_g[l], ln1_b[l], ln2_g[l], ln2_b[l]])
        wo, wfi, wfo = (w_out[l].astype(BF16), w_ffn_in[l].astype(BF16),
                        w_ffn_out[l].astype(BF16))
        gn_a = gn_att[l][None, :]

        tq = _row_tile(seq, 256)
        xl, gate, qt, qit, kt, k_hm, vt, vt_chunks, kidx_t, kidx_b, wt = _project_cols(
            xp.reshape(bsz, seq, d_model), w_pad, tm=_row_tile(seq, 512), cha=tq // 2,
            n_heads=n_heads, idx_heads=IDX_HEADS, **geom)
        xl3 = xl.reshape(bsz, seq, d_lru)
        mix_l, h_last = _lru_prompt(
            xl3, gate.reshape(bsz, seq, d_lru), jnp.zeros((bsz, CONV_W - 1, d_lru), F32),
            jnp.zeros((bsz, d_lru), F32), conv_w[l], lru_vec, wa_bd, wx_bd, tt=_row_tile(seq, 256))
        mix_a = _attn_prompt(qit, wt, qt, kidx_b, k_hm, vt_chunks, gn_a, tq=tq, topk=topk_p,
                             idx_dim=idx_dim)
        xp = _finish(xp, mix_l.reshape(bsz * seq, d_lru), mix_a.reshape(bsz * seq, d_att),
                     wo, wfi, wfo, fin_vec, alpha=alpha, tm=_row_tile(bsz * seq, 512))
        token_major = lambda a: a.reshape(bsz, n_heads, head_dim, seq).transpose(0, 3, 1, 2)
        outs_p.append((token_major(kt), token_major(vt), jnp.swapaxes(kidx_t, 1, 2),
                       xl3[:, seq - (CONV_W - 1):], h_last[:, 0]))

        xl, gate, qb, k, kb, v, vb, qib, tail = _project_rows(
            xs, w_pad, pos_s, tm=_row_tile(dbs * dseq, 256), **geom)
        d3 = lambda a: a.reshape(dbs, dseq, a.shape[-1])
        tm_major = lambda a: jnp.swapaxes(d3(a), 0, 1)
        xl3 = d3(xl)
        mix_l, h_last = _lru_sample(
            tm_major(xl), tm_major(gate), jnp.swapaxes(state_conv[l], 0, 1).astype(F32),
            state_h[l], conv_w[l], lru_vec, wa_bd, wx_bd)
        tail3 = d3(tail)
        ki = tail3[:, :, :idx_dim]
        wi = tail3[:, :, idx_dim:idx_dim + IDX_HEADS]
        mix_a = _attn_sample(
            d3(qib), wi, d3(qb), ki.astype(BF16), d3(kb), d3(vb),
            cache_k[l], cache_v[l], cache_kidx[l], page_table, gn_a, topk=topk_s,
            n_heads=n_heads)
        xs = _finish(xs, jnp.swapaxes(mix_l, 0, 1).reshape(dbs * dseq, d_lru),
                     mix_a.reshape(dbs * dseq, d_att), wo, wfi, wfo, fin_vec, alpha=alpha,
                     tm=_row_tile(dbs * dseq, 256))
        conv_new = jnp.concatenate([state_conv[l].astype(F32), xl3], axis=1)[:, -(CONV_W - 1):]
        outs_s.append((k.reshape(dbs, dseq, n_heads, head_dim), v.reshape(dbs, dseq, n_heads, head_dim),
                       ki, conv_new, h_last))

    stack = lambda outs, j: jnp.stack([o[j] for o in outs])
    return (xp.reshape(bsz, seq, d_model), xs.reshape(dbs, dseq, d_model),
            *(stack(outs_p, j) for j in range(5)), *(stack(outs_s, j) for j in range(5)))
```

```python
import functools

import jax
import jax.numpy as jnp
import numpy as np
from jax import lax
from jax.experimental import pallas as pl
from jax.experimental.pallas import tpu as pltpu

CONV_W = 4
LRU_C = 8.0
LRU_BLOCKS = 8
IDX_HEADS = 8
TOPK_MAX = 256
ROPE_FRACTION = 4
ROPE_THETA = 500000.0
RMS_EPS = 1e-6
LN_EPS = 1e-5

LANES = 128
SUBLANES = 8
VMEM_LIMIT = 56 * 1024 * 1024
MASKED = -1e30
INT_MIN = -2 ** 31

F32 = jnp.float32
BF16 = jnp.bfloat16
I32 = jnp.int32


def _params(*sem):
    return pltpu.CompilerParams(dimension_semantics=sem, vmem_limit_bytes=VMEM_LIMIT)


def _nt_dot(a, b):
    return lax.dot_general(a, b, (((1,), (1,)), ((), ())), preferred_element_type=F32)


def _proj_streams(x_ref, w_ref, c_ref, s1_ref, s2_ref, *, d_lru, d_att, d_qi, idx_dim, rope_half,
                  q_scale):
    xb = x_ref[...].astype(BF16)
    c, s1, s2 = c_ref[...], s1_ref[...], s2_ref[...]

    def proj(lo, width):
        return jnp.dot(xb, w_ref[:, lo:lo + width], preferred_element_type=F32)

    def tiled(t, width):
        reps = width // LANES
        return t if reps == 1 else jnp.concatenate([t] * reps, axis=1)

    def rope(z, cc, ss1, ss2):
        width = z.shape[1]
        return (z * cc + pltpu.roll(z, width - rope_half, 1) * ss1
                + pltpu.roll(z, rope_half, 1) * ss2)

    lo = 0
    xl = proj(lo, d_lru); lo += d_lru
    gate = proj(lo, d_lru); lo += d_lru
    q = rope(proj(lo, d_att), tiled(c, d_att), tiled(s1, d_att), tiled(s2, d_att)); lo += d_att
    k = rope(proj(lo, d_att), tiled(c, d_att), tiled(s1, d_att), tiled(s2, d_att)); lo += d_att
    v = proj(lo, d_att); lo += d_att
    qi = rope(proj(lo, d_qi), tiled(c, d_qi), tiled(s1, d_qi), tiled(s2, d_qi)); lo += d_qi
    tail = proj(lo, LANES)
    is_key = lax.broadcasted_iota(I32, tail.shape, 1) < idx_dim
    tail = rope(tail, jnp.where(is_key, c, 1.0), jnp.where(is_key, s1, 0.0),
                jnp.where(is_key, s2, 0.0))
    return xl, gate, q * q_scale, k, v, qi, tail


def _proj_rows_body(x_ref, w_ref, c_ref, s1_ref, s2_ref,
                    xl_ref, gate_ref, qb_ref, k_ref, kb_ref, v_ref, vb_ref, qib_ref, tail_ref,
                    **geom):
    xl, gate, q, k, v, qi, tail = _proj_streams(x_ref, w_ref, c_ref, s1_ref, s2_ref, **geom)
    xl_ref[...] = xl
    gate_ref[...] = gate
    qb_ref[...] = q.astype(BF16)
    k_ref[...] = k
    kb_ref[...] = k.astype(BF16)
    v_ref[...] = v
    vb_ref[...] = v.astype(BF16)
    qib_ref[...] = qi.astype(BF16)
    tail_ref[...] = tail


def _proj_cols_body(x_ref, w_ref, c_ref, s1_ref, s2_ref,
                    xl_ref, gate_ref, qt_ref, qit_ref, kt_ref, khm_ref, vt_ref, vtc_ref,
                    kidxt_ref, kidx_ref, wt_ref, *, cha, n_heads, idx_heads, **geom):
    xl, gate, q, k, v, qi, tail = _proj_streams(x_ref, w_ref, c_ref, s1_ref, s2_ref, **geom)
    idx_dim = geom["idx_dim"]
    head_dim = k.shape[1] // n_heads
    xl_ref[...] = xl
    gate_ref[...] = gate
    qt_ref[0] = q.T.astype(BF16)
    qit_ref[0] = qi.T.astype(BF16)
    kt_ref[0] = k.T
    for h in range(n_heads):
        khm_ref[0, h] = k[:, h * head_dim:(h + 1) * head_dim].astype(BF16)
    vt = v.T
    vt_ref[0] = vt
    for j in range(vt.shape[1] // cha):
        vtc_ref[0, j] = vt[:, j * cha:(j + 1) * cha].astype(BF16)
    tail_t = tail.T
    kidxt_ref[0] = tail_t[:idx_dim]
    wt_ref[0] = tail_t[idx_dim:idx_dim + idx_heads]
    kidx_ref[0] = tail[:, :idx_dim].astype(BF16)


def _rope_tables(pos, head_dim):
    rope_dim = head_dim // ROPE_FRACTION
    half = rope_dim // 2
    freqs = ROPE_THETA ** (-jnp.arange(half, dtype=F32) / half)
    ang = pos.astype(F32)[:, None] * freqs[None, :]
    cos, sin = jnp.cos(ang), jnp.sin(ang)
    n = pos.shape[0]
    rest = head_dim - rope_dim
    c = jnp.concatenate([cos, cos, jnp.ones((n, rest), F32)], 1)
    s1 = jnp.concatenate([-sin, jnp.zeros((n, half + rest), F32)], 1)
    s2 = jnp.concatenate([jnp.zeros((n, half), F32), sin, jnp.zeros((n, rest), F32)], 1)
    reps = LANES // head_dim
    return [jnp.tile(t, (1, reps)) for t in (c, s1, s2)], half


def _project_rows(x2d, w_pad, pos, *, d_lru, d_att, d_qi, idx_dim, head_dim, tm):
    n, d_model = x2d.shape
    (c, s1, s2), half = _rope_tables(pos, head_dim)
    row = lambda w: pl.BlockSpec((tm, w), lambda i: (i, 0))
    outs = [
        (d_lru, F32), (d_lru, F32), (d_att, BF16), (d_att, F32), (d_att, BF16),
        (d_att, F32), (d_att, BF16), (d_qi, BF16), (LANES, F32)]
    body = functools.partial(_proj_rows_body, d_lru=d_lru, d_att=d_att, d_qi=d_qi,
                             idx_dim=idx_dim, rope_half=half, q_scale=head_dim ** -0.5)
    return pl.pallas_call(
        body,
        grid=(n // tm,),
        in_specs=[row(d_model), pl.BlockSpec(w_pad.shape, lambda i: (0, 0)),
                  row(LANES), row(LANES), row(LANES)],
        out_specs=[row(w) for w, _ in outs],
        out_shape=[jax.ShapeDtypeStruct((n, w), dt) for w, dt in outs],
        compiler_params=_params("parallel"),
        name="proj",
    )(x2d, w_pad, c, s1, s2)


def _project_cols(x3d, w_pad, *, d_lru, d_att, d_qi, idx_dim, head_dim, tm, cha, n_heads,
                  idx_heads):
    bsz, t, d_model = x3d.shape
    n, nt = bsz * t, t // tm
    (c, s1, s2), half = _rope_tables(jnp.arange(t), head_dim)
    row = lambda w: pl.BlockSpec((tm, w), lambda i: (i, 0))
    table = pl.BlockSpec((tm, LANES), lambda i: (i % nt, 0))
    col = lambda w: pl.BlockSpec((1, w, tm), lambda i: (i // nt, 0, i % nt))
    col_shape = lambda w, dt: jax.ShapeDtypeStruct((bsz, w, t), dt)
    body = functools.partial(
        _proj_cols_body, d_lru=d_lru, d_att=d_att, d_qi=d_qi, idx_dim=idx_dim, rope_half=half,
        q_scale=head_dim ** -0.5, cha=cha, n_heads=n_heads, idx_heads=idx_heads)
    return pl.pallas_call(
        body,
        grid=(n // tm,),
        in_specs=[row(d_model), pl.BlockSpec(w_pad.shape, lambda i: (0, 0)), table, table, table],
        out_specs=[
            row(d_lru), row(d_lru), col(d_att), col(d_qi), col(d_att),
            pl.BlockSpec((1, n_heads, tm, head_dim), lambda i: (i // nt, 0, i % nt, 0)),
            col(d_att),
            pl.BlockSpec((1, tm // cha, d_att, cha), lambda i: (i // nt, i % nt, 0, 0)),
            col(idx_dim), pl.BlockSpec((1, tm, idx_dim), lambda i: (i // nt, i % nt, 0)),
            col(idx_heads)],
        out_shape=[
            jax.ShapeDtypeStruct((n, d_lru), F32), jax.ShapeDtypeStruct((n, d_lru), F32),
            col_shape(d_att, BF16), col_shape(d_qi, BF16), col_shape(d_att, F32),
            jax.ShapeDtypeStruct((bsz, n_heads, t, head_dim), BF16),
            col_shape(d_att, F32),
            jax.ShapeDtypeStruct((bsz, t // cha, d_att, cha), BF16),
            col_shape(idx_dim, F32), jax.ShapeDtypeStruct((bsz, t, idx_dim), BF16),
            col_shape(idx_heads, F32)],
        compiler_params=_params("parallel"),
        name="proj_prompt",
    )(x3d.reshape(n, d_model), w_pad, c, s1, s2)


def _softplus(x):
    return jnp.maximum(x, 0.0) + jnp.log1p(jnp.exp(-jnp.abs(x)))


def _gelu_tanh(x):
    return 0.5 * x * (1.0 + jnp.tanh(np.sqrt(2.0 / np.pi).astype(np.float32)
                                     * (x + 0.044715 * (x * x * x))))


def _lru_gates(xc, wa_ref, wx_ref, b_a, b_x, lam):
    xcb = xc.astype(BF16)
    r = jax.nn.sigmoid(jnp.dot(xcb, wa_ref[...], preferred_element_type=F32) + b_a)
    i = jax.nn.sigmoid(jnp.dot(xcb, wx_ref[...], preferred_element_type=F32) + b_x)
    log_a = -LRU_C * r * _softplus(-lam)
    a = jnp.exp(log_a)
    t = jnp.tanh(log_a)
    b = jnp.sqrt(-2.0 * t / (1.0 - t)) * (i * xc)
    return a, b


def _rms_gain(y, g):
    return y * lax.rsqrt(jnp.mean(y * y, axis=-1, keepdims=True) + RMS_EPS) * g


def _lru_prompt_body(xl_ref, gate_ref, cprev_ref, h0_ref, cw_ref, p_ref, wa_ref, wx_ref,
                     mix_ref, hlast_ref, ext_ref, hc_ref, *, tt):
    j = pl.program_id(1)

    @pl.when(j == 0)
    def _():
        ext_ref[0:SUBLANES, :] = cprev_ref[0]
        hc_ref[0:1, :] = h0_ref[0]

    xl = xl_ref[0]
    ext_ref[SUBLANES:SUBLANES + tt, :] = xl
    conv_b, b_a, b_x, lam, gn = (p_ref[r:r + 1, :] for r in range(5))
    xc = conv_b + (cw_ref[0:1, :] * ext_ref[SUBLANES - 3:SUBLANES - 3 + tt, :]
                   + cw_ref[1:2, :] * ext_ref[SUBLANES - 2:SUBLANES - 2 + tt, :]
                   + cw_ref[2:3, :] * ext_ref[SUBLANES - 1:SUBLANES - 1 + tt, :]
                   + cw_ref[3:4, :] * xl)
    ext_ref[0:SUBLANES, :] = ext_ref[tt:tt + SUBLANES, :]

    a, b = _lru_gates(xc, wa_ref, wx_ref, b_a, b_x, lam)
    row = lax.broadcasted_iota(I32, a.shape, 0)
    d = 1
    while d < tt:
        keep = row >= d
        a_prev = jnp.where(keep, pltpu.roll(a, d, 0), 1.0)
        b_prev = jnp.where(keep, pltpu.roll(b, d, 0), 0.0)
        b = a * b_prev + b
        a = a * a_prev
        d *= 2
    h = a * hc_ref[0:1, :] + b
    hc_ref[0:1, :] = h[tt - 1:tt, :]
    hlast_ref[0] = h[tt - 1:tt, :]
    mix_ref[0] = _rms_gain(h * _gelu_tanh(gate_ref[0]), gn).astype(BF16)


def _lru_prompt(xl, gate, conv_prev, h0, conv_w, pvec, wa_bd, wx_bd, *, tt):
    bsz, t, d = xl.shape
    cprev8 = jnp.concatenate(
        [jnp.zeros((bsz, SUBLANES - (CONV_W - 1), d), F32), conv_prev.astype(F32)], axis=1)
    const = lambda shape: pl.BlockSpec(shape, lambda b, j: (0,) * len(shape))
    return pl.pallas_call(
        functools.partial(_lru_prompt_body, tt=tt),
        grid=(bsz, t // tt),
        in_specs=[pl.BlockSpec((1, tt, d), lambda b, j: (b, j, 0)),
                  pl.BlockSpec((1, tt, d), lambda b, j: (b, j, 0)),
                  pl.BlockSpec((1, SUBLANES, d), lambda b, j: (b, 0, 0)),
                  pl.BlockSpec((1, 1, d), lambda b, j: (b, 0, 0)),
                  const(conv_w.shape), const(pvec.shape), const(wa_bd.shape), const(wx_bd.shape)],
        out_specs=[pl.BlockSpec((1, tt, d), lambda b, j: (b, j, 0)),
                   pl.BlockSpec((1, 1, d), lambda b, j: (b, 0, 0))],
        out_shape=[jax.ShapeDtypeStruct((bsz, t, d), BF16),
                   jax.ShapeDtypeStruct((bsz, 1, d), F32)],
        scratch_shapes=[pltpu.VMEM((tt + SUBLANES, d), F32), pltpu.VMEM((SUBLANES, d), F32)],
        compiler_params=_params("parallel", "arbitrary"),
        name="lru_prompt",
    )(xl, gate, cprev8, h0.astype(F32)[:, None, :], conv_w, pvec, wa_bd, wx_bd)


def _lru_sample_body(xl_ref, gate_ref, cprev_ref, h0_ref, cw_ref, p_ref, wa_ref, wx_ref,
                     mix_ref, hlast_ref, *, t_len):
    conv_b, b_a, b_x, lam, gn = (p_ref[r:r + 1, :] for r in range(5))
    xp = [cprev_ref[s] for s in range(CONV_W - 1)] + [xl_ref[s] for s in range(t_len)]
    h = h0_ref[...]
    for s in range(t_len):
        xc = conv_b + (cw_ref[0:1, :] * xp[s] + cw_ref[1:2, :] * xp[s + 1]
                       + cw_ref[2:3, :] * xp[s + 2] + cw_ref[3:4, :] * xp[s + 3])
        a, b = _lru_gates(xc, wa_ref, wx_ref, b_a, b_x, lam)
        h = a * h + b
        mix_ref[s] = _rms_gain(h * _gelu_tanh(gate_ref[s]), gn).astype(BF16)
    hlast_ref[...] = h


def _lru_sample(xl_t, gate_t, cprev_t, h0, conv_w, pvec, wa_bd, wx_bd):
    t_len, dbs, d = xl_t.shape
    return pl.pallas_call(
        functools.partial(_lru_sample_body, t_len=t_len),
        out_shape=[jax.ShapeDtypeStruct((t_len, dbs, d), BF16),
                   jax.ShapeDtypeStruct((dbs, d), F32)],
        compiler_params=pltpu.CompilerParams(vmem_limit_bytes=VMEM_LIMIT),
        name="lru_sample",
    )(xl_t, gate_t, cprev_t, h0.astype(F32), conv_w, pvec, wa_bd, wx_bd)


KEY_NEG_INF = INT_MIN + 0x7FFFFF
REFINE_STEPS = 8


def _key_to_float(key):
    return pltpu.bitcast(jnp.where(key >= 0, key, key ^ 0x7FFFFFFF), F32)


def _threshold_search(count, cell_span, total, topk, theta_ref, need_ref):
    shape = theta_ref.shape

    def value_step(it, carry):
        base, n_base = carry
        trial = base ^ lax.shift_left(jnp.int32(1), jnp.int32(31) - it)
        trial_f = _key_to_float(trial)
        n = count(lambda s: s >= trial_f)
        ok = n >= topk
        return jnp.where(ok, trial, base), jnp.where(ok, n, n_base)

    theta_key, n_ge = lax.fori_loop(
        0, 32, value_step, (jnp.full(shape, INT_MIN, I32), jnp.zeros(shape, I32) + total))
    theta = jnp.where(theta_key < KEY_NEG_INF, -jnp.inf, _key_to_float(theta_key))
    theta_ref[...] = theta
    has_surplus = jnp.max(n_ge) > topk

    @pl.when(has_surplus)
    def _():
        above = _key_to_float(jnp.maximum(theta_key, KEY_NEG_INF) + 1)
        cell_min, cell_max = cell_span(theta, above)

        @pl.when(jnp.max(jnp.where(cell_max > cell_min, 1, 0)) > 0)
        def _():
            def refine_step(_, carry):
                lo, hi = carry
                mid = lo + 0.5 * (hi - lo)
                ok = count(lambda s: s >= mid) >= topk
                return jnp.where(ok, mid, lo), jnp.where(ok, hi, mid)

            theta_ref[...] = lax.fori_loop(0, REFINE_STEPS, refine_step, (theta, above))[0]

        theta_fine = theta_ref[...]
        need_ref[...] = (topk - count(lambda s: s > theta_fine)).astype(F32)

    return has_surplus


def _strict_triangle(n, lower):
    r = lax.broadcasted_iota(I32, (n, n), 0)
    c = lax.broadcasted_iota(I32, (n, n), 1)
    return jnp.where((c < r) if lower else (r < c), 1.0, 0.0).astype(BF16)


def _attn_prompt_body(qit_ref, wt_ref, qt_ref, kidx_ref, k_ref, vt_ref, gn_ref, out_ref,
                      sc_ref, y_ref, theta_ref, need_ref, m_ref, l_ref, s_ref, mx_ref, *, tq, ch,
                      cha, topk, n_heads, head_dim, idx_heads, idx_dim):
    i = pl.program_id(1)
    n_keys = (i + 1) * tq
    qpos = i * tq + lax.broadcasted_iota(I32, (1, tq), 1)
    sub = LANES
    kpos_sub = lax.broadcasted_iota(I32, (sub, tq), 0)
    kpos_ch = lax.broadcasted_iota(I32, (ch, tq), 0)
    nc = (i + 1) * (tq // ch)

    def score_chunk(c, carry):
        for j in range(ch // sub):
            start = pl.multiple_of(c * ch + j * sub, sub)
            rows = pl.ds(start, sub)
            kc = kidx_ref[0, rows, :]
            acc = jnp.zeros((sub, tq), F32)
            for h in range(idx_heads):
                d = jnp.dot(kc, qit_ref[0, h * idx_dim:(h + 1) * idx_dim, :],
                            preferred_element_type=F32)
                acc = acc + jnp.maximum(d, 0.0) * wt_ref[0, h:h + 1, :]
            sc_ref[rows, :] = jnp.where(start + kpos_sub <= qpos, acc, -jnp.inf)
        return carry

    lax.fori_loop(0, nc, score_chunk, 0)

    chunk_rows = lambda c: pl.ds(pl.multiple_of(c * ch, ch), ch)
    fold = lambda x, op: op(x.reshape(ch // SUBLANES, SUBLANES, tq), axis=0)

    def count(pred):
        def body(c, acc):
            hit = jnp.where(pred(sc_ref[chunk_rows(c), :]), 1, 0).astype(I32)
            return acc + fold(hit, jnp.sum)
        acc = lax.fori_loop(0, nc, body, jnp.zeros((SUBLANES, tq), I32))
        return jnp.sum(acc, axis=0, keepdims=True)

    def cell_span(lo, hi):
        def body(c, carry):
            s = sc_ref[chunk_rows(c), :]
            inside = (s >= lo) & (s < hi)
            return (jnp.minimum(carry[0], fold(jnp.where(inside, s, jnp.inf), jnp.min)),
                    jnp.maximum(carry[1], fold(jnp.where(inside, s, -jnp.inf), jnp.max)))
        init = (jnp.full((SUBLANES, tq), jnp.inf, F32), jnp.full((SUBLANES, tq), -jnp.inf, F32))
        lo_acc, hi_acc = lax.fori_loop(0, nc, body, init)
        return (jnp.min(lo_acc, axis=0, keepdims=True), jnp.max(hi_acc, axis=0, keepdims=True))

    has_surplus = _threshold_search(count, cell_span, n_keys, topk, theta_ref, need_ref)
    theta = theta_ref[...]

    @pl.when(jnp.logical_not(has_surplus))
    def _():
        def bias_chunk(c, carry):
            rows = chunk_rows(c)
            sel = (sc_ref[rows, :] >= theta) & (c * ch + kpos_ch <= qpos)
            sc_ref[rows, :] = jnp.where(sel, 0.0, MASKED)
            return carry
        lax.fori_loop(0, nc, bias_chunk, 0)

    @pl.when(has_surplus)
    def _():
        need = need_ref[...]
        lower = _strict_triangle(ch, lower=True)

        def bias_chunk(c, seen):
            rows = chunk_rows(c)
            s = sc_ref[rows, :]
            tie = jnp.where(s == theta, 1.0, 0.0)
            rank = seen + jnp.dot(lower, tie.astype(BF16), preferred_element_type=F32)
            sel = ((s > theta) | ((s == theta) & (rank < need))) & (c * ch + kpos_ch <= qpos)
            sc_ref[rows, :] = jnp.where(sel, 0.0, MASKED)
            return seen + jnp.sum(fold(tie, jnp.sum), axis=0, keepdims=True)
        lax.fori_loop(0, nc, bias_chunk, jnp.zeros((1, tq), F32))

    heads = [(h, slice(h * head_dim, (h + 1) * head_dim)) for h in range(n_heads)]
    m_ref[...] = jnp.full(m_ref.shape, -jnp.inf, F32)
    l_ref[...] = jnp.zeros(l_ref.shape, F32)
    y_ref[...] = jnp.zeros(y_ref.shape, F32)

    nca = (i + 1) * (tq // cha)

    def logits(c, slot):
        rows = pl.ds(pl.multiple_of(c * cha, cha), cha)
        bias = sc_ref[rows, :]
        for h, hs in heads:
            s = jnp.dot(k_ref[0, h, rows, :], qt_ref[0, hs, :],
                        preferred_element_type=F32) + bias
            s_ref[slot, h] = s
            mx_ref[slot, h:h + 1, :] = jnp.max(s, axis=0, keepdims=True)

    def update(c, slot):
        for h, hs in heads:
            m = m_ref[h:h + 1, :]
            m_new = jnp.maximum(m, mx_ref[slot, h:h + 1, :])
            alpha = jnp.exp(m - m_new)
            p = jnp.exp(s_ref[slot, h] - m_new)
            m_ref[h:h + 1, :] = m_new
            l_ref[h:h + 1, :] = alpha * l_ref[h:h + 1, :] + jnp.sum(p, axis=0, keepdims=True)
            y_ref[hs, :] = alpha * y_ref[hs, :] + jnp.dot(
                vt_ref[0, c, hs, :], p.astype(BF16), preferred_element_type=F32)

    logits(0, 0)

    def chunk_pair(j, carry):
        c0 = 2 * j
        logits(c0 + 1, 1)
        update(c0, 0)
        logits(jnp.minimum(c0 + 2, nca - 1), 0)
        update(c0 + 1, 1)
        return carry

    lax.fori_loop(0, nca // 2, chunk_pair, 0)
    for h, hs in heads:
        y_ref[hs, :] = y_ref[hs, :] / l_ref[h:h + 1, :]
    out_ref[0] = _rms_gain(y_ref[...].T, gn_ref[...]).astype(BF16)


def _attn_prompt(qit, wt, qt, kidxb, k_hm, vt, gn_att, *, tq, topk, idx_dim):
    bsz, d_att, t = qt.shape
    n_heads, head_dim = k_hm.shape[1], k_hm.shape[3]
    ch = tq
    cha = vt.shape[3]
    assert (tq // cha) % 2 == 0
    cols = lambda a: pl.BlockSpec((1, a.shape[1], tq), lambda b, i: (b, 0, i))
    full = lambda a: pl.BlockSpec((1,) + a.shape[1:], lambda b, i: (b,) + (0,) * (a.ndim - 1))
    body = functools.partial(
        _attn_prompt_body, tq=tq, ch=ch, cha=cha, topk=topk, n_heads=n_heads,
        head_dim=head_dim, idx_heads=wt.shape[1], idx_dim=idx_dim)
    return pl.pallas_call(
        body,
        grid=(bsz, t // tq),
        in_specs=[cols(qit), cols(wt), cols(qt), full(kidxb), full(k_hm), full(vt),
                  pl.BlockSpec((1, d_att), lambda b, i: (0, 0))],
        out_specs=pl.BlockSpec((1, tq, d_att), lambda b, i: (b, i, 0)),
        out_shape=jax.ShapeDtypeStruct((bsz, t, d_att), BF16),
        scratch_shapes=[pltpu.VMEM((t, tq), F32), pltpu.VMEM((d_att, tq), F32),
                        pltpu.VMEM((1, tq), F32), pltpu.VMEM((1, tq), F32),
                        pltpu.VMEM((n_heads, tq), F32), pltpu.VMEM((n_heads, tq), F32),
                        pltpu.VMEM((2, n_heads, cha, tq), F32),
                        pltpu.VMEM((2, n_heads, tq), F32)],
        compiler_params=_params("parallel", "arbitrary"),
        name="attn_prompt",
    )(qit, wt, qt, kidxb, k_hm, vt, gn_att)


def _score_sample_body(pt_ref, qi_ref, w_ref, *refs, n_pages, page, t_len, idx_heads, past_len):
    del pt_ref
    page_refs, new_ref, out_ref = refs[:n_pages], refs[n_pages], refs[n_pages + 1]
    qi = qi_ref[0]
    w = w_ref[0]
    kt = jnp.concatenate([r[0].astype(BF16) for r in page_refs] + [new_ref[0]], axis=1)
    d = jnp.maximum(jnp.dot(qi, kt, preferred_element_type=F32), 0.0) * w
    s = jnp.sum(d.reshape(t_len, idx_heads, past_len + page), axis=1)
    tpos = past_len + lax.broadcasted_iota(I32, s.shape, 0)
    kpos = lax.broadcasted_iota(I32, s.shape, 1)
    out_ref[0] = jnp.where(kpos <= tpos, s, -jnp.inf)


def _select_sample_body(s_ref, bias_ref, sc_ref, theta_ref, need_ref, *, topk, page):
    nc, rows, _ = sc_ref.shape
    for c in range(nc):
        sc_ref[c] = s_ref[:, c * page:(c + 1) * page]

    def count(pred):
        def body(c, acc):
            return acc + jnp.where(pred(sc_ref[c]), 1, 0).astype(I32)
        acc = lax.fori_loop(0, nc, body, jnp.zeros((rows, page), I32))
        return jnp.sum(acc, axis=1, keepdims=True)

    def cell_span(lo, hi):
        def body(c, carry):
            s = sc_ref[c]
            inside = (s >= lo) & (s < hi)
            return (jnp.minimum(carry[0], jnp.where(inside, s, jnp.inf)),
                    jnp.maximum(carry[1], jnp.where(inside, s, -jnp.inf)))
        init = (jnp.full((rows, page), jnp.inf, F32), jnp.full((rows, page), -jnp.inf, F32))
        lo_acc, hi_acc = lax.fori_loop(0, nc, body, init)
        return (jnp.min(lo_acc, axis=1, keepdims=True), jnp.max(hi_acc, axis=1, keepdims=True))

    has_surplus = _threshold_search(count, cell_span, nc * page, topk, theta_ref, need_ref)
    theta = theta_ref[...]

    @pl.when(jnp.logical_not(has_surplus))
    def _():
        for c in range(nc):
            s = sc_ref[c]
            sel = (s >= theta) & (s > -jnp.inf)
            bias_ref[:, c * page:(c + 1) * page] = jnp.where(sel, 0.0, MASKED)

    @pl.when(has_surplus)
    def _():
        need = need_ref[...]
        upper = _strict_triangle(page, lower=False)
        seen = jnp.zeros((rows, 1), F32)
        for c in range(nc):
            s = sc_ref[c]
            tie = jnp.where(s == theta, 1.0, 0.0)
            rank = seen + jnp.dot(tie.astype(BF16), upper, preferred_element_type=F32)
            sel = ((s > theta) | ((s == theta) & (rank < need))) & (s > -jnp.inf)
            bias_ref[:, c * page:(c + 1) * page] = jnp.where(sel, 0.0, MASKED)
            seen = seen + jnp.sum(tie, axis=1, keepdims=True)


def _attn_sample_body(pt_ref, q_ref, bias_ref, gn_ref, *refs, n_pages, page, t_len, n_heads,
                      head_dim):
    del pt_ref
    k_refs, v_refs = refs[:n_pages], refs[n_pages:2 * n_pages]
    knew_ref, vnew_ref, out_ref = refs[2 * n_pages:]
    past = n_pages * page
    bias = bias_ref[0]
    outs = []
    del past
    logits = []
    for h in range(n_heads):
        kt = jnp.concatenate([r[0, h].astype(BF16) for r in k_refs] + [knew_ref[0, h]], axis=1)
        logits.append(jnp.dot(q_ref[0, h], kt, preferred_element_type=F32) + bias)
    for h in range(n_heads):
        s = logits[h]
        m = jnp.max(s, axis=1, keepdims=True)
        p = jnp.exp(s - m)
        l = jnp.sum(p, axis=1, keepdims=True)
        vt = jnp.concatenate([r[0, h].astype(BF16) for r in v_refs] + [vnew_ref[0, h]], axis=1)
        outs.append(_nt_dot(p.astype(BF16), vt) / l)
    y = jnp.concatenate(outs, axis=1)
    out_ref[0] = _rms_gain(y, gn_ref[...]).astype(BF16)


def _attn_sample(qib, wi, qb, kidx_new, k_new, v_new, cache_k, cache_v, cache_kidx, page_table,
                 gn_att, *, topk, n_heads):
    dbs, t_len, d_att = qb.shape
    n_pages = page_table.shape[1]
    page = cache_k.shape[1]
    idx_dim = cache_kidx.shape[2]
    idx_heads = wi.shape[2]
    head_dim = d_att // n_heads
    nkp = (n_pages + 1) * page
    past_len = n_pages * page
    pad = lambda a: jnp.pad(a, ((0, 0), (0, page - t_len), (0, 0)))
    kidx_t = jnp.transpose(cache_kidx, (0, 2, 1))
    k_t = jnp.transpose(cache_k, (0, 2, 3, 1))
    v_t = jnp.transpose(cache_v, (0, 2, 3, 1))
    new_t = lambda a: jnp.transpose(pad(a).reshape(dbs, page, n_heads, head_dim), (0, 2, 3, 1))

    def paged(shape):
        return [pl.BlockSpec((1,) + shape, functools.partial(
            lambda b, pt, p: (pt[b, p],) + (0,) * len(shape), p=p)) for p in range(n_pages)]

    per_seq = lambda shape: pl.BlockSpec((1,) + shape, lambda b, pt: (b,) + (0,) * len(shape))

    scores = pl.pallas_call(
        functools.partial(_score_sample_body, n_pages=n_pages, page=page, t_len=t_len,
                          idx_heads=idx_heads, past_len=past_len),
        grid_spec=pltpu.PrefetchScalarGridSpec(
            num_scalar_prefetch=1, grid=(dbs,),
            in_specs=[per_seq((t_len * idx_heads, idx_dim)), per_seq((t_len * idx_heads, 1))]
                     + paged((idx_dim, page)) + [per_seq((idx_dim, page))],
            out_specs=per_seq((t_len, nkp))),
        out_shape=jax.ShapeDtypeStruct((dbs, t_len, nkp), F32),
        compiler_params=_params("parallel"),
        name="score_sample",
    )(page_table, qib.reshape(dbs, t_len * idx_heads, idx_dim),
      wi.reshape(dbs, t_len * idx_heads, 1), *([kidx_t] * n_pages),
      jnp.transpose(pad(kidx_new), (0, 2, 1)))

    rows = dbs * t_len
    tr = min(rows, 128)
    bias = pl.pallas_call(
        functools.partial(_select_sample_body, topk=topk, page=page),
        grid=(rows // tr,),
        in_specs=[pl.BlockSpec((tr, nkp), lambda r: (r, 0))],
        out_specs=pl.BlockSpec((tr, nkp), lambda r: (r, 0)),
        out_shape=jax.ShapeDtypeStruct((rows, nkp), F32),
        scratch_shapes=[pltpu.VMEM((n_pages + 1, tr, page), F32), pltpu.VMEM((tr, 1), F32),
                        pltpu.VMEM((tr, 1), F32)],
        compiler_params=_params("parallel"),
        name="select_sample",
    )(scores.reshape(rows, nkp)).reshape(dbs, t_len, nkp)

    kv_pages = paged((n_heads, head_dim, page))
    return pl.pallas_call(
        functools.partial(_attn_sample_body, n_pages=n_pages, page=page, t_len=t_len,
                          n_heads=n_heads, head_dim=head_dim),
        grid_spec=pltpu.PrefetchScalarGridSpec(
            num_scalar_prefetch=1, grid=(dbs,),
            in_specs=[per_seq((n_heads, t_len, head_dim)), per_seq((t_len, nkp)),
                      pl.BlockSpec((1, d_att), lambda b, pt: (0, 0))]
                     + kv_pages + kv_pages
                     + [per_seq((n_heads, head_dim, page)), per_seq((n_heads, head_dim, page))],
            out_specs=per_seq((t_len, d_att))),
        out_shape=jax.ShapeDtypeStruct((dbs, t_len, d_att), BF16),
        compiler_params=_params("parallel"),
        name="attn_sample",
    )(page_table, qb.reshape(dbs, t_len, n_heads, head_dim).transpose(0, 2, 1, 3), bias, gn_att,
      *([k_t] * n_pages), *([v_t] * n_pages), new_t(k_new), new_t(v_new))


def _layer_norm(x, g, b):
    mu = jnp.mean(x, axis=-1, keepdims=True)
    xc = x - mu
    var = jnp.mean(xc * xc, axis=-1, keepdims=True)
    return xc * lax.rsqrt(var + LN_EPS) * g + b


def _finish_body(x_ref, ml_ref, ma_ref, wo_ref, wfi_ref, wfo_ref, p_ref, out_ref,
                 *, alpha, d_lru, d_ff, fc):
    ln1_g, ln1_b, ln2_g, ln2_b = (p_ref[r:r + 1, :] for r in range(4))
    y = (jnp.dot(ml_ref[...], wo_ref[0:d_lru, :], preferred_element_type=F32)
         + jnp.dot(ma_ref[...], wo_ref[d_lru:, :], preferred_element_type=F32))
    x1 = _layer_norm(alpha * x_ref[...] + y, ln1_g, ln1_b)
    x1b = x1.astype(BF16)
    f = jnp.zeros(x1.shape, F32)
    for c in range(d_ff // fc):
        u = jnp.dot(x1b, wfi_ref[:, c * fc:(c + 1) * fc], preferred_element_type=F32)
        g = jnp.dot(x1b, wfi_ref[:, d_ff + c * fc:d_ff + (c + 1) * fc],
                    preferred_element_type=F32)
        hidden = (g * jax.nn.sigmoid(g) * u).astype(BF16)
        f = f + jnp.dot(hidden, wfo_ref[c * fc:(c + 1) * fc, :], preferred_element_type=F32)
    out_ref[...] = _layer_norm(alpha * x1 + f, ln2_g, ln2_b)


def _finish(x2d, mix_lru, mix_att, wo, wfi, wfo, pvec, *, alpha, tm):
    n, d_model = x2d.shape
    d_lru = mix_lru.shape[1]
    d_ff = wfo.shape[0]
    fc = 2 * LANES if d_ff % (2 * LANES) == 0 else LANES
    row = lambda w: pl.BlockSpec((tm, w), lambda i: (i, 0))
    const = lambda a: pl.BlockSpec(a.shape, lambda i: (0, 0), pipeline_mode=pl.Buffered(1))
    return pl.pallas_call(
        functools.partial(_finish_body, alpha=alpha, d_lru=d_lru, d_ff=d_ff, fc=fc),
        grid=(n // tm,),
        in_specs=[row(d_model), row(d_lru), row(mix_att.shape[1]),
                  const(wo), const(wfi), const(wfo), const(pvec)],
        out_specs=row(d_model),
        out_shape=jax.ShapeDtypeStruct((n, d_model), F32),
        compiler_params=_params("parallel"),
        name="finish",
    )(x2d, mix_lru, mix_att, wo, wfi, wfo, pvec)


def _block_diag(w):
    nb, bi, bo = w.shape
    eye = jnp.eye(nb, dtype=w.dtype)
    return (w[:, :, None, :] * eye[:, None, :, None]).reshape(nb * bi, nb * bo)


def _row_tile(n, want):
    tm = min(n, want)
    while n % tm:
        tm //= 2
    return tm


def kernel(x_prompt, x_sample, cache_k, cache_v, cache_kidx, state_conv, state_h, page_table,
           w_in, conv_w, conv_b, w_a, b_a, w_x, b_x, lam, gn_lru, gn_att, w_out,
           ln1_g, ln1_b, w_ffn_in, w_ffn_out, ln2_g, ln2_b):
    depth, d_model, d_in = w_in.shape
    bsz, seq, _ = x_prompt.shape
    dbs, dseq, _ = x_sample.shape
    d_lru = conv_w.shape[2]
    n_phys, page, n_heads, head_dim = cache_k.shape[1:]
    d_att = n_heads * head_dim
    idx_dim = cache_kidx.shape[3]
    d_qi = IDX_HEADS * idx_dim
    n_pages = page_table.shape[1]
    past_len = n_pages * page
    alpha = (2.0 * depth) ** 0.25
    assert d_in == 2 * d_lru + 3 * d_att + d_qi + idx_dim + IDX_HEADS
    assert idx_dim + IDX_HEADS <= LANES and LANES % head_dim == 0 and head_dim == idx_dim
    geom = dict(d_lru=d_lru, d_att=d_att, d_qi=d_qi, idx_dim=idx_dim, head_dim=head_dim)
    d_main = d_in - idx_dim - IDX_HEADS

    pos_p = jnp.tile(jnp.arange(seq), bsz)
    pos_s = jnp.tile(past_len + jnp.arange(dseq), dbs)
    topk_p = min(TOPK_MAX, seq // 4)
    topk_s = min(TOPK_MAX, (past_len + dseq) // 4)

    xp = x_prompt.reshape(bsz * seq, d_model)
    xs = x_sample.reshape(dbs * dseq, d_model)
    outs_p, outs_s = [], []
    for l in range(depth):
        w_pad = jnp.pad(w_in[l], ((0, 0), (0, d_main + LANES - d_in))).astype(BF16)
        wa_bd = _block_diag(w_a[l]).astype(BF16)
        wx_bd = _block_diag(w_x[l]).astype(BF16)
        lru_vec = jnp.stack([conv_b[l], b_a[l], b_x[l], lam[l], gn_lru[l]]
                            + [jnp.zeros_like(lam[l])] * 3)
        fin_vec = jnp.stack([ln1_g[l], ln1_b[l], ln2_g[l], ln2_b[l]])
        wo, wfi, wfo = (w_out[l].astype(BF16), w_ffn_in[l].astype(BF16),
                        w_ffn_out[l].astype(BF16))
        gn_a = gn_att[l][None, :]

        tq = _row_tile(seq, 256)
        xl, gate, qt, qit, kt, k_hm, vt, vt_chunks, kidx_t, kidx_b, wt = _project_cols(
            xp.reshape(bsz, seq, d_model), w_pad, tm=_row_tile(seq, 512), cha=tq // 2,
            n_heads=n_heads, idx_heads=IDX_HEADS, **geom)
        xl3 = xl.reshape(bsz, seq, d_lru)
        mix_l, h_last = _lru_prompt(
            xl3, gate.reshape(bsz, seq, d_lru), jnp.zeros((bsz, CONV_W - 1, d_lru), F32),
            jnp.zeros((bsz, d_lru), F32), conv_w[l], lru_vec, wa_bd, wx_bd, tt=_row_tile(seq, 256))
        mix_a = _attn_prompt(qit, wt, qt, kidx_b, k_hm, vt_chunks, gn_a, tq=tq, topk=topk_p,
                             idx_dim=idx_dim)
        xp = _finish(xp, mix_l.reshape(bsz * seq, d_lru), mix_a.reshape(bsz * seq, d_att),
                     wo, wfi, wfo, fin_vec, alpha=alpha, tm=_row_tile(bsz * seq, 512))
        token_major = lambda a: a.reshape(bsz, n_heads, head_dim, seq).transpose(0, 3, 1, 2)
        outs_p.append((token_major(kt), token_major(vt), jnp.swapaxes(kidx_t, 1, 2),
                       xl3[:, seq - (CONV_W - 1):], h_last[:, 0]))

        xl, gate, qb, k, kb, v, vb, qib, tail = _project_rows(
            xs, w_pad, pos_s, tm=_row_tile(dbs * dseq, 256), **geom)
        d3 = lambda a: a.reshape(dbs, dseq, a.shape[-1])
        tm_major = lambda a: jnp.swapaxes(d3(a), 0, 1)
        xl3 = d3(xl)
        mix_l, h_last = _lru_sample(
            tm_major(xl), tm_major(gate), jnp.swapaxes(state_conv[l], 0, 1).astype(F32),
            state_h[l], conv_w[l], lru_vec, wa_bd, wx_bd)
        tail3 = d3(tail)
        ki = tail3[:, :, :idx_dim]
        wi = tail3[:, :, idx_dim:idx_dim + IDX_HEADS]
        mix_a = _attn_sample(
            d3(qib), wi, d3(qb), ki.astype(BF16), d3(kb), d3(vb),
            cache_k[l], cache_v[l], cache_kidx[l], page_table, gn_a, topk=topk_s,
            n_heads=n_heads)
        xs = _finish(xs, jnp.swapaxes(mix_l, 0, 1).reshape(dbs * dseq, d_lru),
                     mix_a.reshape(dbs * dseq, d_att), wo, wfi, wfo, fin_vec, alpha=alpha,
                     tm=_row_tile(dbs * dseq, 256))
        conv_new = jnp.concatenate([state_conv[l].astype(F32), xl3], axis=1)[:, -(CONV_W - 1):]
        outs_s.append((k.reshape(dbs, dseq, n_heads, head_dim), v.reshape(dbs, dseq, n_heads, head_dim),
                       ki, conv_new, h_last))

    stack = lambda outs, j: jnp.stack([o[j] for o in outs])
    return (xp.reshape(bsz, seq, d_model), xs.reshape(dbs, dseq, d_model),
            *(stack(outs_p, j) for j in range(5)), *(stack(outs_s, j) for j in range(5)))
```

```python
import functools

import jax
import jax.numpy as jnp
import numpy as np
from jax import lax
from jax.experimental import pallas as pl
from jax.experimental.pallas import tpu as pltpu

CONV_W = 4
LRU_C = 8.0
LRU_BLOCKS = 8
IDX_HEADS = 8
TOPK_MAX = 256
ROPE_FRACTION = 4
ROPE_THETA = 500000.0
RMS_EPS = 1e-6
LN_EPS = 1e-5

LANES = 128
SUBLANES = 8
VMEM_LIMIT = 56 * 1024 * 1024
MASKED = -1e30
INT_MIN = -2 ** 31

F32 = jnp.float32
BF16 = jnp.bfloat16
I32 = jnp.int32


def _params(*sem):
    return pltpu.CompilerParams(dimension_semantics=sem, vmem_limit_bytes=VMEM_LIMIT)


def _nt_dot(a, b):
    return lax.dot_general(a, b, (((1,), (1,)), ((), ())), preferred_element_type=F32)


def _proj_streams(x_ref, w_ref, c_ref, s1_ref, s2_ref, *, d_lru, d_att, d_qi, idx_dim, rope_half,
                  q_scale):
    xb = x_ref[...].astype(BF16)
    c, s1, s2 = c_ref[...], s1_ref[...], s2_ref[...]

    def proj(lo, width):
        return jnp.dot(xb, w_ref[:, lo:lo + width], preferred_element_type=F32)

    def tiled(t, width):
        reps = width // LANES
        return t if reps == 1 else jnp.concatenate([t] * reps, axis=1)

    def rope(z, cc, ss1, ss2):
        width = z.shape[1]
        return (z * cc + pltpu.roll(z, width - rope_half, 1) * ss1
                + pltpu.roll(z, rope_half, 1) * ss2)

    lo = 0
    xl = proj(lo, d_lru); lo += d_lru
    gate = proj(lo, d_lru); lo += d_lru
    q = rope(proj(lo, d_att), tiled(c, d_att), tiled(s1, d_att), tiled(s2, d_att)); lo += d_att
    k = rope(proj(lo, d_att), tiled(c, d_att), tiled(s1, d_att), tiled(s2, d_att)); lo += d_att
    v = proj(lo, d_att); lo += d_att
    qi = rope(proj(lo, d_qi), tiled(c, d_qi), tiled(s1, d_qi), tiled(s2, d_qi)); lo += d_qi
    tail = proj(lo, LANES)
    is_key = lax.broadcasted_iota(I32, tail.shape, 1) < idx_dim
    tail = rope(tail, jnp.where(is_key, c, 1.0), jnp.where(is_key, s1, 0.0),
                jnp.where(is_key, s2, 0.0))
    return xl, gate, q * q_scale, k, v, qi, tail


def _proj_rows_body(x_ref, w_ref, c_ref, s1_ref, s2_ref,
                    xl_ref, gate_ref, qb_ref, k_ref, kb_ref, v_ref, vb_ref, qib_ref, tail_ref,
                    **geom):
    xl, gate, q, k, v, qi, tail = _proj_streams(x_ref, w_ref, c_ref, s1_ref, s2_ref, **geom)
    xl_ref[...] = xl
    gate_ref[...] = gate
    qb_ref[...] = q.astype(BF16)
    k_ref[...] = k
    kb_ref[...] = k.astype(BF16)
    v_ref[...] = v
    vb_ref[...] = v.astype(BF16)
    qib_ref[...] = qi.astype(BF16)
    tail_ref[...] = tail


def _proj_cols_body(x_ref, w_ref, c_ref, s1_ref, s2_ref,
                    xl_ref, gate_ref, qt_ref, qit_ref, kt_ref, khm_ref, vt_ref, vtc_ref,
                    kidxt_ref, kidx_ref, wt_ref, *, cha, n_heads, idx_heads, **geom):
    xl, gate, q, k, v, qi, tail = _proj_streams(x_ref, w_ref, c_ref, s1_ref, s2_ref, **geom)
    idx_dim = geom["idx_dim"]
    head_dim = k.shape[1] // n_heads
    xl_ref[...] = xl
    gate_ref[...] = gate
    qt_ref[0] = q.T.astype(BF16)
    qit_ref[0] = qi.T.astype(BF16)
    kt_ref[0] = k.T
    for h in range(n_heads):
        khm_ref[0, h] = k[:, h * head_dim:(h + 1) * head_dim].astype(BF16)
    vt = v.T
    vt_ref[0] = vt
    for j in range(vt.shape[1] // cha):
        vtc_ref[0, j] = vt[:, j * cha:(j + 1) * cha].astype(BF16)
    tail_t = tail.T
    kidxt_ref[0] = tail_t[:idx_dim]
    wt_ref[0] = tail_t[idx_dim:idx_dim + idx_heads]
    kidx_ref[0] = tail[:, :idx_dim].astype(BF16)


def _rope_tables(pos, head_dim):
    rope_dim = head_dim // ROPE_FRACTION
    half = rope_dim // 2
    freqs = ROPE_THETA ** (-jnp.arange(half, dtype=F32) / half)
    ang = pos.astype(F32)[:, None] * freqs[None, :]
    cos, sin = jnp.cos(ang), jnp.sin(ang)
    n = pos.shape[0]
    rest = head_dim - rope_dim
    c = jnp.concatenate([cos, cos, jnp.ones((n, rest), F32)], 1)
    s1 = jnp.concatenate([-sin, jnp.zeros((n, half + rest), F32)], 1)
    s2 = jnp.concatenate([jnp.zeros((n, half), F32), sin, jnp.zeros((n, rest), F32)], 1)
    reps = LANES // head_dim
    return [jnp.tile(t, (1, reps)) for t in (c, s1, s2)], half


def _project_rows(x2d, w_pad, pos, *, d_lru, d_att, d_qi, idx_dim, head_dim, tm):
    n, d_model = x2d.shape
    (c, s1, s2), half = _rope_tables(pos, head_dim)
    row = lambda w: pl.BlockSpec((tm, w), lambda i: (i, 0))
    outs = [
        (d_lru, F32), (d_lru, F32), (d_att, BF16), (d_att, F32), (d_att, BF16),
        (d_att, F32), (d_att, BF16), (d_qi, BF16), (LANES, F32)]
    body = functools.partial(_proj_rows_body, d_lru=d_lru, d_att=d_att, d_qi=d_qi,
                             idx_dim=idx_dim, rope_half=half, q_scale=head_dim ** -0.5)
    return pl.pallas_call(
        body,
        grid=(n // tm,),
        in_specs=[row(d_model), pl.BlockSpec(w_pad.shape, lambda i: (0, 0)),
                  row(LANES), row(LANES), row(LANES)],
        out_specs=[row(w) for w, _ in outs],
        out_shape=[jax.ShapeDtypeStruct((n, w), dt) for w, dt in outs],
        compiler_params=_params("parallel"),
        name="proj",
    )(x2d, w_pad, c, s1, s2)


def _project_cols(x3d, w_pad, *, d_lru, d_att, d_qi, idx_dim, head_dim, tm, cha, n_heads,
                  idx_heads):
    bsz, t, d_model = x3d.shape
    n, nt = bsz * t, t // tm
    (c, s1, s2), half = _rope_tables(jnp.arange(t), head_dim)
    row = lambda w: pl.BlockSpec((tm, w), lambda i: (i, 0))
    table = pl.BlockSpec((tm, LANES), lambda i: (i % nt, 0))
    col = lambda w: pl.BlockSpec((1, w, tm), lambda i: (i // nt, 0, i % nt))
    col_shape = lambda w, dt: jax.ShapeDtypeStruct((bsz, w, t), dt)
    body = functools.partial(
        _proj_cols_body, d_lru=d_lru, d_att=d_att, d_qi=d_qi, idx_dim=idx_dim, rope_half=half,
        q_scale=head_dim ** -0.5, cha=cha, n_heads=n_heads, idx_heads=idx_heads)
    return pl.pallas_call(
        body,
        grid=(n // tm,),
        in_specs=[row(d_model), pl.BlockSpec(w_pad.shape, lambda i: (0, 0)), table, table, table],
        out_specs=[
            row(d_lru), row(d_lru), col(d_att), col(d_qi), col(d_att),
            pl.BlockSpec((1, n_heads, tm, head_dim), lambda i: (i // nt, 0, i % nt, 0)),
            col(d_att),
            pl.BlockSpec((1, tm // cha, d_att, cha), lambda i: (i // nt, i % nt, 0, 0)),
            col(idx_dim), pl.BlockSpec((1, tm, idx_dim), lambda i: (i // nt, i % nt, 0)),
            col(idx_heads)],
        out_shape=[
            jax.ShapeDtypeStruct((n, d_lru), F32), jax.ShapeDtypeStruct((n, d_lru), F32),
            col_shape(d_att, BF16), col_shape(d_qi, BF16), col_shape(d_att, F32),
            jax.ShapeDtypeStruct((bsz, n_heads, t, head_dim), BF16),
            col_shape(d_att, F32),
            jax.ShapeDtypeStruct((bsz, t // cha, d_att, cha), BF16),
            col_shape(idx_dim, F32), jax.ShapeDtypeStruct((bsz, t, idx_dim), BF16),
            col_shape(idx_heads, F32)],
        compiler_params=_params("parallel"),
        name="proj_prompt",
    )(x3d.reshape(n, d_model), w_pad, c, s1, s2)


def _softplus(x):
    return jnp.maximum(x, 0.0) + jnp.log1p(jnp.exp(-jnp.abs(x)))


def _gelu_tanh(x):
    return 0.5 * x * (1.0 + jnp.tanh(np.sqrt(2.0 / np.pi).astype(np.float32)
                                     * (x + 0.044715 * (x * x * x))))


def _lru_gates(xc, wa_ref, wx_ref, b_a, b_x, lam):
    xcb = xc.astype(BF16)
    r = jax.nn.sigmoid(jnp.dot(xcb, wa_ref[...], preferred_element_type=F32) + b_a)
    i = jax.nn.sigmoid(jnp.dot(xcb, wx_ref[...], preferred_element_type=F32) + b_x)
    log_a = -LRU_C * r * _softplus(-lam)
    a = jnp.exp(log_a)
    t = jnp.tanh(log_a)
    b = jnp.sqrt(-2.0 * t / (1.0 - t)) * (i * xc)
    return a, b


def _rms_gain(y, g):
    return y * lax.rsqrt(jnp.mean(y * y, axis=-1, keepdims=True) + RMS_EPS) * g


def _lru_prompt_body(xl_ref, gate_ref, cprev_ref, h0_ref, cw_ref, p_ref, wa_ref, wx_ref,
                     mix_ref, hlast_ref, ext_ref, hc_ref, *, tt):
    j = pl.program_id(1)

    @pl.when(j == 0)
    def _():
        ext_ref[0:SUBLANES, :] = cprev_ref[0]
        hc_ref[0:1, :] = h0_ref[0]

    xl = xl_ref[0]
    ext_ref[SUBLANES:SUBLANES + tt, :] = xl
    conv_b, b_a, b_x, lam, gn = (p_ref[r:r + 1, :] for r in range(5))
    xc = conv_b + (cw_ref[0:1, :] * ext_ref[SUBLANES - 3:SUBLANES - 3 + tt, :]
                   + cw_ref[1:2, :] * ext_ref[SUBLANES - 2:SUBLANES - 2 + tt, :]
                   + cw_ref[2:3, :] * ext_ref[SUBLANES - 1:SUBLANES - 1 + tt, :]
                   + cw_ref[3:4, :] * xl)
    ext_ref[0:SUBLANES, :] = ext_ref[tt:tt + SUBLANES, :]

    a, b = _lru_gates(xc, wa_ref, wx_ref, b_a, b_x, lam)
    groups = tt // SUBLANES
    a = a.reshape(groups, SUBLANES, a.shape[1])
    b = b.reshape(groups, SUBLANES, b.shape[1])
    row = lax.broadcasted_iota(I32, a.shape, 1)
    d = 1
    while d < SUBLANES:
        keep = row >= d
        a_prev = jnp.where(keep, pltpu.roll(a, d, 1), 1.0)
        b_prev = jnp.where(keep, pltpu.roll(b, d, 1), 0.0)
        b = a * b_prev + b
        a = a * a_prev
        d *= 2
    state = hc_ref[0:1, :]
    hs = []
    for g in range(groups):
        hs.append(a[g] * state + b[g])
        state = hs[-1][SUBLANES - 1:SUBLANES, :]
    h = jnp.concatenate(hs, axis=0)
    hc_ref[0:1, :] = state
    hlast_ref[0] = state
    mix_ref[0] = _rms_gain(h * _gelu_tanh(gate_ref[0]), gn).astype(BF16)


def _lru_prompt(xl, gate, conv_prev, h0, conv_w, pvec, wa_bd, wx_bd, *, tt):
    bsz, t, d = xl.shape
    cprev8 = jnp.concatenate(
        [jnp.zeros((bsz, SUBLANES - (CONV_W - 1), d), F32), conv_prev.astype(F32)], axis=1)
    const = lambda shape: pl.BlockSpec(shape, lambda b, j: (0,) * len(shape))
    return pl.pallas_call(
        functools.partial(_lru_prompt_body, tt=tt),
        grid=(bsz, t // tt),
        in_specs=[pl.BlockSpec((1, tt, d), lambda b, j: (b, j, 0)),
                  pl.BlockSpec((1, tt, d), lambda b, j: (b, j, 0)),
                  pl.BlockSpec((1, SUBLANES, d), lambda b, j: (b, 0, 0)),
                  pl.BlockSpec((1, 1, d), lambda b, j: (b, 0, 0)),
                  const(conv_w.shape), const(pvec.shape), const(wa_bd.shape), const(wx_bd.shape)],
        out_specs=[pl.BlockSpec((1, tt, d), lambda b, j: (b, j, 0)),
                   pl.BlockSpec((1, 1, d), lambda b, j: (b, 0, 0))],
        out_shape=[jax.ShapeDtypeStruct((bsz, t, d), BF16),
                   jax.ShapeDtypeStruct((bsz, 1, d), F32)],
        scratch_shapes=[pltpu.VMEM((tt + SUBLANES, d), F32), pltpu.VMEM((SUBLANES, d), F32)],
        compiler_params=_params("parallel", "arbitrary"),
        name="lru_prompt",
    )(xl, gate, cprev8, h0.astype(F32)[:, None, :], conv_w, pvec, wa_bd, wx_bd)


def _lru_sample_body(xl_ref, gate_ref, cprev_ref, h0_ref, cw_ref, p_ref, wa_ref, wx_ref,
                     mix_ref, hlast_ref, *, t_len):
    conv_b, b_a, b_x, lam, gn = (p_ref[r:r + 1, :] for r in range(5))
    xp = [cprev_ref[s] for s in range(CONV_W - 1)] + [xl_ref[s] for s in range(t_len)]
    h = h0_ref[...]
    for s in range(t_len):
        xc = conv_b + (cw_ref[0:1, :] * xp[s] + cw_ref[1:2, :] * xp[s + 1]
                       + cw_ref[2:3, :] * xp[s + 2] + cw_ref[3:4, :] * xp[s + 3])
        a, b = _lru_gates(xc, wa_ref, wx_ref, b_a, b_x, lam)
        h = a * h + b
        mix_ref[s] = _rms_gain(h * _gelu_tanh(gate_ref[s]), gn).astype(BF16)
    hlast_ref[...] = h


def _lru_sample(xl_t, gate_t, cprev_t, h0, conv_w, pvec, wa_bd, wx_bd):
    t_len, dbs, d = xl_t.shape
    return pl.pallas_call(
        functools.partial(_lru_sample_body, t_len=t_len),
        out_shape=[jax.ShapeDtypeStruct((t_len, dbs, d), BF16),
                   jax.ShapeDtypeStruct((dbs, d), F32)],
        compiler_params=pltpu.CompilerParams(vmem_limit_bytes=VMEM_LIMIT),
        name="lru_sample",
    )(xl_t, gate_t, cprev_t, h0.astype(F32), conv_w, pvec, wa_bd, wx_bd)


KEY_NEG_INF = INT_MIN + 0x7FFFFF
REFINE_STEPS = 8


def _key_to_float(key):
    return pltpu.bitcast(jnp.where(key >= 0, key, key ^ 0x7FFFFFFF), F32)


def _threshold_search(count, cell_span, total, topk, theta_ref, need_ref):
    shape = theta_ref.shape

    def value_step(it, carry):
        base, n_base = carry
        trial = base ^ lax.shift_left(jnp.int32(1), jnp.int32(31) - it)
        trial_f = _key_to_float(trial)
        n = count(lambda s: s >= trial_f)
        ok = n >= topk
        return jnp.where(ok, trial, base), jnp.where(ok, n, n_base)

    theta_key, n_ge = lax.fori_loop(
        0, 32, value_step, (jnp.full(shape, INT_MIN, I32), jnp.zeros(shape, I32) + total))
    theta = jnp.where(theta_key < KEY_NEG_INF, -jnp.inf, _key_to_float(theta_key))
    theta_ref[...] = theta
    has_surplus = jnp.max(n_ge) > topk

    @pl.when(has_surplus)
    def _():
        above = _key_to_float(jnp.maximum(theta_key, KEY_NEG_INF) + 1)
        cell_min, cell_max = cell_span(theta, above)

        @pl.when(jnp.max(jnp.where(cell_max > cell_min, 1, 0)) > 0)
        def _():
            def refine_step(_, carry):
                lo, hi = carry
                mid = lo + 0.5 * (hi - lo)
                ok = count(lambda s: s >= mid) >= topk
                return jnp.where(ok, mid, lo), jnp.where(ok, hi, mid)

            theta_ref[...] = lax.fori_loop(0, REFINE_STEPS, refine_step, (theta, above))[0]

        theta_fine = theta_ref[...]
        need_ref[...] = (topk - count(lambda s: s > theta_fine)).astype(F32)

    return has_surplus


def _strict_triangle(n, lower):
    r = lax.broadcasted_iota(I32, (n, n), 0)
    c = lax.broadcasted_iota(I32, (n, n), 1)
    return jnp.where((c < r) if lower else (r < c), 1.0, 0.0).astype(BF16)


def _attn_prompt_body(qit_ref, wt_ref, qt_ref, kidx_ref, k_ref, vt_ref, gn_ref, out_ref,
                      sc_ref, y_ref, theta_ref, need_ref, m_ref, l_ref, s_ref, mx_ref, *, tq, ch,
                      cha, topk, n_heads, head_dim, idx_heads, idx_dim):
    i = pl.program_id(1)
    n_keys = (i + 1) * tq
    qpos = i * tq + lax.broadcasted_iota(I32, (1, tq), 1)
    sub = LANES
    kpos_sub = lax.broadcasted_iota(I32, (sub, tq), 0)
    kpos_ch = lax.broadcasted_iota(I32, (ch, tq), 0)
    nc = (i + 1) * (tq // ch)

    def score_chunk(c, carry):
        for j in range(ch // sub):
            start = pl.multiple_of(c * ch + j * sub, sub)
            rows = pl.ds(start, sub)
            kc = kidx_ref[0, rows, :]
            acc = jnp.zeros((sub, tq), F32)
            for h in range(idx_heads):
                d = jnp.dot(kc, qit_ref[0, h * idx_dim:(h + 1) * idx_dim, :],
                            preferred_element_type=F32)
                acc = acc + jnp.maximum(d, 0.0) * wt_ref[0, h:h + 1, :]
            sc_ref[rows, :] = jnp.where(start + kpos_sub <= qpos, acc, -jnp.inf)
        return carry

    n_pairs = lax.div(nc, 2)
    lax.fori_loop(0, n_pairs, lambda j, c: score_chunk(2 * j + 1, score_chunk(2 * j, c)), 0)
    lax.fori_loop(2 * n_pairs, nc, score_chunk, 0)

    chunk_rows = lambda c: pl.ds(pl.multiple_of(c * ch, ch), ch)
    fold = lambda x, op: op(x.reshape(ch // SUBLANES, SUBLANES, tq), axis=0)

    def count(pred):
        def body(c, acc):
            hit = jnp.where(pred(sc_ref[chunk_rows(c), :]), 1, 0).astype(I32)
            return acc + fold(hit, jnp.sum)
        acc = lax.fori_loop(0, n_pairs, lambda j, acc: body(2 * j + 1, body(2 * j, acc)),
                            jnp.zeros((SUBLANES, tq), I32))
        acc = lax.fori_loop(2 * n_pairs, nc, body, acc)
        return jnp.sum(acc, axis=0, keepdims=True)

    def cell_span(lo, hi):
        def body(c, carry):
            s = sc_ref[chunk_rows(c), :]
            inside = (s >= lo) & (s < hi)
            return (jnp.minimum(carry[0], fold(jnp.where(inside, s, jnp.inf), jnp.min)),
                    jnp.maximum(carry[1], fold(jnp.where(inside, s, -jnp.inf), jnp.max)))
        init = (jnp.full((SUBLANES, tq), jnp.inf, F32), jnp.full((SUBLANES, tq), -jnp.inf, F32))
        lo_acc, hi_acc = lax.fori_loop(0, nc, body, init)
        return (jnp.min(lo_acc, axis=0, keepdims=True), jnp.max(hi_acc, axis=0, keepdims=True))

    has_surplus = _threshold_search(count, cell_span, n_keys, topk, theta_ref, need_ref)
    theta = theta_ref[...]

    @pl.when(jnp.logical_not(has_surplus))
    def _():
        def bias_chunk(c, carry):
            rows = chunk_rows(c)
            sel = (sc_ref[rows, :] >= theta) & (c * ch + kpos_ch <= qpos)
            sc_ref[rows, :] = jnp.where(sel, 0.0, MASKED)
            return carry
        lax.fori_loop(0, nc, bias_chunk, 0)

    @pl.when(has_surplus)
    def _():
        need = need_ref[...]
        lower = _strict_triangle(ch, lower=True)

        def bias_chunk(c, seen):
            rows = chunk_rows(c)
            s = sc_ref[rows, :]
            tie = jnp.where(s == theta, 1.0, 0.0)
            rank = seen + jnp.dot(lower, tie.astype(BF16), preferred_element_type=F32)
            sel = ((s > theta) | ((s == theta) & (rank < need))) & (c * ch + kpos_ch <= qpos)
            sc_ref[rows, :] = jnp.where(sel, 0.0, MASKED)
            return seen + jnp.sum(fold(tie, jnp.sum), axis=0, keepdims=True)
        lax.fori_loop(0, nc, bias_chunk, jnp.zeros((1, tq), F32))

    heads = [(h, slice(h * head_dim, (h + 1) * head_dim)) for h in range(n_heads)]
    m_ref[...] = jnp.full(m_ref.shape, -jnp.inf, F32)
    l_ref[...] = jnp.zeros(l_ref.shape, F32)
    y_ref[...] = jnp.zeros(y_ref.shape, F32)

    nca = (i + 1) * (tq // cha)

    def logits(c, slot):
        rows = pl.ds(pl.multiple_of(c * cha, cha), cha)
        bias = sc_ref[rows, :]
        for h, hs in heads:
            s = jnp.dot(k_ref[0, h, rows, :], qt_ref[0, hs, :],
                        preferred_element_type=F32) + bias
            s_ref[slot, h] = s
            mx_ref[slot, h:h + 1, :] = jnp.max(s, axis=0, keepdims=True)

    def update(c, slot):
        for h, hs in heads:
            m = m_ref[h:h + 1, :]
            m_new = jnp.maximum(m, mx_ref[slot, h:h + 1, :])
            alpha = jnp.exp(m - m_new)
            p = jnp.exp(s_ref[slot, h] - m_new)
            m_ref[h:h + 1, :] = m_new
            l_ref[h:h + 1, :] = alpha * l_ref[h:h + 1, :] + jnp.sum(p, axis=0, keepdims=True)
            y_ref[hs, :] = alpha * y_ref[hs, :] + jnp.dot(
                vt_ref[0, c, hs, :], p.astype(BF16), preferred_element_type=F32)

    logits(0, 0)

    def chunk_pair(j, carry):
        c0 = 2 * j
        logits(c0 + 1, 1)
        update(c0, 0)
        logits(jnp.minimum(c0 + 2, nca - 1), 0)
        update(c0 + 1, 1)
        return carry

    lax.fori_loop(0, nca // 2, chunk_pair, 0)
    for h, hs in heads:
        y_ref[hs, :] = y_ref[hs, :] / l_ref[h:h + 1, :]
    out_ref[0] = _rms_gain(y_ref[...].T, gn_ref[...]).astype(BF16)


def _attn_prompt(qit, wt, qt, kidxb, k_hm, vt, gn_att, *, tq, topk, idx_dim):
    bsz, d_att, t = qt.shape
    n_heads, head_dim = k_hm.shape[1], k_hm.shape[3]
    ch = tq
    cha = vt.shape[3]
    assert (tq // cha) % 2 == 0
    cols = lambda a: pl.BlockSpec((1, a.shape[1], tq), lambda b, i: (b, 0, i))
    full = lambda a: pl.BlockSpec((1,) + a.shape[1:], lambda b, i: (b,) + (0,) * (a.ndim - 1))
    body = functools.partial(
        _attn_prompt_body, tq=tq, ch=ch, cha=cha, topk=topk, n_heads=n_heads,
        head_dim=head_dim, idx_heads=wt.shape[1], idx_dim=idx_dim)
    return pl.pallas_call(
        body,
        grid=(bsz, t // tq),
        in_specs=[cols(qit), cols(wt), cols(qt), full(kidxb), full(k_hm), full(vt),
                  pl.BlockSpec((1, d_att), lambda b, i: (0, 0))],
        out_specs=pl.BlockSpec((1, tq, d_att), lambda b, i: (b, i, 0)),
        out_shape=jax.ShapeDtypeStruct((bsz, t, d_att), BF16),
        scratch_shapes=[pltpu.VMEM((t, tq), F32), pltpu.VMEM((d_att, tq), F32),
                        pltpu.VMEM((1, tq), F32), pltpu.VMEM((1, tq), F32),
                        pltpu.VMEM((n_heads, tq), F32), pltpu.VMEM((n_heads, tq), F32),
                        pltpu.VMEM((2, n_heads, cha, tq), F32),
                        pltpu.VMEM((2, n_heads, tq), F32)],
        compiler_params=_params("parallel", "arbitrary"),
        name="attn_prompt",
    )(qit, wt, qt, kidxb, k_hm, vt, gn_att)


def _score_sample_body(pt_ref, qi_ref, w_ref, *refs, group, n_pages, page, t_len, idx_heads,
                       past_len):
    del pt_ref
    page_refs, new_ref, out_ref = refs[:group * n_pages], refs[-2], refs[-1]
    for g in range(group):
        qi = qi_ref[g]
        w = w_ref[g]
        pages = page_refs[g * n_pages:(g + 1) * n_pages]
        kt = jnp.concatenate([r[0].astype(BF16) for r in pages] + [new_ref[g]], axis=1)
        d = jnp.maximum(jnp.dot(qi, kt, preferred_element_type=F32), 0.0) * w
        s = jnp.sum(d.reshape(t_len, idx_heads, past_len + page), axis=1)
        tpos = past_len + lax.broadcasted_iota(I32, s.shape, 0)
        kpos = lax.broadcasted_iota(I32, s.shape, 1)
        out_ref[g] = jnp.where(kpos <= tpos, s, -jnp.inf)


def _select_sample_body(s_ref, bias_ref, sc_ref, theta_ref, need_ref, *, topk, page):
    nc, rows, _ = sc_ref.shape
    for c in range(nc):
        sc_ref[c] = s_ref[:, c * page:(c + 1) * page]

    def count(pred):
        def body(c, acc):
            return acc + jnp.where(pred(sc_ref[c]), 1, 0).astype(I32)
        acc = lax.fori_loop(0, nc, body, jnp.zeros((rows, page), I32))
        return jnp.sum(acc, axis=1, keepdims=True)

    def cell_span(lo, hi):
        def body(c, carry):
            s = sc_ref[c]
            inside = (s >= lo) & (s < hi)
            return (jnp.minimum(carry[0], jnp.where(inside, s, jnp.inf)),
                    jnp.maximum(carry[1], jnp.where(inside, s, -jnp.inf)))
        init = (jnp.full((rows, page), jnp.inf, F32), jnp.full((rows, page), -jnp.inf, F32))
        lo_acc, hi_acc = lax.fori_loop(0, nc, body, init)
        return (jnp.min(lo_acc, axis=1, keepdims=True), jnp.max(hi_acc, axis=1, keepdims=True))

    has_surplus = _threshold_search(count, cell_span, nc * page, topk, theta_ref, need_ref)
    theta = theta_ref[...]

    @pl.when(jnp.logical_not(has_surplus))
    def _():
        for c in range(nc):
            s = sc_ref[c]
            sel = (s >= theta) & (s > -jnp.inf)
            bias_ref[:, c * page:(c + 1) * page] = jnp.where(sel, 0.0, MASKED)

    @pl.when(has_surplus)
    def _():
        need = need_ref[...]
        upper = _strict_triangle(page, lower=False)
        seen = jnp.zeros((rows, 1), F32)
        for c in range(nc):
            s = sc_ref[c]
            tie = jnp.where(s == theta, 1.0, 0.0)
            rank = seen + jnp.dot(tie.astype(BF16), upper, preferred_element_type=F32)
            sel = ((s > theta) | ((s == theta) & (rank < need))) & (s > -jnp.inf)
            bias_ref[:, c * page:(c + 1) * page] = jnp.where(sel, 0.0, MASKED)
            seen = seen + jnp.sum(tie, axis=1, keepdims=True)


def _attn_sample_body(pt_ref, q_ref, bias_ref, gn_ref, *refs, n_pages, page, t_len, n_heads,
                      head_dim):
    del pt_ref
    k_refs, v_refs = refs[:n_pages], refs[n_pages:2 * n_pages]
    knew_ref, vnew_ref, out_ref = refs[2 * n_pages:]
    past = n_pages * page
    bias = bias_ref[0]
    outs = []
    del past
    logits = []
    for h in range(n_heads):
        kt = jnp.concatenate([r[0, h].astype(BF16) for r in k_refs] + [knew_ref[0, h]], axis=1)
        logits.append(jnp.dot(q_ref[0, h], kt, preferred_element_type=F32) + bias)
    for h in range(n_heads):
        s = logits[h]
        m = jnp.max(s, axis=1, keepdims=True)
        p = jnp.exp(s - m)
        l = jnp.sum(p, axis=1, keepdims=True)
        vt = jnp.concatenate([r[0, h].astype(BF16) for r in v_refs] + [vnew_ref[0, h]], axis=1)
        outs.append(_nt_dot(p.astype(BF16), vt) / l)
    y = jnp.concatenate(outs, axis=1)
    out_ref[0] = _rms_gain(y, gn_ref[...]).astype(BF16)


def _attn_sample(qib, wi, qb, kidx_new, k_new, v_new, cache_k, cache_v, cache_kidx, page_table,
                 gn_att, *, topk, n_heads):
    dbs, t_len, d_att = qb.shape
    n_pages = page_table.shape[1]
    page = cache_k.shape[1]
    idx_dim = cache_kidx.shape[2]
    idx_heads = wi.shape[2]
    head_dim = d_att // n_heads
    nkp = (n_pages + 1) * page
    past_len = n_pages * page
    pad = lambda a: jnp.pad(a, ((0, 0), (0, page - t_len), (0, 0)))
    kidx_t = jnp.transpose(cache_kidx, (0, 2, 1))
    k_t = jnp.transpose(cache_k, (0, 2, 3, 1))
    v_t = jnp.transpose(cache_v, (0, 2, 3, 1))
    new_t = lambda a: jnp.transpose(pad(a).reshape(dbs, page, n_heads, head_dim), (0, 2, 3, 1))

    def paged(shape):
        return [pl.BlockSpec((1,) + shape, functools.partial(
            lambda b, pt, p: (pt[b, p],) + (0,) * len(shape), p=p)) for p in range(n_pages)]

    per_seq = lambda shape: pl.BlockSpec((1,) + shape, lambda b, pt: (b,) + (0,) * len(shape))

    group = _row_tile(dbs, 4)
    seq_group = lambda shape: pl.BlockSpec((group,) + shape,
                                           lambda b, pt: (b,) + (0,) * len(shape))
    group_pages = [pl.BlockSpec((1, idx_dim, page), functools.partial(
        lambda b, pt, g, p: (pt[b * group + g, p], 0, 0), g=g, p=p))
        for g in range(group) for p in range(n_pages)]
    scores = pl.pallas_call(
        functools.partial(_score_sample_body, group=group, n_pages=n_pages, page=page,
                          t_len=t_len, idx_heads=idx_heads, past_len=past_len),
        grid_spec=pltpu.PrefetchScalarGridSpec(
            num_scalar_prefetch=1, grid=(dbs // group,),
            in_specs=[seq_group((t_len * idx_heads, idx_dim)), seq_group((t_len * idx_heads, 1))]
                     + group_pages + [seq_group((idx_dim, page))],
            out_specs=seq_group((t_len, nkp))),
        out_shape=jax.ShapeDtypeStruct((dbs, t_len, nkp), F32),
        compiler_params=_params("parallel"),
        name="score_sample",
    )(page_table, qib.reshape(dbs, t_len * idx_heads, idx_dim),
      wi.reshape(dbs, t_len * idx_heads, 1), *([kidx_t] * (group * n_pages)),
      jnp.transpose(pad(kidx_new), (0, 2, 1)))

    rows = dbs * t_len
    tr = min(rows, 128)
    bias = pl.pallas_call(
        functools.partial(_select_sample_body, topk=topk, page=page),
        grid=(rows // tr,),
        in_specs=[pl.BlockSpec((tr, nkp), lambda r: (r, 0))],
        out_specs=pl.BlockSpec((tr, nkp), lambda r: (r, 0)),
        out_shape=jax.ShapeDtypeStruct((rows, nkp), F32),
        scratch_shapes=[pltpu.VMEM((n_pages + 1, tr, page), F32), pltpu.VMEM((tr, 1), F32),
                        pltpu.VMEM((tr, 1), F32)],
        compiler_params=_params("parallel"),
        name="select_sample",
    )(scores.reshape(rows, nkp)).reshape(dbs, t_len, nkp)

    kv_pages = paged((n_heads, head_dim, page))
    return pl.pallas_call(
        functools.partial(_attn_sample_body, n_pages=n_pages, page=page, t_len=t_len,
                          n_heads=n_heads, head_dim=head_dim),
        grid_spec=pltpu.PrefetchScalarGridSpec(
            num_scalar_prefetch=1, grid=(dbs,),
            in_specs=[per_seq((n_heads, t_len, head_dim)), per_seq((t_len, nkp)),
                      pl.BlockSpec((1, d_att), lambda b, pt: (0, 0))]
                     + kv_pages + kv_pages
                     + [per_seq((n_heads, head_dim, page)), per_seq((n_heads, head_dim, page))],
            out_specs=per_seq((t_len, d_att))),
        out_shape=jax.ShapeDtypeStruct((dbs, t_len, d_att), BF16),
        compiler_params=_params("parallel"),
        name="attn_sample",
    )(page_table, qb.reshape(dbs, t_len, n_heads, head_dim).transpose(0, 2, 1, 3), bias, gn_att,
      *([k_t] * n_pages), *([v_t] * n_pages), new_t(k_new), new_t(v_new))


def _layer_norm(x, g, b):
    mu = jnp.mean(x, axis=-1, keepdims=True)
    xc = x - mu
    var = jnp.mean(xc * xc, axis=-1, keepdims=True)
    return xc * lax.rsqrt(var + LN_EPS) * g + b


def _finish_body(x_ref, ml_ref, ma_ref, wo_ref, wfi_ref, wfo_ref, p_ref, out_ref,
                 *, alpha, d_lru, d_ff, fc):
    ln1_g, ln1_b, ln2_g, ln2_b = (p_ref[r:r + 1, :] for r in range(4))
    y = (jnp.dot(ml_ref[...], wo_ref[0:d_lru, :], preferred_element_type=F32)
         + jnp.dot(ma_ref[...], wo_ref[d_lru:, :], preferred_element_type=F32))
    x1 = _layer_norm(alpha * x_ref[...] + y, ln1_g, ln1_b)
    x1b = x1.astype(BF16)
    f = jnp.zeros(x1.shape, F32)
    for c in range(d_ff // fc):
        u = jnp.dot(x1b, wfi_ref[:, c * fc:(c + 1) * fc], preferred_element_type=F32)
        g = jnp.dot(x1b, wfi_ref[:, d_ff + c * fc:d_ff + (c + 1) * fc],
                    preferred_element_type=F32)
        hidden = (g * jax.nn.sigmoid(g) * u).astype(BF16)
        f = f + jnp.dot(hidden, wfo_ref[c * fc:(c + 1) * fc, :], preferred_element_type=F32)
    out_ref[...] = _layer_norm(alpha * x1 + f, ln2_g, ln2_b)


def _finish(x2d, mix_lru, mix_att, wo, wfi, wfo, pvec, *, alpha, tm):
    n, d_model = x2d.shape
    d_lru = mix_lru.shape[1]
    d_ff = wfo.shape[0]
    fc = 2 * LANES if d_ff % (2 * LANES) == 0 else LANES
    row = lambda w: pl.BlockSpec((tm, w), lambda i: (i, 0))
    const = lambda a: pl.BlockSpec(a.shape, lambda i: (0, 0), pipeline_mode=pl.Buffered(1))
    return pl.pallas_call(
        functools.partial(_finish_body, alpha=alpha, d_lru=d_lru, d_ff=d_ff, fc=fc),
        grid=(n // tm,),
        in_specs=[row(d_model), row(d_lru), row(mix_att.shape[1]),
                  const(wo), const(wfi), const(wfo), const(pvec)],
        out_specs=row(d_model),
        out_shape=jax.ShapeDtypeStruct((n, d_model), F32),
        compiler_params=_params("parallel"),
        name="finish",
    )(x2d, mix_lru, mix_att, wo, wfi, wfo, pvec)


def _block_diag(w):
    nb, bi, bo = w.shape
    eye = jnp.eye(nb, dtype=w.dtype)
    return (w[:, :, None, :] * eye[:, None, :, None]).reshape(nb * bi, nb * bo)


def _row_tile(n, want):
    tm = min(n, want)
    while n % tm:
        tm //= 2
    return tm


def kernel(x_prompt, x_sample, cache_k, cache_v, cache_kidx, state_conv, state_h, page_table,
           w_in, conv_w, conv_b, w_a, b_a, w_x, b_x, lam, gn_lru, gn_att, w_out,
           ln1_g, ln1_b, w_ffn_in, w_ffn_out, ln2_g, ln2_b):
    depth, d_model, d_in = w_in.shape
    bsz, seq, _ = x_prompt.shape
    dbs, dseq, _ = x_sample.shape
    d_lru = conv_w.shape[2]
    n_phys, page, n_heads, head_dim = cache_k.shape[1:]
    d_att = n_heads * head_dim
    idx_dim = cache_kidx.shape[3]
    d_qi = IDX_HEADS * idx_dim
    n_pages = page_table.shape[1]
    past_len = n_pages * page
    alpha = (2.0 * depth) ** 0.25
    assert d_in == 2 * d_lru + 3 * d_att + d_qi + idx_dim + IDX_HEADS
    assert idx_dim + IDX_HEADS <= LANES and LANES % head_dim == 0 and head_dim == idx_dim
    geom = dict(d_lru=d_lru, d_att=d_att, d_qi=d_qi, idx_dim=idx_dim, head_dim=head_dim)
    d_main = d_in - idx_dim - IDX_HEADS

    pos_p = jnp.tile(jnp.arange(seq), bsz)
    pos_s = jnp.tile(past_len + jnp.arange(dseq), dbs)
    topk_p = min(TOPK_MAX, seq // 4)
    topk_s = min(TOPK_MAX, (past_len + dseq) // 4)

    xp = x_prompt.reshape(bsz * seq, d_model)
    xs = x_sample.reshape(dbs * dseq, d_model)
    outs_p, outs_s = [], []
    for l in range(depth):
        w_pad = jnp.pad(w_in[l], ((0, 0), (0, d_main + LANES - d_in))).astype(BF16)
        wa_bd = _block_diag(w_a[l]).astype(BF16)
        wx_bd = _block_diag(w_x[l]).astype(BF16)
        lru_vec = jnp.stack([conv_b[l], b_a[l], b_x[l], lam[l], gn_lru[l]]
                            + [jnp.zeros_like(lam[l])] * 3)
        fin_vec = jnp.stack([ln1_g[l], ln1_b[l], ln2_g[l], ln2_b[l]])
        wo, wfi, wfo = (w_out[l].astype(BF16), w_ffn_in[l].astype(BF16),
                        w_ffn_out[l].astype(BF16))
        gn_a = gn_att[l][None, :]

        tq = _row_tile(seq, 256)
        xl, gate, qt, qit, kt, k_hm, vt, vt_chunks, kidx_t, kidx_b, wt = _project_cols(
            xp.reshape(bsz, seq, d_model), w_pad, tm=_row_tile(seq, 512), cha=tq // 2,
            n_heads=n_heads, idx_heads=IDX_HEADS, **geom)
        xl3 = xl.reshape(bsz, seq, d_lru)
        mix_l, h_last = _lru_prompt(
            xl3, gate.reshape(bsz, seq, d_lru), jnp.zeros((bsz, CONV_W - 1, d_lru), F32),
            jnp.zeros((bsz, d_lru), F32), conv_w[l], lru_vec, wa_bd, wx_bd, tt=_row_tile(seq, 256))
        mix_a = _attn_prompt(qit, wt, qt, kidx_b, k_hm, vt_chunks, gn_a, tq=tq, topk=topk_p,
                             idx_dim=idx_dim)
        xp = _finish(xp, mix_l.reshape(bsz * seq, d_lru), mix_a.reshape(bsz * seq, d_att),
                     wo, wfi, wfo, fin_vec, alpha=alpha, tm=_row_tile(bsz * seq, 512))
        token_major = lambda a: a.reshape(bsz, n_heads, head_dim, seq).transpose(0, 3, 1, 2)
        outs_p.append((token_major(kt), token_major(vt), jnp.swapaxes(kidx_t, 1, 2),
                       xl3[:, seq - (CONV_W - 1):], h_last[:, 0]))

        xl, gate, qb, k, kb, v, vb, qib, tail = _project_rows(
            xs, w_pad, pos_s, tm=_row_tile(dbs * dseq, 256), **geom)
        d3 = lambda a: a.reshape(dbs, dseq, a.shape[-1])
        tm_major = lambda a: jnp.swapaxes(d3(a), 0, 1)
        xl3 = d3(xl)
        mix_l, h_last = _lru_sample(
            tm_major(xl), tm_major(gate), jnp.swapaxes(state_conv[l], 0, 1).astype(F32),
            state_h[l], conv_w[l], lru_vec, wa_bd, wx_bd)
        tail3 = d3(tail)
        ki = tail3[:, :, :idx_dim]
        wi = tail3[:, :, idx_dim:idx_dim + IDX_HEADS]
        mix_a = _attn_sample(
            d3(qib), wi, d3(qb), ki.astype(BF16), d3(kb), d3(vb),
            cache_k[l], cache_v[l], cache_kidx[l], page_table, gn_a, topk=topk_s,
            n_heads=n_heads)
        xs = _finish(xs, jnp.swapaxes(mix_l, 0, 1).reshape(dbs * dseq, d_lru),
                     mix_a.reshape(dbs * dseq, d_att), wo, wfi, wfo, fin_vec, alpha=alpha,
                     tm=_row_tile(dbs * dseq, 256))
        conv_new = jnp.concatenate([state_conv[l].astype(F32), xl3], axis=1)[:, -(CONV_W - 1):]
        outs_s.append((k.reshape(dbs, dseq, n_heads, head_dim), v.reshape(dbs, dseq, n_heads, head_dim),
                       ki, conv_new, h_last))

    stack = lambda outs, j: jnp.stack([o[j] for o in outs])
    return (xp.reshape(bsz, seq, d_model), xs.reshape(dbs, dseq, d_model),
            *(stack(outs_p, j) for j in range(5)), *(stack(outs_s, j) for j in range(5)))
```

```python
import functools

import jax
import jax.numpy as jnp
import numpy as np
from jax import lax
from jax.experimental import pallas as pl
from jax.experimental.pallas import tpu as pltpu

CONV_W = 4
LRU_C = 8.0
LRU_BLOCKS = 8
IDX_HEADS = 8
TOPK_MAX = 256
ROPE_FRACTION = 4
ROPE_THETA = 500000.0
RMS_EPS = 1e-6
LN_EPS = 1e-5

LANES = 128
SUBLANES = 8
VMEM_LIMIT = 56 * 1024 * 1024
MASKED = -1e30
INT_MIN = -2 ** 31

F32 = jnp.float32
BF16 = jnp.bfloat16
I32 = jnp.int32


def _params(*sem):
    return pltpu.CompilerParams(dimension_semantics=sem, vmem_limit_bytes=VMEM_LIMIT)


def _nt_dot(a, b):
    return lax.dot_general(a, b, (((1,), (1,)), ((), ())), preferred_element_type=F32)


def _proj_streams(x_ref, w_ref, c_ref, s1_ref, s2_ref, *, d_lru, d_att, d_qi, idx_dim, rope_half,
                  q_scale):
    xb = x_ref[...].astype(BF16)
    c, s1, s2 = c_ref[...], s1_ref[...], s2_ref[...]

    def proj(lo, width):
        return jnp.dot(xb, w_ref[:, lo:lo + width], preferred_element_type=F32)

    def tiled(t, width):
        reps = width // LANES
        return t if reps == 1 else jnp.concatenate([t] * reps, axis=1)

    def rope(z, cc, ss1, ss2):
        width = z.shape[1]
        return (z * cc + pltpu.roll(z, width - rope_half, 1) * ss1
                + pltpu.roll(z, rope_half, 1) * ss2)

    lo = 0
    xl = proj(lo, d_lru); lo += d_lru
    gate = proj(lo, d_lru); lo += d_lru
    q = rope(proj(lo, d_att), tiled(c, d_att), tiled(s1, d_att), tiled(s2, d_att)); lo += d_att
    k = rope(proj(lo, d_att), tiled(c, d_att), tiled(s1, d_att), tiled(s2, d_att)); lo += d_att
    v = proj(lo, d_att); lo += d_att
    qi = rope(proj(lo, d_qi), tiled(c, d_qi), tiled(s1, d_qi), tiled(s2, d_qi)); lo += d_qi
    tail = proj(lo, LANES)
    is_key = lax.broadcasted_iota(I32, tail.shape, 1) < idx_dim
    tail = rope(tail, jnp.where(is_key, c, 1.0), jnp.where(is_key, s1, 0.0),
                jnp.where(is_key, s2, 0.0))
    return xl, gate, q * q_scale, k, v, qi, tail


def _proj_rows_body(x_ref, w_ref, c_ref, s1_ref, s2_ref,
                    xl_ref, gate_ref, qb_ref, k_ref, kb_ref, v_ref, vb_ref, qib_ref, tail_ref,
                    **geom):
    xl, gate, q, k, v, qi, tail = _proj_streams(x_ref, w_ref, c_ref, s1_ref, s2_ref, **geom)
    xl_ref[...] = xl
    gate_ref[...] = gate
    qb_ref[...] = q.astype(BF16)
    k_ref[...] = k
    kb_ref[...] = k.astype(BF16)
    v_ref[...] = v
    vb_ref[...] = v.astype(BF16)
    qib_ref[...] = qi.astype(BF16)
    tail_ref[...] = tail


def _proj_cols_body(x_ref, w_ref, c_ref, s1_ref, s2_ref,
                    xl_ref, gate_ref, qt_ref, qit_ref, kt_ref, khm_ref, vt_ref, vtc_ref,
                    kidxt_ref, kidx_ref, wt_ref, *, cha, n_heads, idx_heads, **geom):
    xl, gate, q, k, v, qi, tail = _proj_streams(x_ref, w_ref, c_ref, s1_ref, s2_ref, **geom)
    idx_dim = geom["idx_dim"]
    head_dim = k.shape[1] // n_heads
    xl_ref[...] = xl
    gate_ref[...] = gate
    qt_ref[0] = q.T.astype(BF16)
    qit_ref[0] = qi.T.astype(BF16)
    kt_ref[0] = k.T
    for h in range(n_heads):
        khm_ref[0, h] = k[:, h * head_dim:(h + 1) * head_dim].astype(BF16)
    vt = v.T
    vt_ref[0] = vt
    for j in range(vt.shape[1] // cha):
        vtc_ref[0, j] = vt[:, j * cha:(j + 1) * cha].astype(BF16)
    tail_t = tail.T
    kidxt_ref[0] = tail_t[:idx_dim]
    wt_ref[0] = tail_t[idx_dim:idx_dim + idx_heads]
    kidx_ref[0] = tail[:, :idx_dim].astype(BF16)


def _rope_tables(pos, head_dim):
    rope_dim = head_dim // ROPE_FRACTION
    half = rope_dim // 2
    freqs = ROPE_THETA ** (-jnp.arange(half, dtype=F32) / half)
    dim = jnp.arange(LANES) % head_dim
    ang = pos.astype(F32)[:, None] * freqs[dim % half][None, :]
    cos, sin = jnp.cos(ang), jnp.sin(ang)
    c = jnp.where(dim < rope_dim, cos, 1.0)
    s1 = jnp.where(dim < half, -sin, 0.0)
    s2 = jnp.where((dim >= half) & (dim < rope_dim), sin, 0.0)
    return [c, s1, s2], half


def _project_rows(x2d, w_pad, pos, *, d_lru, d_att, d_qi, idx_dim, head_dim, tm):
    n, d_model = x2d.shape
    (c, s1, s2), half = _rope_tables(pos, head_dim)
    row = lambda w: pl.BlockSpec((tm, w), lambda i: (i, 0))
    outs = [
        (d_lru, F32), (d_lru, F32), (d_att, BF16), (d_att, F32), (d_att, BF16),
        (d_att, F32), (d_att, BF16), (d_qi, BF16), (LANES, F32)]
    body = functools.partial(_proj_rows_body, d_lru=d_lru, d_att=d_att, d_qi=d_qi,
                             idx_dim=idx_dim, rope_half=half, q_scale=head_dim ** -0.5)
    return pl.pallas_call(
        body,
        grid=(n // tm,),
        in_specs=[row(d_model), pl.BlockSpec(w_pad.shape, lambda i: (0, 0)),
                  row(LANES), row(LANES), row(LANES)],
        out_specs=[row(w) for w, _ in outs],
        out_shape=[jax.ShapeDtypeStruct((n, w), dt) for w, dt in outs],
        compiler_params=_params("parallel"),
        name="proj",
    )(x2d, w_pad, c, s1, s2)


def _project_cols(x3d, w_pad, *, d_lru, d_att, d_qi, idx_dim, head_dim, tm, cha, n_heads,
                  idx_heads):
    bsz, t, d_model = x3d.shape
    n, nt = bsz * t, t // tm
    (c, s1, s2), half = _rope_tables(jnp.arange(t), head_dim)
    row = lambda w: pl.BlockSpec((tm, w), lambda i: (i, 0))
    table = pl.BlockSpec((tm, LANES), lambda i: (i % nt, 0))
    col = lambda w: pl.BlockSpec((1, w, tm), lambda i: (i // nt, 0, i % nt))
    col_shape = lambda w, dt: jax.ShapeDtypeStruct((bsz, w, t), dt)
    body = functools.partial(
        _proj_cols_body, d_lru=d_lru, d_att=d_att, d_qi=d_qi, idx_dim=idx_dim, rope_half=half,
        q_scale=head_dim ** -0.5, cha=cha, n_heads=n_heads, idx_heads=idx_heads)
    return pl.pallas_call(
        body,
        grid=(n // tm,),
        in_specs=[row(d_model), pl.BlockSpec(w_pad.shape, lambda i: (0, 0)), table, table, table],
        out_specs=[
            row(d_lru), row(d_lru), col(d_att), col(d_qi), col(d_att),
            pl.BlockSpec((1, n_heads, tm, head_dim), lambda i: (i // nt, 0, i % nt, 0)),
            col(d_att),
            pl.BlockSpec((1, tm // cha, d_att, cha), lambda i: (i // nt, i % nt, 0, 0)),
            col(idx_dim), pl.BlockSpec((1, tm, idx_dim), lambda i: (i // nt, i % nt, 0)),
            col(idx_heads)],
        out_shape=[
            jax.ShapeDtypeStruct((n, d_lru), F32), jax.ShapeDtypeStruct((n, d_lru), F32),
            col_shape(d_att, BF16), col_shape(d_qi, BF16), col_shape(d_att, F32),
            jax.ShapeDtypeStruct((bsz, n_heads, t, head_dim), BF16),
            col_shape(d_att, F32),
            jax.ShapeDtypeStruct((bsz, t // cha, d_att, cha), BF16),
            col_shape(idx_dim, F32), jax.ShapeDtypeStruct((bsz, t, idx_dim), BF16),
            col_shape(idx_heads, F32)],
        compiler_params=_params("parallel"),
        name="proj_prompt",
    )(x3d.reshape(n, d_model), w_pad, c, s1, s2)


def _softplus(x):
    return jnp.maximum(x, 0.0) + jnp.log1p(jnp.exp(-jnp.abs(x)))


def _gelu_tanh(x):
    return 0.5 * x * (1.0 + jnp.tanh(np.sqrt(2.0 / np.pi).astype(np.float32)
                                     * (x + 0.044715 * (x * x * x))))


def _lru_gates(xc, wa_ref, wx_ref, b_a, b_x, lam):
    xcb = xc.astype(BF16)
    r = jax.nn.sigmoid(jnp.dot(xcb, wa_ref[...], preferred_element_type=F32) + b_a)
    i = jax.nn.sigmoid(jnp.dot(xcb, wx_ref[...], preferred_element_type=F32) + b_x)
    log_a = -LRU_C * r * _softplus(-lam)
    a = jnp.exp(log_a)
    t = jnp.tanh(log_a)
    b = jnp.sqrt(-2.0 * t / (1.0 - t)) * (i * xc)
    return a, b


def _rms_gain(y, g):
    return y * lax.rsqrt(jnp.mean(y * y, axis=-1, keepdims=True) + RMS_EPS) * g


def _lru_prompt_body(xl_ref, gate_ref, cprev_ref, h0_ref, cw_ref, p_ref, wa_ref, wx_ref,
                     mix_ref, hlast_ref, ext_ref, hc_ref, *, tt):
    j = pl.program_id(1)

    @pl.when(j == 0)
    def _():
        ext_ref[0:SUBLANES, :] = cprev_ref[0]
        hc_ref[0:1, :] = h0_ref[0]

    xl = xl_ref[0]
    ext_ref[SUBLANES:SUBLANES + tt, :] = xl
    conv_b, b_a, b_x, lam, gn = (p_ref[r:r + 1, :] for r in range(5))
    xc = conv_b + (cw_ref[0:1, :] * ext_ref[SUBLANES - 3:SUBLANES - 3 + tt, :]
                   + cw_ref[1:2, :] * ext_ref[SUBLANES - 2:SUBLANES - 2 + tt, :]
                   + cw_ref[2:3, :] * ext_ref[SUBLANES - 1:SUBLANES - 1 + tt, :]
                   + cw_ref[3:4, :] * xl)
    ext_ref[0:SUBLANES, :] = ext_ref[tt:tt + SUBLANES, :]

    a, b = _lru_gates(xc, wa_ref, wx_ref, b_a, b_x, lam)
    groups = tt // SUBLANES
    a = a.reshape(groups, SUBLANES, a.shape[1])
    b = b.reshape(groups, SUBLANES, b.shape[1])
    row = lax.broadcasted_iota(I32, a.shape, 1)
    d = 1
    while d < SUBLANES:
        keep = row >= d
        a_prev = jnp.where(keep, pltpu.roll(a, d, 1), 1.0)
        b_prev = jnp.where(keep, pltpu.roll(b, d, 1), 0.0)
        b = a * b_prev + b
        a = a * a_prev
        d *= 2
    state = hc_ref[0:1, :]
    hs = []
    for g in range(groups):
        hs.append(a[g] * state + b[g])
        state = hs[-1][SUBLANES - 1:SUBLANES, :]
    h = jnp.concatenate(hs, axis=0)
    hc_ref[0:1, :] = state
    hlast_ref[0] = state
    mix_ref[0] = _rms_gain(h * _gelu_tanh(gate_ref[0]), gn).astype(BF16)


def _lru_prompt(xl, gate, conv_prev, h0, conv_w, pvec, wa_bd, wx_bd, *, tt):
    bsz, t, d = xl.shape
    cprev8 = jnp.concatenate(
        [jnp.zeros((bsz, SUBLANES - (CONV_W - 1), d), F32), conv_prev.astype(F32)], axis=1)
    const = lambda shape: pl.BlockSpec(shape, lambda b, j: (0,) * len(shape))
    return pl.pallas_call(
        functools.partial(_lru_prompt_body, tt=tt),
        grid=(bsz, t // tt),
        in_specs=[pl.BlockSpec((1, tt, d), lambda b, j: (b, j, 0)),
                  pl.BlockSpec((1, tt, d), lambda b, j: (b, j, 0)),
                  pl.BlockSpec((1, SUBLANES, d), lambda b, j: (b, 0, 0)),
                  pl.BlockSpec((1, 1, d), lambda b, j: (b, 0, 0)),
                  const(conv_w.shape), const(pvec.shape), const(wa_bd.shape), const(wx_bd.shape)],
        out_specs=[pl.BlockSpec((1, tt, d), lambda b, j: (b, j, 0)),
                   pl.BlockSpec((1, 1, d), lambda b, j: (b, 0, 0))],
        out_shape=[jax.ShapeDtypeStruct((bsz, t, d), BF16),
                   jax.ShapeDtypeStruct((bsz, 1, d), F32)],
        scratch_shapes=[pltpu.VMEM((tt + SUBLANES, d), F32), pltpu.VMEM((SUBLANES, d), F32)],
        compiler_params=_params("parallel", "arbitrary"),
        name="lru_prompt",
    )(xl, gate, cprev8, h0.astype(F32)[:, None, :], conv_w, pvec, wa_bd, wx_bd)


def _lru_sample_body(xl_ref, gate_ref, cprev_ref, h0_ref, cw_ref, p_ref, wa_ref, wx_ref,
                     mix_ref, hlast_ref, *, t_len):
    conv_b, b_a, b_x, lam, gn = (p_ref[r:r + 1, :] for r in range(5))
    xp = [cprev_ref[s] for s in range(CONV_W - 1)] + [xl_ref[s] for s in range(t_len)]
    h = h0_ref[...]
    for s in range(t_len):
        xc = conv_b + (cw_ref[0:1, :] * xp[s] + cw_ref[1:2, :] * xp[s + 1]
                       + cw_ref[2:3, :] * xp[s + 2] + cw_ref[3:4, :] * xp[s + 3])
        a, b = _lru_gates(xc, wa_ref, wx_ref, b_a, b_x, lam)
        h = a * h + b
        mix_ref[s] = _rms_gain(h * _gelu_tanh(gate_ref[s]), gn).astype(BF16)
    hlast_ref[...] = h


def _lru_sample(xl_t, gate_t, cprev_t, h0, conv_w, pvec, wa_bd, wx_bd):
    t_len, dbs, d = xl_t.shape
    return pl.pallas_call(
        functools.partial(_lru_sample_body, t_len=t_len),
        out_shape=[jax.ShapeDtypeStruct((t_len, dbs, d), BF16),
                   jax.ShapeDtypeStruct((dbs, d), F32)],
        compiler_params=pltpu.CompilerParams(vmem_limit_bytes=VMEM_LIMIT),
        name="lru_sample",
    )(xl_t, gate_t, cprev_t, h0.astype(F32), conv_w, pvec, wa_bd, wx_bd)


KEY_NEG_INF = INT_MIN + 0x7FFFFF
REFINE_STEPS = 8


def _key_to_float(key):
    return pltpu.bitcast(jnp.where(key >= 0, key, key ^ 0x7FFFFFFF), F32)


KEY16_NEG_INF = -2 ** 15 + 0x7F


def _key16_to_bf16(key):
    bits = lax.shift_left(jnp.where(key >= 0, key, key ^ 0x7FFF), 16)
    return pltpu.bitcast(bits, F32).astype(BF16)


def _threshold_search(count, cell_span, total, topk, theta_ref, need_ref, count_bf16=None):
    shape = theta_ref.shape

    if count_bf16 is None:
        def value_step(it, carry):
            base, n_base = carry
            trial = base ^ lax.shift_left(jnp.int32(1), jnp.int32(31) - it)
            trial_f = _key_to_float(trial)
            n = count(lambda s: s >= trial_f)
            ok = n >= topk
            return jnp.where(ok, trial, base), jnp.where(ok, n, n_base)

        theta_key, n_ge = lax.fori_loop(
            0, 32, value_step, (jnp.full(shape, INT_MIN, I32), jnp.zeros(shape, I32) + total))
    else:
        def coarse_step(it, base):
            trial = base + lax.shift_left(jnp.int32(1), jnp.int32(15) - it)
            return jnp.where(count_bf16(_key16_to_bf16(trial)) >= topk, trial, base)

        k16 = lax.fori_loop(0, 16, coarse_step, jnp.full(shape, -2 ** 15, I32))
        k16 = jnp.maximum(k16, KEY16_NEG_INF)
        key32 = lambda k: lax.shift_left(k, 16) + jnp.where(k < 0, 0xFFFF, 0)
        lo0 = jnp.maximum(key32(k16 - 2), KEY_NEG_INF)
        hi0 = jnp.maximum(key32(k16 + 1), lo0 + 1)

        def fine_step(_, carry):
            lo, hi = carry
            mid = lo + lax.shift_right_arithmetic(hi - lo, 1)
            mid_f = _key_to_float(mid)
            ok = count(lambda s: s >= mid_f) >= topk
            return jnp.where(ok, mid, lo), jnp.where(ok, hi, mid)

        theta_key, _ = lax.fori_loop(0, 18, fine_step, (lo0, hi0))
        theta_f = _key_to_float(theta_key)
        n_ge = count(lambda s: s >= theta_f)
    theta = jnp.where(theta_key < KEY_NEG_INF, -jnp.inf, _key_to_float(theta_key))
    theta_ref[...] = theta
    has_surplus = jnp.max(n_ge) > topk

    @pl.when(has_surplus)
    def _():
        above = _key_to_float(jnp.maximum(theta_key, KEY_NEG_INF) + 1)
        cell_min, cell_max = cell_span(theta, above)

        @pl.when(jnp.max(jnp.where(cell_max > cell_min, 1, 0)) > 0)
        def _():
            def refine_step(_, carry):
                lo, hi = carry
                mid = lo + 0.5 * (hi - lo)
                ok = count(lambda s: s >= mid) >= topk
                return jnp.where(ok, mid, lo), jnp.where(ok, hi, mid)

            theta_ref[...] = lax.fori_loop(0, REFINE_STEPS, refine_step, (theta, above))[0]

        theta_fine = theta_ref[...]
        need_ref[...] = (topk - count(lambda s: s > theta_fine)).astype(F32)

    return has_surplus


def _strict_triangle(n, lower):
    r = lax.broadcasted_iota(I32, (n, n), 0)
    c = lax.broadcasted_iota(I32, (n, n), 1)
    return jnp.where((c < r) if lower else (r < c), 1.0, 0.0).astype(BF16)


def _attn_prompt_body(qit_ref, wt_ref, qt_ref, kidx_ref, k_ref, vt_ref, gn_ref, out_ref,
                      sc_ref, sb_ref, y_ref, theta_ref, need_ref, m_ref, l_ref, s_ref, mx_ref, *,
                      tq, ch, cha, topk, n_heads, head_dim, idx_heads, idx_dim):
    i = pl.program_id(1)
    n_keys = (i + 1) * tq
    qpos = i * tq + lax.broadcasted_iota(I32, (1, tq), 1)
    sub = LANES
    kpos_sub = lax.broadcasted_iota(I32, (sub, tq), 0)
    kpos_ch = lax.broadcasted_iota(I32, (ch, tq), 0)
    nc = (i + 1) * (tq // ch)

    def score_chunk(c, carry):
        for j in range(ch // sub):
            start = pl.multiple_of(c * ch + j * sub, sub)
            rows = pl.ds(start, sub)
            kc = kidx_ref[0, rows, :]
            acc = jnp.zeros((sub, tq), F32)
            for h in range(idx_heads):
                d = jnp.dot(kc, qit_ref[0, h * idx_dim:(h + 1) * idx_dim, :],
                            preferred_element_type=F32)
                acc = acc + jnp.maximum(d, 0.0) * wt_ref[0, h:h + 1, :]
            score = jnp.where(start + kpos_sub <= qpos, acc, -jnp.inf)
            sc_ref[rows, :] = score
            sb_ref[rows, :] = score.astype(BF16)
        return carry

    n_pairs = lax.div(nc, 2)
    lax.fori_loop(0, n_pairs, lambda j, c: score_chunk(2 * j + 1, score_chunk(2 * j, c)), 0)
    lax.fori_loop(2 * n_pairs, nc, score_chunk, 0)

    chunk_rows = lambda c: pl.ds(pl.multiple_of(c * ch, ch), ch)
    fold = lambda x, op: op(x.reshape(ch // SUBLANES, SUBLANES, tq), axis=0)

    def count_bf16(trial):
        packed = 2 * SUBLANES
        one, zero = jnp.ones((ch, tq), BF16), jnp.zeros((ch, tq), BF16)

        def body(c, acc):
            hit = jnp.where(sb_ref[chunk_rows(c), :] >= trial, one, zero)
            parts = hit.reshape(ch // packed, packed, tq)
            tot = parts[0]
            for r in range(1, ch // packed):
                tot = tot + parts[r]
            return acc + tot.astype(F32)
        acc = lax.fori_loop(0, n_pairs, lambda j, acc: body(2 * j + 1, body(2 * j, acc)),
                            jnp.zeros((packed, tq), F32))
        acc = lax.fori_loop(2 * n_pairs, nc, body, acc)
        return jnp.sum(acc, axis=0, keepdims=True).astype(I32)

    def count(pred):
        def body(c, acc):
            hit = jnp.where(pred(sc_ref[chunk_rows(c), :]), 1, 0).astype(I32)
            return acc + fold(hit, jnp.sum)
        acc = lax.fori_loop(0, n_pairs, lambda j, acc: body(2 * j + 1, body(2 * j, acc)),
                            jnp.zeros((SUBLANES, tq), I32))
        acc = lax.fori_loop(2 * n_pairs, nc, body, acc)
        return jnp.sum(acc, axis=0, keepdims=True)

    def cell_span(lo, hi):
        def body(c, carry):
            s = sc_ref[chunk_rows(c), :]
            inside = (s >= lo) & (s < hi)
            return (jnp.minimum(carry[0], fold(jnp.where(inside, s, jnp.inf), jnp.min)),
                    jnp.maximum(carry[1], fold(jnp.where(inside, s, -jnp.inf), jnp.max)))
        init = (jnp.full((SUBLANES, tq), jnp.inf, F32), jnp.full((SUBLANES, tq), -jnp.inf, F32))
        lo_acc, hi_acc = lax.fori_loop(0, nc, body, init)
        return (jnp.min(lo_acc, axis=0, keepdims=True), jnp.max(hi_acc, axis=0, keepdims=True))

    has_surplus = _threshold_search(count, cell_span, n_keys, topk, theta_ref, need_ref,
                                    count_bf16=count_bf16)
    theta = theta_ref[...]

    @pl.when(jnp.logical_not(has_surplus))
    def _():
        def bias_chunk(c, carry):
            rows = chunk_rows(c)
            sel = (sc_ref[rows, :] >= theta) & (c * ch + kpos_ch <= qpos)
            sc_ref[rows, :] = jnp.where(sel, 0.0, MASKED)
            return carry
        lax.fori_loop(0, nc, bias_chunk, 0)

    @pl.when(has_surplus)
    def _():
        need = need_ref[...]
        lower = _strict_triangle(ch, lower=True)

        def bias_chunk(c, seen):
            rows = chunk_rows(c)
            s = sc_ref[rows, :]
            tie = jnp.where(s == theta, 1.0, 0.0)
            rank = seen + jnp.dot(lower, tie.astype(BF16), preferred_element_type=F32)
            sel = ((s > theta) | ((s == theta) & (rank < need))) & (c * ch + kpos_ch <= qpos)
            sc_ref[rows, :] = jnp.where(sel, 0.0, MASKED)
            return seen + jnp.sum(fold(tie, jnp.sum), axis=0, keepdims=True)
        lax.fori_loop(0, nc, bias_chunk, jnp.zeros((1, tq), F32))

    heads = [(h, slice(h * head_dim, (h + 1) * head_dim)) for h in range(n_heads)]
    m_ref[...] = jnp.full(m_ref.shape, -jnp.inf, F32)
    l_ref[...] = jnp.zeros(l_ref.shape, F32)
    y_ref[...] = jnp.zeros(y_ref.shape, F32)

    nca = (i + 1) * (tq // cha)

    def logits(c, slot):
        rows = pl.ds(pl.multiple_of(c * cha, cha), cha)
        bias = sc_ref[rows, :]
        for h, hs in heads:
            s = jnp.dot(k_ref[0, h, rows, :], qt_ref[0, hs, :],
                        preferred_element_type=F32) + bias
            s_ref[slot, h] = s
            mx_ref[slot, h:h + 1, :] = jnp.max(s, axis=0, keepdims=True)

    def update(c, slot):
        for h, hs in heads:
            m = m_ref[h:h + 1, :]
            m_new = jnp.maximum(m, mx_ref[slot, h:h + 1, :])
            alpha = jnp.exp(m - m_new)
            p = jnp.exp(s_ref[slot, h] - m_new)
            m_ref[h:h + 1, :] = m_new
            l_ref[h:h + 1, :] = alpha * l_ref[h:h + 1, :] + jnp.sum(p, axis=0, keepdims=True)
            y_ref[hs, :] = alpha * y_ref[hs, :] + jnp.dot(
                vt_ref[0, c, hs, :], p.astype(BF16), preferred_element_type=F32)

    logits(0, 0)

    def chunk_pair(j, carry):
        c0 = 2 * j
        logits(c0 + 1, 1)
        update(c0, 0)
        logits(jnp.minimum(c0 + 2, nca - 1), 0)
        update(c0 + 1, 1)
        return carry

    lax.fori_loop(0, nca // 2, chunk_pair, 0)
    for h, hs in heads:
        y_ref[hs, :] = y_ref[hs, :] / l_ref[h:h + 1, :]
    out_ref[0] = _rms_gain(y_ref[...].T, gn_ref[...]).astype(BF16)


def _attn_prompt(qit, wt, qt, kidxb, k_hm, vt, gn_att, *, tq, topk, idx_dim):
    bsz, d_att, t = qt.shape
    n_heads, head_dim = k_hm.shape[1], k_hm.shape[3]
    ch = tq
    cha = vt.shape[3]
    assert (tq // cha) % 2 == 0
    cols = lambda a: pl.BlockSpec((1, a.shape[1], tq), lambda b, i: (b, 0, i))
    full = lambda a: pl.BlockSpec((1,) + a.shape[1:], lambda b, i: (b,) + (0,) * (a.ndim - 1))
    body = functools.partial(
        _attn_prompt_body, tq=tq, ch=ch, cha=cha, topk=topk, n_heads=n_heads,
        head_dim=head_dim, idx_heads=wt.shape[1], idx_dim=idx_dim)
    return pl.pallas_call(
        body,
        grid=(bsz, t // tq),
        in_specs=[cols(qit), cols(wt), cols(qt), full(kidxb), full(k_hm), full(vt),
                  pl.BlockSpec((1, d_att), lambda b, i: (0, 0))],
        out_specs=pl.BlockSpec((1, tq, d_att), lambda b, i: (b, i, 0)),
        out_shape=jax.ShapeDtypeStruct((bsz, t, d_att), BF16),
        scratch_shapes=[pltpu.VMEM((t, tq), F32), pltpu.VMEM((t, tq), BF16),
                        pltpu.VMEM((d_att, tq), F32),
                        pltpu.VMEM((1, tq), F32), pltpu.VMEM((1, tq), F32),
                        pltpu.VMEM((n_heads, tq), F32), pltpu.VMEM((n_heads, tq), F32),
                        pltpu.VMEM((2, n_heads, cha, tq), F32),
                        pltpu.VMEM((2, n_heads, tq), F32)],
        compiler_params=_params("parallel", "arbitrary"),
        name="attn_prompt",
    )(qit, wt, qt, kidxb, k_hm, vt, gn_att)


def _score_sample_body(pt_ref, qi_ref, w_ref, *refs, group, n_pages, page, t_len, idx_heads,
                       past_len):
    del pt_ref
    page_refs, new_ref, out_ref = refs[:group * n_pages], refs[-2], refs[-1]
    for g in range(group):
        qi = qi_ref[g]
        w = w_ref[g]
        pages = page_refs[g * n_pages:(g + 1) * n_pages]
        kt = jnp.concatenate([r[0].astype(BF16) for r in pages] + [new_ref[g]], axis=1)
        d = jnp.maximum(jnp.dot(qi, kt, preferred_element_type=F32), 0.0) * w
        s = jnp.sum(d.reshape(t_len, idx_heads, past_len + page), axis=1)
        tpos = past_len + lax.broadcasted_iota(I32, s.shape, 0)
        kpos = lax.broadcasted_iota(I32, s.shape, 1)
        out_ref[g] = jnp.where(kpos <= tpos, s, -jnp.inf)


def _select_sample_body(s_ref, bias_ref, sc_ref, theta_ref, need_ref, *, topk, page):
    nc, rows, _ = sc_ref.shape
    for c in range(nc):
        sc_ref[c] = s_ref[:, c * page:(c + 1) * page]

    def count(pred):
        def body(c, acc):
            return acc + jnp.where(pred(sc_ref[c]), 1, 0).astype(I32)
        acc = lax.fori_loop(0, nc, body, jnp.zeros((rows, page), I32))
        return jnp.sum(acc, axis=1, keepdims=True)

    def cell_span(lo, hi):
        def body(c, carry):
            s = sc_ref[c]
            inside = (s >= lo) & (s < hi)
            return (jnp.minimum(carry[0], jnp.where(inside, s, jnp.inf)),
                    jnp.maximum(carry[1], jnp.where(inside, s, -jnp.inf)))
        init = (jnp.full((rows, page), jnp.inf, F32), jnp.full((rows, page), -jnp.inf, F32))
        lo_acc, hi_acc = lax.fori_loop(0, nc, body, init)
        return (jnp.min(lo_acc, axis=1, keepdims=True), jnp.max(hi_acc, axis=1, keepdims=True))

    has_surplus = _threshold_search(count, cell_span, nc * page, topk, theta_ref, need_ref)
    theta = theta_ref[...]

    @pl.when(jnp.logical_not(has_surplus))
    def _():
        for c in range(nc):
            s = sc_ref[c]
            sel = (s >= theta) & (s > -jnp.inf)
            bias_ref[:, c * page:(c + 1) * page] = jnp.where(sel, 0.0, MASKED)

    @pl.when(has_surplus)
    def _():
        need = need_ref[...]
        upper = _strict_triangle(page, lower=False)
        seen = jnp.zeros((rows, 1), F32)
        for c in range(nc):
            s = sc_ref[c]
            tie = jnp.where(s == theta, 1.0, 0.0)
            rank = seen + jnp.dot(tie.astype(BF16), upper, preferred_element_type=F32)
            sel = ((s > theta) | ((s == theta) & (rank < need))) & (s > -jnp.inf)
            bias_ref[:, c * page:(c + 1) * page] = jnp.where(sel, 0.0, MASKED)
            seen = seen + jnp.sum(tie, axis=1, keepdims=True)


def _attn_sample_body(pt_ref, q_ref, bias_ref, gn_ref, *refs, n_pages, page, t_len, n_heads,
                      head_dim):
    del pt_ref
    k_refs, v_refs = refs[:n_pages], refs[n_pages:2 * n_pages]
    knew_ref, vnew_ref, out_ref = refs[2 * n_pages:]
    past = n_pages * page
    bias = bias_ref[0]
    outs = []
    del past
    logits = []
    for h in range(n_heads):
        kt = jnp.concatenate([r[0, h].astype(BF16) for r in k_refs] + [knew_ref[0, h]], axis=1)
        logits.append(jnp.dot(q_ref[0, h], kt, preferred_element_type=F32) + bias)
    for h in range(n_heads):
        s = logits[h]
        m = jnp.max(s, axis=1, keepdims=True)
        p = jnp.exp(s - m)
        l = jnp.sum(p, axis=1, keepdims=True)
        vt = jnp.concatenate([r[0, h].astype(BF16) for r in v_refs] + [vnew_ref[0, h]], axis=1)
        outs.append(_nt_dot(p.astype(BF16), vt) / l)
    y = jnp.concatenate(outs, axis=1)
    out_ref[0] = _rms_gain(y, gn_ref[...]).astype(BF16)


def _attn_sample(qib, wi, qb, kidx_new, k_new, v_new, cache_k, cache_v, cache_kidx, page_table,
                 gn_att, *, topk, n_heads):
    dbs, t_len, d_att = qb.shape
    n_pages = page_table.shape[1]
    page = cache_k.shape[1]
    idx_dim = cache_kidx.shape[2]
    idx_heads = wi.shape[2]
    head_dim = d_att // n_heads
    nkp = (n_pages + 1) * page
    past_len = n_pages * page
    pad = lambda a: jnp.pad(a, ((0, 0), (0, page - t_len), (0, 0)))
    kidx_t = jnp.transpose(cache_kidx, (0, 2, 1))
    k_t = jnp.transpose(cache_k, (0, 2, 3, 1))
    v_t = jnp.transpose(cache_v, (0, 2, 3, 1))
    new_t = lambda a: jnp.transpose(pad(a).reshape(dbs, page, n_heads, head_dim), (0, 2, 3, 1))

    def paged(shape):
        return [pl.BlockSpec((1,) + shape, functools.partial(
            lambda b, pt, p: (pt[b, p],) + (0,) * len(shape), p=p)) for p in range(n_pages)]

    per_seq = lambda shape: pl.BlockSpec((1,) + shape, lambda b, pt: (b,) + (0,) * len(shape))

    group = _row_tile(dbs, 4)
    seq_group = lambda shape: pl.BlockSpec((group,) + shape,
                                           lambda b, pt: (b,) + (0,) * len(shape))
    group_pages = [pl.BlockSpec((1, idx_dim, page), functools.partial(
        lambda b, pt, g, p: (pt[b * group + g, p], 0, 0), g=g, p=p))
        for g in range(group) for p in range(n_pages)]
    scores = pl.pallas_call(
        functools.partial(_score_sample_body, group=group, n_pages=n_pages, page=page,
                          t_len=t_len, idx_heads=idx_heads, past_len=past_len),
        grid_spec=pltpu.PrefetchScalarGridSpec(
            num_scalar_prefetch=1, grid=(dbs // group,),
            in_specs=[seq_group((t_len * idx_heads, idx_dim)), seq_group((t_len * idx_heads, 1))]
                     + group_pages + [seq_group((idx_dim, page))],
            out_specs=seq_group((t_len, nkp))),
        out_shape=jax.ShapeDtypeStruct((dbs, t_len, nkp), F32),
        compiler_params=_params("parallel"),
        name="score_sample",
    )(page_table, qib.reshape(dbs, t_len * idx_heads, idx_dim),
      wi.reshape(dbs, t_len * idx_heads, 1), *([kidx_t] * (group * n_pages)),
      jnp.transpose(pad(kidx_new), (0, 2, 1)))

    rows = dbs * t_len
    tr = min(rows, 128)
    bias = pl.pallas_call(
        functools.partial(_select_sample_body, topk=topk, page=page),
        grid=(rows // tr,),
        in_specs=[pl.BlockSpec((tr, nkp), lambda r: (r, 0))],
        out_specs=pl.BlockSpec((tr, nkp), lambda r: (r, 0)),
        out_shape=jax.ShapeDtypeStruct((rows, nkp), F32),
        scratch_shapes=[pltpu.VMEM((n_pages + 1, tr, page), F32), pltpu.VMEM((tr, 1), F32),
                        pltpu.VMEM((tr, 1), F32)],
        compiler_params=_params("parallel"),
        name="select_sample",
    )(scores.reshape(rows, nkp)).reshape(dbs, t_len, nkp)

    kv_pages = paged((n_heads, head_dim, page))
    return pl.pallas_call(
        functools.partial(_attn_sample_body, n_pages=n_pages, page=page, t_len=t_len,
                          n_heads=n_heads, head_dim=head_dim),
        grid_spec=pltpu.PrefetchScalarGridSpec(
            num_scalar_prefetch=1, grid=(dbs,),
            in_specs=[per_seq((n_heads, t_len, head_dim)), per_seq((t_len, nkp)),
                      pl.BlockSpec((1, d_att), lambda b, pt: (0, 0))]
                     + kv_pages + kv_pages
                     + [per_seq((n_heads, head_dim, page)), per_seq((n_heads, head_dim, page))],
            out_specs=per_seq((t_len, d_att))),
        out_shape=jax.ShapeDtypeStruct((dbs, t_len, d_att), BF16),
        compiler_params=_params("parallel"),
        name="attn_sample",
    )(page_table, qb.reshape(dbs, t_len, n_heads, head_dim).transpose(0, 2, 1, 3), bias, gn_att,
      *([k_t] * n_pages), *([v_t] * n_pages), new_t(k_new), new_t(v_new))


def _layer_norm(x, g, b):
    mu = jnp.mean(x, axis=-1, keepdims=True)
    xc = x - mu
    var = jnp.mean(xc * xc, axis=-1, keepdims=True)
    return xc * lax.rsqrt(var + LN_EPS) * g + b


def _finish_body(x_ref, ml_ref, ma_ref, wo_ref, wfi_ref, wfo_ref, p_ref, out_ref,
                 *, alpha, d_lru, d_ff, fc):
    ln1_g, ln1_b, ln2_g, ln2_b = (p_ref[r:r + 1, :] for r in range(4))
    y = (jnp.dot(ml_ref[...], wo_ref[0:d_lru, :], preferred_element_type=F32)
         + jnp.dot(ma_ref[...], wo_ref[d_lru:, :], preferred_element_type=F32))
    x1 = _layer_norm(alpha * x_ref[...] + y, ln1_g, ln1_b)
    x1b = x1.astype(BF16)
    f = jnp.zeros(x1.shape, F32)
    for c in range(d_ff // fc):
        u = jnp.dot(x1b, wfi_ref[:, c * fc:(c + 1) * fc], preferred_element_type=F32)
        g = jnp.dot(x1b, wfi_ref[:, d_ff + c * fc:d_ff + (c + 1) * fc],
                    preferred_element_type=F32)
        hidden = (g * jax.nn.sigmoid(g) * u).astype(BF16)
        f = f + jnp.dot(hidden, wfo_ref[c * fc:(c + 1) * fc, :], preferred_element_type=F32)
    out_ref[...] = _layer_norm(alpha * x1 + f, ln2_g, ln2_b)


def _finish(x2d, mix_lru, mix_att, wo, wfi, wfo, pvec, *, alpha, tm):
    n, d_model = x2d.shape
    d_lru = mix_lru.shape[1]
    d_ff = wfo.shape[0]
    fc = 2 * LANES if d_ff % (2 * LANES) == 0 else LANES
    row = lambda w: pl.BlockSpec((tm, w), lambda i: (i, 0))
    const = lambda a: pl.BlockSpec(a.shape, lambda i: (0, 0), pipeline_mode=pl.Buffered(1))
    return pl.pallas_call(
        functools.partial(_finish_body, alpha=alpha, d_lru=d_lru, d_ff=d_ff, fc=fc),
        grid=(n // tm,),
        in_specs=[row(d_model), row(d_lru), row(mix_att.shape[1]),
                  const(wo), const(wfi), const(wfo), const(pvec)],
        out_specs=row(d_model),
        out_shape=jax.ShapeDtypeStruct((n, d_model), F32),
        compiler_params=_params("parallel"),
        name="finish",
    )(x2d, mix_lru, mix_att, wo, wfi, wfo, pvec)


def _block_diag(w):
    nb, bi, bo = w.shape
    eye = jnp.eye(nb, dtype=w.dtype)
    return (w[:, :, None, :] * eye[:, None, :, None]).reshape(nb * bi, nb * bo)


def _row_tile(n, want):
    tm = min(n, want)
    while n % tm:
        tm //= 2
    return tm


def kernel(x_prompt, x_sample, cache_k, cache_v, cache_kidx, state_conv, state_h, page_table,
           w_in, conv_w, conv_b, w_a, b_a, w_x, b_x, lam, gn_lru, gn_att, w_out,
           ln1_g, ln1_b, w_ffn_in, w_ffn_out, ln2_g, ln2_b):
    depth, d_model, d_in = w_in.shape
    bsz, seq, _ = x_prompt.shape
    dbs, dseq, _ = x_sample.shape
    d_lru = conv_w.shape[2]
    n_phys, page, n_heads, head_dim = cache_k.shape[1:]
    d_att = n_heads * head_dim
    idx_dim = cache_kidx.shape[3]
    d_qi = IDX_HEADS * idx_dim
    n_pages = page_table.shape[1]
    past_len = n_pages * page
    alpha = (2.0 * depth) ** 0.25
    assert d_in == 2 * d_lru + 3 * d_att + d_qi + idx_dim + IDX_HEADS
    assert idx_dim + IDX_HEADS <= LANES and LANES % head_dim == 0 and head_dim == idx_dim
    geom = dict(d_lru=d_lru, d_att=d_att, d_qi=d_qi, idx_dim=idx_dim, head_dim=head_dim)
    d_main = d_in - idx_dim - IDX_HEADS

    pos_p = jnp.tile(jnp.arange(seq), bsz)
    pos_s = jnp.tile(past_len + jnp.arange(dseq), dbs)
    topk_p = min(TOPK_MAX, seq // 4)
    topk_s = min(TOPK_MAX, (past_len + dseq) // 4)

    xp = x_prompt.reshape(bsz * seq, d_model)
    xs = x_sample.reshape(dbs * dseq, d_model)
    outs_p, outs_s = [], []
    for l in range(depth):
        w_pad = jnp.pad(w_in[l], ((0, 0), (0, d_main + LANES - d_in))).astype(BF16)
        wa_bd = _block_diag(w_a[l]).astype(BF16)
        wx_bd = _block_diag(w_x[l]).astype(BF16)
        lru_vec = jnp.stack([conv_b[l], b_a[l], b_x[l], lam[l], gn_lru[l]]
                            + [jnp.zeros_like(lam[l])] * 3)
        fin_vec = jnp.stack([ln1_g[l], ln1_b[l], ln2_g[l], ln2_b[l]])
        wo, wfi, wfo = (w_out[l].astype(BF16), w_ffn_in[l].astype(BF16),
                        w_ffn_out[l].astype(BF16))
        gn_a = gn_att[l][None, :]

        tq = _row_tile(seq, 256)
        xl, gate, qt, qit, kt, k_hm, vt, vt_chunks, kidx_t, kidx_b, wt = _project_cols(
            xp.reshape(bsz, seq, d_model), w_pad, tm=_row_tile(seq, 512), cha=tq // 2,
            n_heads=n_heads, idx_heads=IDX_HEADS, **geom)
        xl3 = xl.reshape(bsz, seq, d_lru)
        mix_l, h_last = _lru_prompt(
            xl3, gate.reshape(bsz, seq, d_lru), jnp.zeros((bsz, CONV_W - 1, d_lru), F32),
            jnp.zeros((bsz, d_lru), F32), conv_w[l], lru_vec, wa_bd, wx_bd, tt=_row_tile(seq, 256))
        mix_a = _attn_prompt(qit, wt, qt, kidx_b, k_hm, vt_chunks, gn_a, tq=tq, topk=topk_p,
                             idx_dim=idx_dim)
        xp = _finish(xp, mix_l.reshape(bsz * seq, d_lru), mix_a.reshape(bsz * seq, d_att),
                     wo, wfi, wfo, fin_vec, alpha=alpha, tm=_row_tile(bsz * seq, 512))
        token_major = lambda a: a.reshape(bsz, n_heads, head_dim, seq).transpose(0, 3, 1, 2)
        outs_p.append((token_major(kt), token_major(vt), jnp.swapaxes(kidx_t, 1, 2),
                       xl3[:, seq - (CONV_W - 1):], h_last[:, 0]))

        xl, gate, qb, k, kb, v, vb, qib, tail = _project_rows(
            xs, w_pad, pos_s, tm=_row_tile(dbs * dseq, 256), **geom)
        d3 = lambda a: a.reshape(dbs, dseq, a.shape[-1])
        tm_major = lambda a: jnp.swapaxes(d3(a), 0, 1)
        xl3 = d3(xl)
        mix_l, h_last = _lru_sample(
            tm_major(xl), tm_major(gate), jnp.swapaxes(state_conv[l], 0, 1).astype(F32),
            state_h[l], conv_w[l], lru_vec, wa_bd, wx_bd)
        tail3 = d3(tail)
        ki = tail3[:, :, :idx_dim]
        wi = tail3[:, :, idx_dim:idx_dim + IDX_HEADS]
        mix_a = _attn_sample(
            d3(qib), wi, d3(qb), ki.astype(BF16), d3(kb), d3(vb),
            cache_k[l], cache_v[l], cache_kidx[l], page_table, gn_a, topk=topk_s,
            n_heads=n_heads)
        xs = _finish(xs, jnp.swapaxes(mix_l, 0, 1).reshape(dbs * dseq, d_lru),
                     mix_a.reshape(dbs * dseq, d_att), wo, wfi, wfo, fin_vec, alpha=alpha,
                     tm=_row_tile(dbs * dseq, 256))
        conv_new = jnp.concatenate([state_conv[l].astype(F32), xl3], axis=1)[:, -(CONV_W - 1):]
        outs_s.append((k.reshape(dbs, dseq, n_heads, head_dim), v.reshape(dbs, dseq, n_heads, head_dim),
                       ki, conv_new, h_last))

    stack = lambda outs, j: jnp.stack([o[j] for o in outs])
    return (xp.reshape(bsz, seq, d_model), xs.reshape(dbs, dseq, d_model),
            *(stack(outs_p, j) for j in range(5)), *(stack(outs_s, j) for j in range(5)))
```

```python
import functools

import jax
import jax.numpy as jnp
import numpy as np
from jax import lax
from jax.experimental import pallas as pl
from jax.experimental.pallas import tpu as pltpu

CONV_W = 4
LRU_C = 8.0
IDX_HEADS = 8
TOPK_MAX = 256
ROPE_FRACTION = 4
ROPE_THETA = 500000.0
RMS_EPS = 1e-6
LN_EPS = 1e-5

LANES = 128
SUBLANES = 8
VMEM_LIMIT = 56 * 1024 * 1024
MASKED = -1e30
INT_MIN = -2 ** 31

PROJ_ROWS = 512
PROJ_ROWS_SAMPLE = 256
FINISH_ROWS = 512
FINISH_ROWS_SAMPLE = 256
LRU_ROWS = 256
ATTN_QUERIES = 256
SCORE_GROUP = 4
SELECT_ROWS = 128

F32 = jnp.float32
BF16 = jnp.bfloat16
I32 = jnp.int32


def _params(*sem):
    return pltpu.CompilerParams(dimension_semantics=sem, vmem_limit_bytes=VMEM_LIMIT)


def _nt_dot(a, b):
    return lax.dot_general(a, b, (((1,), (1,)), ((), ())), preferred_element_type=F32)


def _proj_streams(x_ref, w_ref, c_ref, s1_ref, s2_ref, *, d_lru, d_att, d_qi, idx_dim, rope_half,
                  q_scale):
    xb = x_ref[...].astype(BF16)
    c, s1, s2 = c_ref[...], s1_ref[...], s2_ref[...]

    def proj(lo, width):
        return jnp.dot(xb, w_ref[:, lo:lo + width], preferred_element_type=F32)

    def tiled(t, width):
        reps = width // LANES
        return t if reps == 1 else jnp.concatenate([t] * reps, axis=1)

    def rope(z, cc, ss1, ss2):
        width = z.shape[1]
        return (z * cc + pltpu.roll(z, width - rope_half, 1) * ss1
                + pltpu.roll(z, rope_half, 1) * ss2)

    lo = 0
    xl = proj(lo, d_lru); lo += d_lru
    gate = proj(lo, d_lru); lo += d_lru
    q = rope(proj(lo, d_att), tiled(c, d_att), tiled(s1, d_att), tiled(s2, d_att)); lo += d_att
    k = rope(proj(lo, d_att), tiled(c, d_att), tiled(s1, d_att), tiled(s2, d_att)); lo += d_att
    v = proj(lo, d_att); lo += d_att
    qi = rope(proj(lo, d_qi), tiled(c, d_qi), tiled(s1, d_qi), tiled(s2, d_qi)); lo += d_qi
    tail = proj(lo, LANES)
    is_key = lax.broadcasted_iota(I32, tail.shape, 1) < idx_dim
    tail = rope(tail, jnp.where(is_key, c, 1.0), jnp.where(is_key, s1, 0.0),
                jnp.where(is_key, s2, 0.0))
    return xl, gate, q * q_scale, k, v, qi, tail


def _proj_rows_body(x_ref, w_ref, c_ref, s1_ref, s2_ref,
                    xl_ref, gate_ref, qb_ref, k_ref, kb_ref, v_ref, vb_ref, qib_ref, tail_ref,
                    **geom):
    xl, gate, q, k, v, qi, tail = _proj_streams(x_ref, w_ref, c_ref, s1_ref, s2_ref, **geom)
    xl_ref[...] = xl
    gate_ref[...] = gate
    qb_ref[...] = q.astype(BF16)
    k_ref[...] = k
    kb_ref[...] = k.astype(BF16)
    v_ref[...] = v
    vb_ref[...] = v.astype(BF16)
    qib_ref[...] = qi.astype(BF16)
    tail_ref[...] = tail


def _proj_cols_body(x_ref, w_ref, c_ref, s1_ref, s2_ref,
                    xl_ref, gate_ref, qt_ref, qit_ref, kt_ref, khm_ref, vt_ref, vtc_ref,
                    kidxt_ref, kidx_ref, wt_ref, *, cha, n_heads, idx_heads, **geom):
    xl, gate, q, k, v, qi, tail = _proj_streams(x_ref, w_ref, c_ref, s1_ref, s2_ref, **geom)
    idx_dim = geom["idx_dim"]
    head_dim = k.shape[1] // n_heads
    xl_ref[...] = xl
    gate_ref[...] = gate
    qt_ref[0] = q.T.astype(BF16)
    qit_ref[0] = qi.T.astype(BF16)
    kt_ref[0] = k.T
    for h in range(n_heads):
        khm_ref[0, h] = k[:, h * head_dim:(h + 1) * head_dim].astype(BF16)
    vt = v.T
    vt_ref[0] = vt
    for j in range(vt.shape[1] // cha):
        vtc_ref[0, j] = vt[:, j * cha:(j + 1) * cha].astype(BF16)
    tail_t = tail.T
    kidxt_ref[0] = tail_t[:idx_dim]
    wt_ref[0] = tail_t[idx_dim:idx_dim + idx_heads]
    kidx_ref[0] = tail[:, :idx_dim].astype(BF16)


def _rope_tables(pos, head_dim):
    rope_dim = head_dim // ROPE_FRACTION
    half = rope_dim // 2
    freqs = ROPE_THETA ** (-jnp.arange(half, dtype=F32) / half)
    dim = jnp.arange(LANES) % head_dim
    ang = pos.astype(F32)[:, None] * freqs[dim % half][None, :]
    cos, sin = jnp.cos(ang), jnp.sin(ang)
    c = jnp.where(dim < rope_dim, cos, 1.0)
    s1 = jnp.where(dim < half, -sin, 0.0)
    s2 = jnp.where((dim >= half) & (dim < rope_dim), sin, 0.0)
    return [c, s1, s2], half


def _project_rows(x2d, w_pad, pos, *, d_lru, d_att, d_qi, idx_dim, head_dim, tm):
    n, d_model = x2d.shape
    (c, s1, s2), half = _rope_tables(pos, head_dim)
    row = lambda w: pl.BlockSpec((tm, w), lambda i: (i, 0))
    outs = [
        (d_lru, F32), (d_lru, F32), (d_att, BF16), (d_att, F32), (d_att, BF16),
        (d_att, F32), (d_att, BF16), (d_qi, BF16), (LANES, F32)]
    body = functools.partial(_proj_rows_body, d_lru=d_lru, d_att=d_att, d_qi=d_qi,
                             idx_dim=idx_dim, rope_half=half, q_scale=head_dim ** -0.5)
    return pl.pallas_call(
        body,
        grid=(n // tm,),
        in_specs=[row(d_model), pl.BlockSpec(w_pad.shape, lambda i: (0, 0)),
                  row(LANES), row(LANES), row(LANES)],
        out_specs=[row(w) for w, _ in outs],
        out_shape=[jax.ShapeDtypeStruct((n, w), dt) for w, dt in outs],
        compiler_params=_params("parallel"),
        name="proj",
    )(x2d, w_pad, c, s1, s2)


def _project_cols(x3d, w_pad, *, d_lru, d_att, d_qi, idx_dim, head_dim, tm, cha, n_heads,
                  idx_heads):
    bsz, t, d_model = x3d.shape
    n, nt = bsz * t, t // tm
    (c, s1, s2), half = _rope_tables(jnp.arange(t), head_dim)
    row = lambda w: pl.BlockSpec((tm, w), lambda i: (i, 0))
    table = pl.BlockSpec((tm, LANES), lambda i: (i % nt, 0))
    col = lambda w: pl.BlockSpec((1, w, tm), lambda i: (i // nt, 0, i % nt))
    col_shape = lambda w, dt: jax.ShapeDtypeStruct((bsz, w, t), dt)
    body = functools.partial(
        _proj_cols_body, d_lru=d_lru, d_att=d_att, d_qi=d_qi, idx_dim=idx_dim, rope_half=half,
        q_scale=head_dim ** -0.5, cha=cha, n_heads=n_heads, idx_heads=idx_heads)
    return pl.pallas_call(
        body,
        grid=(n // tm,),
        in_specs=[row(d_model), pl.BlockSpec(w_pad.shape, lambda i: (0, 0)), table, table, table],
        out_specs=[
            row(d_lru), row(d_lru), col(d_att), col(d_qi), col(d_att),
            pl.BlockSpec((1, n_heads, tm, head_dim), lambda i: (i // nt, 0, i % nt, 0)),
            col(d_att),
            pl.BlockSpec((1, tm // cha, d_att, cha), lambda i: (i // nt, i % nt, 0, 0)),
            col(idx_dim), pl.BlockSpec((1, tm, idx_dim), lambda i: (i // nt, i % nt, 0)),
            col(idx_heads)],
        out_shape=[
            jax.ShapeDtypeStruct((n, d_lru), F32), jax.ShapeDtypeStruct((n, d_lru), F32),
            col_shape(d_att, BF16), col_shape(d_qi, BF16), col_shape(d_att, F32),
            jax.ShapeDtypeStruct((bsz, n_heads, t, head_dim), BF16),
            col_shape(d_att, F32),
            jax.ShapeDtypeStruct((bsz, t // cha, d_att, cha), BF16),
            col_shape(idx_dim, F32), jax.ShapeDtypeStruct((bsz, t, idx_dim), BF16),
            col_shape(idx_heads, F32)],
        compiler_params=_params("parallel"),
        name="proj_prompt",
    )(x3d.reshape(n, d_model), w_pad, c, s1, s2)


def _softplus(x):
    return jnp.maximum(x, 0.0) + jnp.log1p(jnp.exp(-jnp.abs(x)))


def _gelu_tanh(x):
    return 0.5 * x * (1.0 + jnp.tanh(np.sqrt(2.0 / np.pi).astype(np.float32)
                                     * (x + 0.044715 * (x * x * x))))


def _lru_gates(xc, wa_ref, wx_ref, b_a, b_x, lam):
    xcb = xc.astype(BF16)
    r = jax.nn.sigmoid(jnp.dot(xcb, wa_ref[...], preferred_element_type=F32) + b_a)
    i = jax.nn.sigmoid(jnp.dot(xcb, wx_ref[...], preferred_element_type=F32) + b_x)
    log_a = -LRU_C * r * _softplus(-lam)
    a = jnp.exp(log_a)
    t = jnp.tanh(log_a)
    b = jnp.sqrt(-2.0 * t / (1.0 - t)) * (i * xc)
    return a, b


def _rms_gain(y, g):
    return y * lax.rsqrt(jnp.mean(y * y, axis=-1, keepdims=True) + RMS_EPS) * g


def _lru_prompt_body(xl_ref, gate_ref, cprev_ref, h0_ref, cw_ref, p_ref, wa_ref, wx_ref,
                     mix_ref, hlast_ref, ext_ref, hc_ref, *, tt):
    j = pl.program_id(1)

    @pl.when(j == 0)
    def _():
        ext_ref[0:SUBLANES, :] = cprev_ref[0]
        hc_ref[0:1, :] = h0_ref[0]

    xl = xl_ref[0]
    ext_ref[SUBLANES:SUBLANES + tt, :] = xl
    conv_b, b_a, b_x, lam, gn = (p_ref[r:r + 1, :] for r in range(5))
    xc = conv_b + (cw_ref[0:1, :] * ext_ref[SUBLANES - 3:SUBLANES - 3 + tt, :]
                   + cw_ref[1:2, :] * ext_ref[SUBLANES - 2:SUBLANES - 2 + tt, :]
                   + cw_ref[2:3, :] * ext_ref[SUBLANES - 1:SUBLANES - 1 + tt, :]
                   + cw_ref[3:4, :] * xl)
    ext_ref[0:SUBLANES, :] = ext_ref[tt:tt + SUBLANES, :]

    a, b = _lru_gates(xc, wa_ref, wx_ref, b_a, b_x, lam)
    groups = tt // SUBLANES
    a = a.reshape(groups, SUBLANES, a.shape[1])
    b = b.reshape(groups, SUBLANES, b.shape[1])
    row = lax.broadcasted_iota(I32, a.shape, 1)
    d = 1
    while d < SUBLANES:
        keep = row >= d
        a_prev = jnp.where(keep, pltpu.roll(a, d, 1), 1.0)
        b_prev = jnp.where(keep, pltpu.roll(b, d, 1), 0.0)
        b = a * b_prev + b
        a = a * a_prev
        d *= 2
    state = hc_ref[0:1, :]
    hs = []
    for g in range(groups):
        hs.append(a[g] * state + b[g])
        state = hs[-1][SUBLANES - 1:SUBLANES, :]
    h = jnp.concatenate(hs, axis=0)
    hc_ref[0:1, :] = state
    hlast_ref[0] = state
    mix_ref[0] = _rms_gain(h * _gelu_tanh(gate_ref[0]), gn).astype(BF16)


def _lru_prompt(xl, gate, conv_prev, h0, conv_w, pvec, wa_bd, wx_bd, *, tt):
    bsz, t, d = xl.shape
    cprev8 = jnp.concatenate(
        [jnp.zeros((bsz, SUBLANES - (CONV_W - 1), d), F32), conv_prev.astype(F32)], axis=1)
    const = lambda shape: pl.BlockSpec(shape, lambda b, j: (0,) * len(shape))
    return pl.pallas_call(
        functools.partial(_lru_prompt_body, tt=tt),
        grid=(bsz, t // tt),
        in_specs=[pl.BlockSpec((1, tt, d), lambda b, j: (b, j, 0)),
                  pl.BlockSpec((1, tt, d), lambda b, j: (b, j, 0)),
                  pl.BlockSpec((1, SUBLANES, d), lambda b, j: (b, 0, 0)),
                  pl.BlockSpec((1, 1, d), lambda b, j: (b, 0, 0)),
                  const(conv_w.shape), const(pvec.shape), const(wa_bd.shape), const(wx_bd.shape)],
        out_specs=[pl.BlockSpec((1, tt, d), lambda b, j: (b, j, 0)),
                   pl.BlockSpec((1, 1, d), lambda b, j: (b, 0, 0))],
        out_shape=[jax.ShapeDtypeStruct((bsz, t, d), BF16),
                   jax.ShapeDtypeStruct((bsz, 1, d), F32)],
        scratch_shapes=[pltpu.VMEM((tt + SUBLANES, d), F32), pltpu.VMEM((SUBLANES, d), F32)],
        compiler_params=_params("parallel", "arbitrary"),
        name="lru_prompt",
    )(xl, gate, cprev8, h0.astype(F32)[:, None, :], conv_w, pvec, wa_bd, wx_bd)


def _lru_sample_body(xl_ref, gate_ref, cprev_ref, h0_ref, cw_ref, p_ref, wa_ref, wx_ref,
                     mix_ref, hlast_ref, *, t_len):
    conv_b, b_a, b_x, lam, gn = (p_ref[r:r + 1, :] for r in range(5))
    xp = [cprev_ref[s] for s in range(CONV_W - 1)] + [xl_ref[s] for s in range(t_len)]
    h = h0_ref[...]
    for s in range(t_len):
        xc = conv_b + (cw_ref[0:1, :] * xp[s] + cw_ref[1:2, :] * xp[s + 1]
                       + cw_ref[2:3, :] * xp[s + 2] + cw_ref[3:4, :] * xp[s + 3])
        a, b = _lru_gates(xc, wa_ref, wx_ref, b_a, b_x, lam)
        h = a * h + b
        mix_ref[s] = _rms_gain(h * _gelu_tanh(gate_ref[s]), gn).astype(BF16)
    hlast_ref[...] = h


def _lru_sample(xl_t, gate_t, cprev_t, h0, conv_w, pvec, wa_bd, wx_bd):
    t_len, dbs, d = xl_t.shape
    return pl.pallas_call(
        functools.partial(_lru_sample_body, t_len=t_len),
        out_shape=[jax.ShapeDtypeStruct((t_len, dbs, d), BF16),
                   jax.ShapeDtypeStruct((dbs, d), F32)],
        compiler_params=pltpu.CompilerParams(vmem_limit_bytes=VMEM_LIMIT),
        name="lru_sample",
    )(xl_t, gate_t, cprev_t, h0.astype(F32), conv_w, pvec, wa_bd, wx_bd)


KEY_NEG_INF = INT_MIN + 0x7FFFFF
REFINE_STEPS = 8


def _key_to_float(key):
    return pltpu.bitcast(jnp.where(key >= 0, key, key ^ 0x7FFFFFFF), F32)


KEY16_NEG_INF = -2 ** 15 + 0x7F


def _key16_to_bf16(key):
    bits = lax.shift_left(jnp.where(key >= 0, key, key ^ 0x7FFF), 16)
    return pltpu.bitcast(bits, F32).astype(BF16)


def _threshold_search(count, cell_span, total, topk, theta_ref, need_ref, count_bf16=None):
    shape = theta_ref.shape

    if count_bf16 is None:
        def value_step(it, carry):
            base, n_base = carry
            trial = base ^ lax.shift_left(jnp.int32(1), jnp.int32(31) - it)
            trial_f = _key_to_float(trial)
            n = count(lambda s: s >= trial_f)
            ok = n >= topk
            return jnp.where(ok, trial, base), jnp.where(ok, n, n_base)

        theta_key, n_ge = lax.fori_loop(
            0, 32, value_step, (jnp.full(shape, INT_MIN, I32), jnp.zeros(shape, I32) + total))
    else:
        def coarse_step(it, base):
            trial = base + lax.shift_left(jnp.int32(1), jnp.int32(15) - it)
            return jnp.where(count_bf16(_key16_to_bf16(trial)) >= topk, trial, base)

        k16 = lax.fori_loop(0, 16, coarse_step, jnp.full(shape, -2 ** 15, I32))
        k16 = jnp.maximum(k16, KEY16_NEG_INF)
        key32 = lambda k: lax.shift_left(k, 16) + jnp.where(k < 0, 0xFFFF, 0)
        lo0 = jnp.maximum(key32(k16 - 2), KEY_NEG_INF)
        hi0 = jnp.maximum(key32(k16 + 1), lo0 + 1)

        def fine_step(_, carry):
            lo, hi = carry
            mid = lo + lax.shift_right_arithmetic(hi - lo, 1)
            mid_f = _key_to_float(mid)
            ok = count(lambda s: s >= mid_f) >= topk
            return jnp.where(ok, mid, lo), jnp.where(ok, hi, mid)

        theta_key, _ = lax.fori_loop(0, 18, fine_step, (lo0, hi0))
        theta_f = _key_to_float(theta_key)
        n_ge = count(lambda s: s >= theta_f)
    theta = jnp.where(theta_key < KEY_NEG_INF, -jnp.inf, _key_to_float(theta_key))
    theta_ref[...] = theta
    has_surplus = jnp.max(n_ge) > topk

    @pl.when(has_surplus)
    def _():
        above = _key_to_float(jnp.maximum(theta_key, KEY_NEG_INF) + 1)
        cell_min, cell_max = cell_span(theta, above)

        @pl.when(jnp.max(jnp.where(cell_max > cell_min, 1, 0)) > 0)
        def _():
            def refine_step(_, carry):
                lo, hi = carry
                mid = lo + 0.5 * (hi - lo)
                ok = count(lambda s: s >= mid) >= topk
                return jnp.where(ok, mid, lo), jnp.where(ok, hi, mid)

            theta_ref[...] = lax.fori_loop(0, REFINE_STEPS, refine_step, (theta, above))[0]

        theta_fine = theta_ref[...]
        need_ref[...] = (topk - count(lambda s: s > theta_fine)).astype(F32)

    return has_surplus


def _strict_triangle(n, lower):
    r = lax.broadcasted_iota(I32, (n, n), 0)
    c = lax.broadcasted_iota(I32, (n, n), 1)
    return jnp.where((c < r) if lower else (r < c), 1.0, 0.0).astype(BF16)


def _attn_prompt_body(qit_ref, wt_ref, qt_ref, kidx_ref, k_ref, vt_ref, gn_ref, out_ref,
                      sc_ref, sb_ref, y_ref, theta_ref, need_ref, m_ref, l_ref, s_ref, mx_ref, *,
                      tq, ch, cha, topk, n_heads, head_dim, idx_heads, idx_dim):
    i = pl.program_id(1)
    n_keys = (i + 1) * tq
    qpos = i * tq + lax.broadcasted_iota(I32, (1, tq), 1)
    sub = LANES
    kpos_sub = lax.broadcasted_iota(I32, (sub, tq), 0)
    kpos_ch = lax.broadcasted_iota(I32, (ch, tq), 0)
    nc = (i + 1) * (tq // ch)

    def score_chunk(c, carry):
        for j in range(ch // sub):
            start = pl.multiple_of(c * ch + j * sub, sub)
            rows = pl.ds(start, sub)
            kc = kidx_ref[0, rows, :]
            acc = jnp.zeros((sub, tq), F32)
            for h in range(idx_heads):
                d = jnp.dot(kc, qit_ref[0, h * idx_dim:(h + 1) * idx_dim, :],
                            preferred_element_type=F32)
                acc = acc + jnp.maximum(d, 0.0) * wt_ref[0, h:h + 1, :]
            score = jnp.where(start + kpos_sub <= qpos, acc, -jnp.inf)
            sc_ref[rows, :] = score
            sb_ref[rows, :] = score.astype(BF16)
        return carry

    n_pairs = lax.div(nc, 2)

    def over_chunks(body, init):
        carry = lax.fori_loop(0, n_pairs, lambda j, x: body(2 * j + 1, body(2 * j, x)), init)
        return lax.fori_loop(2 * n_pairs, nc, body, carry)

    over_chunks(score_chunk, 0)

    chunk_rows = lambda c: pl.ds(pl.multiple_of(c * ch, ch), ch)
    fold = lambda x, op: op(x.reshape(ch // SUBLANES, SUBLANES, tq), axis=0)

    def count_bf16(trial):
        packed = 2 * SUBLANES
        one, zero = jnp.ones((ch, tq), BF16), jnp.zeros((ch, tq), BF16)

        def body(c, acc):
            hit = jnp.where(sb_ref[chunk_rows(c), :] >= trial, one, zero)
            parts = hit.reshape(ch // packed, packed, tq)
            tot = parts[0]
            for r in range(1, ch // packed):
                tot = tot + parts[r]
            return acc + tot.astype(F32)
        acc = over_chunks(body, jnp.zeros((packed, tq), F32))
        return jnp.sum(acc, axis=0, keepdims=True).astype(I32)

    def count(pred):
        def body(c, acc):
            hit = jnp.where(pred(sc_ref[chunk_rows(c), :]), 1, 0).astype(I32)
            return acc + fold(hit, jnp.sum)
        acc = over_chunks(body, jnp.zeros((SUBLANES, tq), I32))
        return jnp.sum(acc, axis=0, keepdims=True)

    def cell_span(lo, hi):
        def body(c, carry):
            s = sc_ref[chunk_rows(c), :]
            inside = (s >= lo) & (s < hi)
            return (jnp.minimum(carry[0], fold(jnp.where(inside, s, jnp.inf), jnp.min)),
                    jnp.maximum(carry[1], fold(jnp.where(inside, s, -jnp.inf), jnp.max)))
        init = (jnp.full((SUBLANES, tq), jnp.inf, F32), jnp.full((SUBLANES, tq), -jnp.inf, F32))
        lo_acc, hi_acc = over_chunks(body, init)
        return (jnp.min(lo_acc, axis=0, keepdims=True), jnp.max(hi_acc, axis=0, keepdims=True))

    has_surplus = _threshold_search(count, cell_span, n_keys, topk, theta_ref, need_ref,
                                    count_bf16=count_bf16)
    theta = theta_ref[...]

    @pl.when(jnp.logical_not(has_surplus))
    def _():
        def bias_chunk(c, carry):
            rows = chunk_rows(c)
            sel = (sc_ref[rows, :] >= theta) & (c * ch + kpos_ch <= qpos)
            sc_ref[rows, :] = jnp.where(sel, 0.0, MASKED)
            return carry
        over_chunks(bias_chunk, 0)

    @pl.when(has_surplus)
    def _():
        need = need_ref[...]
        lower = _strict_triangle(ch, lower=True)

        def bias_chunk(c, seen):
            rows = chunk_rows(c)
            s = sc_ref[rows, :]
            tie = jnp.where(s == theta, 1.0, 0.0)
            rank = seen + jnp.dot(lower, tie.astype(BF16), preferred_element_type=F32)
            sel = ((s > theta) | ((s == theta) & (rank < need))) & (c * ch + kpos_ch <= qpos)
            sc_ref[rows, :] = jnp.where(sel, 0.0, MASKED)
            return seen + jnp.sum(fold(tie, jnp.sum), axis=0, keepdims=True)
        over_chunks(bias_chunk, jnp.zeros((1, tq), F32))

    heads = [(h, slice(h * head_dim, (h + 1) * head_dim)) for h in range(n_heads)]
    m_ref[...] = jnp.full(m_ref.shape, -jnp.inf, F32)
    l_ref[...] = jnp.zeros(l_ref.shape, F32)
    y_ref[...] = jnp.zeros(y_ref.shape, F32)

    nca = (i + 1) * (tq // cha)

    def logits(c, slot):
        rows = pl.ds(pl.multiple_of(c * cha, cha), cha)
        bias = sc_ref[rows, :]
        for h, hs in heads:
            s = jnp.dot(k_ref[0, h, rows, :], qt_ref[0, hs, :],
                        preferred_element_type=F32) + bias
            s_ref[slot, h] = s
            mx_ref[slot, h:h + 1, :] = jnp.max(s, axis=0, keepdims=True)

    def update(c, slot):
        for h, hs in heads:
            m = m_ref[h:h + 1, :]
            m_new = jnp.maximum(m, mx_ref[slot, h:h + 1, :])
            alpha = jnp.exp(m - m_new)
            p = jnp.exp(s_ref[slot, h] - m_new)
            m_ref[h:h + 1, :] = m_new
            l_ref[h:h + 1, :] = alpha * l_ref[h:h + 1, :] + jnp.sum(p, axis=0, keepdims=True)
            y_ref[hs, :] = alpha * y_ref[hs, :] + jnp.dot(
                vt_ref[0, c, hs, :], p.astype(BF16), preferred_element_type=F32)

    logits(0, 0)

    def chunk_pair(j, carry):
        c0 = 2 * j
        logits(c0 + 1, 1)
        update(c0, 0)
        logits(jnp.minimum(c0 + 2, nca - 1), 0)
        update(c0 + 1, 1)
        return carry

    lax.fori_loop(0, nca // 2, chunk_pair, 0)
    for h, hs in heads:
        y_ref[hs, :] = y_ref[hs, :] / l_ref[h:h + 1, :]
    out_ref[0] = _rms_gain(y_ref[...].T, gn_ref[...]).astype(BF16)


def _attn_prompt(qit, wt, qt, kidxb, k_hm, vt, gn_att, *, tq, topk, idx_dim):
    bsz, d_att, t = qt.shape
    n_heads, head_dim = k_hm.shape[1], k_hm.shape[3]
    ch = tq
    cha = vt.shape[3]
    assert (tq // cha) % 2 == 0
    cols = lambda a: pl.BlockSpec((1, a.shape[1], tq), lambda b, i: (b, 0, i))
    full = lambda a: pl.BlockSpec((1,) + a.shape[1:], lambda b, i: (b,) + (0,) * (a.ndim - 1))
    body = functools.partial(
        _attn_prompt_body, tq=tq, ch=ch, cha=cha, topk=topk, n_heads=n_heads,
        head_dim=head_dim, idx_heads=wt.shape[1], idx_dim=idx_dim)
    return pl.pallas_call(
        body,
        grid=(bsz, t // tq),
        in_specs=[cols(qit), cols(wt), cols(qt), full(kidxb), full(k_hm), full(vt),
                  pl.BlockSpec((1, d_att), lambda b, i: (0, 0))],
        out_specs=pl.BlockSpec((1, tq, d_att), lambda b, i: (b, i, 0)),
        out_shape=jax.ShapeDtypeStruct((bsz, t, d_att), BF16),
        scratch_shapes=[pltpu.VMEM((t, tq), F32), pltpu.VMEM((t, tq), BF16),
                        pltpu.VMEM((d_att, tq), F32),
                        pltpu.VMEM((1, tq), F32), pltpu.VMEM((1, tq), F32),
                        pltpu.VMEM((n_heads, tq), F32), pltpu.VMEM((n_heads, tq), F32),
                        pltpu.VMEM((2, n_heads, cha, tq), F32),
                        pltpu.VMEM((2, n_heads, tq), F32)],
        compiler_params=_params("parallel", "arbitrary"),
        name="attn_prompt",
    )(qit, wt, qt, kidxb, k_hm, vt, gn_att)


def _score_sample_body(pt_ref, qi_ref, w_ref, *refs, group, n_pages, page, t_len, idx_heads,
                       past_len):
    del pt_ref
    page_refs, new_ref, out_ref = refs[:group * n_pages], refs[-2], refs[-1]
    for g in range(group):
        qi = qi_ref[g]
        w = w_ref[g]
        pages = page_refs[g * n_pages:(g + 1) * n_pages]
        kt = jnp.concatenate([r[0].astype(BF16) for r in pages] + [new_ref[g]], axis=1)
        d = jnp.maximum(jnp.dot(qi, kt, preferred_element_type=F32), 0.0) * w
        s = jnp.sum(d.reshape(t_len, idx_heads, past_len + page), axis=1)
        tpos = past_len + lax.broadcasted_iota(I32, s.shape, 0)
        kpos = lax.broadcasted_iota(I32, s.shape, 1)
        out_ref[g] = jnp.where(kpos <= tpos, s, -jnp.inf)


def _select_sample_body(s_ref, bias_ref, sc_ref, theta_ref, need_ref, *, topk, page):
    nc, rows, _ = sc_ref.shape
    for c in range(nc):
        sc_ref[c] = s_ref[:, c * page:(c + 1) * page]

    def count(pred):
        def body(c, acc):
            return acc + jnp.where(pred(sc_ref[c]), 1, 0).astype(I32)
        acc = lax.fori_loop(0, nc, body, jnp.zeros((rows, page), I32))
        return jnp.sum(acc, axis=1, keepdims=True)

    def cell_span(lo, hi):
        def body(c, carry):
            s = sc_ref[c]
            inside = (s >= lo) & (s < hi)
            return (jnp.minimum(carry[0], jnp.where(inside, s, jnp.inf)),
                    jnp.maximum(carry[1], jnp.where(inside, s, -jnp.inf)))
        init = (jnp.full((rows, page), jnp.inf, F32), jnp.full((rows, page), -jnp.inf, F32))
        lo_acc, hi_acc = lax.fori_loop(0, nc, body, init)
        return (jnp.min(lo_acc, axis=1, keepdims=True), jnp.max(hi_acc, axis=1, keepdims=True))

    has_surplus = _threshold_search(count, cell_span, nc * page, topk, theta_ref, need_ref)
    theta = theta_ref[...]

    @pl.when(jnp.logical_not(has_surplus))
    def _():
        for c in range(nc):
            s = sc_ref[c]
            sel = (s >= theta) & (s > -jnp.inf)
            bias_ref[:, c * page:(c + 1) * page] = jnp.where(sel, 0.0, MASKED)

    @pl.when(has_surplus)
    def _():
        need = need_ref[...]
        upper = _strict_triangle(page, lower=False)
        seen = jnp.zeros((rows, 1), F32)
        for c in range(nc):
            s = sc_ref[c]
            tie = jnp.where(s == theta, 1.0, 0.0)
            rank = seen + jnp.dot(tie.astype(BF16), upper, preferred_element_type=F32)
            sel = ((s > theta) | ((s == theta) & (rank < need))) & (s > -jnp.inf)
            bias_ref[:, c * page:(c + 1) * page] = jnp.where(sel, 0.0, MASKED)
            seen = seen + jnp.sum(tie, axis=1, keepdims=True)


def _attn_sample_body(pt_ref, q_ref, bias_ref, gn_ref, *refs, n_pages, page, t_len, n_heads,
                      head_dim):
    del pt_ref
    k_refs, v_refs = refs[:n_pages], refs[n_pages:2 * n_pages]
    knew_ref, vnew_ref, out_ref = refs[2 * n_pages:]
    past = n_pages * page
    bias = bias_ref[0]
    outs = []
    del past
    logits = []
    for h in range(n_heads):
        kt = jnp.concatenate([r[0, h].astype(BF16) for r in k_refs] + [knew_ref[0, h]], axis=1)
        logits.append(jnp.dot(q_ref[0, h], kt, preferred_element_type=F32) + bias)
    for h in range(n_heads):
        s = logits[h]
        m = jnp.max(s, axis=1, keepdims=True)
        p = jnp.exp(s - m)
        l = jnp.sum(p, axis=1, keepdims=True)
        vt = jnp.concatenate([r[0, h].astype(BF16) for r in v_refs] + [vnew_ref[0, h]], axis=1)
        outs.append(_nt_dot(p.astype(BF16), vt) / l)
    y = jnp.concatenate(outs, axis=1)
    out_ref[0] = _rms_gain(y, gn_ref[...]).astype(BF16)


def _attn_sample(qib, wi, qb, kidx_new, k_new, v_new, cache_k, cache_v, cache_kidx, page_table,
                 gn_att, *, topk, n_heads):
    dbs, t_len, d_att = qb.shape
    n_pages = page_table.shape[1]
    page = cache_k.shape[1]
    idx_dim = cache_kidx.shape[2]
    idx_heads = wi.shape[2]
    head_dim = d_att // n_heads
    nkp = (n_pages + 1) * page
    past_len = n_pages * page
    pad = lambda a: jnp.pad(a, ((0, 0), (0, page - t_len), (0, 0)))
    kidx_t = jnp.transpose(cache_kidx, (0, 2, 1))
    k_t = jnp.transpose(cache_k, (0, 2, 3, 1))
    v_t = jnp.transpose(cache_v, (0, 2, 3, 1))
    new_t = lambda a: jnp.transpose(pad(a).reshape(dbs, page, n_heads, head_dim), (0, 2, 3, 1))

    def paged(shape):
        return [pl.BlockSpec((1,) + shape, functools.partial(
            lambda b, pt, p: (pt[b, p],) + (0,) * len(shape), p=p)) for p in range(n_pages)]

    per_seq = lambda shape: pl.BlockSpec((1,) + shape, lambda b, pt: (b,) + (0,) * len(shape))

    group = _row_tile(dbs, SCORE_GROUP)
    seq_group = lambda shape: pl.BlockSpec((group,) + shape,
                                           lambda b, pt: (b,) + (0,) * len(shape))
    group_pages = [pl.BlockSpec((1, idx_dim, page), functools.partial(
        lambda b, pt, g, p: (pt[b * group + g, p], 0, 0), g=g, p=p))
        for g in range(group) for p in range(n_pages)]
    scores = pl.pallas_call(
        functools.partial(_score_sample_body, group=group, n_pages=n_pages, page=page,
                          t_len=t_len, idx_heads=idx_heads, past_len=past_len),
        grid_spec=pltpu.PrefetchScalarGridSpec(
            num_scalar_prefetch=1, grid=(dbs // group,),
            in_specs=[seq_group((t_len * idx_heads, idx_dim)), seq_group((t_len * idx_heads, 1))]
                     + group_pages + [seq_group((idx_dim, page))],
            out_specs=seq_group((t_len, nkp))),
        out_shape=jax.ShapeDtypeStruct((dbs, t_len, nkp), F32),
        compiler_params=_params("parallel"),
        name="score_sample",
    )(page_table, qib.reshape(dbs, t_len * idx_heads, idx_dim),
      wi.reshape(dbs, t_len * idx_heads, 1), *([kidx_t] * (group * n_pages)),
      jnp.transpose(pad(kidx_new), (0, 2, 1)))

    rows = dbs * t_len
    tr = _row_tile(rows, SELECT_ROWS)
    bias = pl.pallas_call(
        functools.partial(_select_sample_body, topk=topk, page=page),
        grid=(rows // tr,),
        in_specs=[pl.BlockSpec((tr, nkp), lambda r: (r, 0))],
        out_specs=pl.BlockSpec((tr, nkp), lambda r: (r, 0)),
        out_shape=jax.ShapeDtypeStruct((rows, nkp), F32),
        scratch_shapes=[pltpu.VMEM((n_pages + 1, tr, page), F32), pltpu.VMEM((tr, 1), F32),
                        pltpu.VMEM((tr, 1), F32)],
        compiler_params=_params("parallel"),
        name="select_sample",
    )(scores.reshape(rows, nkp)).reshape(dbs, t_len, nkp)

    kv_pages = paged((n_heads, head_dim, page))
    return pl.pallas_call(
        functools.partial(_attn_sample_body, n_pages=n_pages, page=page, t_len=t_len,
                          n_heads=n_heads, head_dim=head_dim),
        grid_spec=pltpu.PrefetchScalarGridSpec(
            num_scalar_prefetch=1, grid=(dbs,),
            in_specs=[per_seq((n_heads, t_len, head_dim)), per_seq((t_len, nkp)),
                      pl.BlockSpec((1, d_att), lambda b, pt: (0, 0))]
                     + kv_pages + kv_pages
                     + [per_seq((n_heads, head_dim, page)), per_seq((n_heads, head_dim, page))],
            out_specs=per_seq((t_len, d_att))),
        out_shape=jax.ShapeDtypeStruct((dbs, t_len, d_att), BF16),
        compiler_params=_params("parallel"),
        name="attn_sample",
    )(page_table, qb.reshape(dbs, t_len, n_heads, head_dim).transpose(0, 2, 1, 3), bias, gn_att,
      *([k_t] * n_pages), *([v_t] * n_pages), new_t(k_new), new_t(v_new))


def _layer_norm(x, g, b):
    mu = jnp.mean(x, axis=-1, keepdims=True)
    xc = x - mu
    var = jnp.mean(xc * xc, axis=-1, keepdims=True)
    return xc * lax.rsqrt(var + LN_EPS) * g + b


def _finish_body(x_ref, ml_ref, ma_ref, wo_ref, wfi_ref, wfo_ref, p_ref, out_ref,
                 *, alpha, d_lru, d_ff, fc):
    ln1_g, ln1_b, ln2_g, ln2_b = (p_ref[r:r + 1, :] for r in range(4))
    y = (jnp.dot(ml_ref[...], wo_ref[0:d_lru, :], preferred_element_type=F32)
         + jnp.dot(ma_ref[...], wo_ref[d_lru:, :], preferred_element_type=F32))
    x1 = _layer_norm(alpha * x_ref[...] + y, ln1_g, ln1_b)
    x1b = x1.astype(BF16)
    f = jnp.zeros(x1.shape, F32)
    for c in range(d_ff // fc):
        u = jnp.dot(x1b, wfi_ref[:, c * fc:(c + 1) * fc], preferred_element_type=F32)
        g = jnp.dot(x1b, wfi_ref[:, d_ff + c * fc:d_ff + (c + 1) * fc],
                    preferred_element_type=F32)
        hidden = (g * jax.nn.sigmoid(g) * u).astype(BF16)
        f = f + jnp.dot(hidden, wfo_ref[c * fc:(c + 1) * fc, :], preferred_element_type=F32)
    out_ref[...] = _layer_norm(alpha * x1 + f, ln2_g, ln2_b)


def _finish(x2d, mix_lru, mix_att, wo, wfi, wfo, pvec, *, alpha, tm):
    n, d_model = x2d.shape
    d_lru = mix_lru.shape[1]
    d_ff = wfo.shape[0]
    fc = 2 * LANES if d_ff % (2 * LANES) == 0 else LANES
    row = lambda w: pl.BlockSpec((tm, w), lambda i: (i, 0))
    const = lambda a: pl.BlockSpec(a.shape, lambda i: (0, 0), pipeline_mode=pl.Buffered(1))
    return pl.pallas_call(
        functools.partial(_finish_body, alpha=alpha, d_lru=d_lru, d_ff=d_ff, fc=fc),
        grid=(n // tm,),
        in_specs=[row(d_model), row(d_lru), row(mix_att.shape[1]),
                  const(wo), const(wfi), const(wfo), const(pvec)],
        out_specs=row(d_model),
        out_shape=jax.ShapeDtypeStruct((n, d_model), F32),
        compiler_params=_params("parallel"),
        name="finish",
    )(x2d, mix_lru, mix_att, wo, wfi, wfo, pvec)


def _block_diag(w):
    nb, bi, bo = w.shape
    eye = jnp.eye(nb, dtype=w.dtype)
    return (w[:, :, None, :] * eye[:, None, :, None]).reshape(nb * bi, nb * bo)


def _row_tile(n, want):
    tm = min(n, want)
    while n % tm:
        tm //= 2
    return tm


def kernel(x_prompt, x_sample, cache_k, cache_v, cache_kidx, state_conv, state_h, page_table,
           w_in, conv_w, conv_b, w_a, b_a, w_x, b_x, lam, gn_lru, gn_att, w_out,
           ln1_g, ln1_b, w_ffn_in, w_ffn_out, ln2_g, ln2_b):
    depth, d_model, d_in = w_in.shape
    bsz, seq, _ = x_prompt.shape
    dbs, dseq, _ = x_sample.shape
    d_lru = conv_w.shape[2]
    n_phys, page, n_heads, head_dim = cache_k.shape[1:]
    d_att = n_heads * head_dim
    idx_dim = cache_kidx.shape[3]
    d_qi = IDX_HEADS * idx_dim
    n_pages = page_table.shape[1]
    past_len = n_pages * page
    alpha = (2.0 * depth) ** 0.25
    assert d_in == 2 * d_lru + 3 * d_att + d_qi + idx_dim + IDX_HEADS
    assert idx_dim + IDX_HEADS <= LANES and LANES % head_dim == 0 and head_dim == idx_dim
    geom = dict(d_lru=d_lru, d_att=d_att, d_qi=d_qi, idx_dim=idx_dim, head_dim=head_dim)
    d_main = d_in - idx_dim - IDX_HEADS

    pos_s = jnp.tile(past_len + jnp.arange(dseq), dbs)
    topk_p = min(TOPK_MAX, seq // 4)
    topk_s = min(TOPK_MAX, (past_len + dseq) // 4)

    xp = x_prompt.reshape(bsz * seq, d_model)
    xs = x_sample.reshape(dbs * dseq, d_model)
    outs_p, outs_s = [], []
    for l in range(depth):
        w_pad = jnp.pad(w_in[l], ((0, 0), (0, d_main + LANES - d_in))).astype(BF16)
        wa_bd = _block_diag(w_a[l]).astype(BF16)
        wx_bd = _block_diag(w_x[l]).astype(BF16)
        lru_vec = jnp.stack([conv_b[l], b_a[l], b_x[l], lam[l], gn_lru[l]]
                            + [jnp.zeros_like(lam[l])] * 3)
        fin_vec = jnp.stack([ln1_g[l], ln1_b[l], ln2_g[l], ln2_b[l]])
        wo, wfi, wfo = (w_out[l].astype(BF16), w_ffn_in[l].astype(BF16),
                        w_ffn_out[l].astype(BF16))
        gn_a = gn_att[l][None, :]

        tq = _row_tile(seq, ATTN_QUERIES)
        xl, gate, qt, qit, kt, k_hm, vt, vt_chunks, kidx_t, kidx_b, wt = _project_cols(
            xp.reshape(bsz, seq, d_model), w_pad, tm=_row_tile(seq, PROJ_ROWS), cha=tq // 2,
            n_heads=n_heads, idx_heads=IDX_HEADS, **geom)
        xl3 = xl.reshape(bsz, seq, d_lru)
        mix_l, h_last = _lru_prompt(
            xl3, gate.reshape(bsz, seq, d_lru), jnp.zeros((bsz, CONV_W - 1, d_lru), F32),
            jnp.zeros((bsz, d_lru), F32), conv_w[l], lru_vec, wa_bd, wx_bd,
            tt=_row_tile(seq, LRU_ROWS))
        mix_a = _attn_prompt(qit, wt, qt, kidx_b, k_hm, vt_chunks, gn_a, tq=tq, topk=topk_p,
                             idx_dim=idx_dim)
        xp = _finish(xp, mix_l.reshape(bsz * seq, d_lru), mix_a.reshape(bsz * seq, d_att),
                     wo, wfi, wfo, fin_vec, alpha=alpha, tm=_row_tile(bsz * seq, FINISH_ROWS))
        token_major = lambda a: a.reshape(bsz, n_heads, head_dim, seq).transpose(0, 3, 1, 2)
        outs_p.append((token_major(kt), token_major(vt), jnp.swapaxes(kidx_t, 1, 2),
                       xl3[:, seq - (CONV_W - 1):], h_last[:, 0]))

        xl, gate, qb, k, kb, v, vb, qib, tail = _project_rows(
            xs, w_pad, pos_s, tm=_row_tile(dbs * dseq, PROJ_ROWS_SAMPLE), **geom)
        d3 = lambda a: a.reshape(dbs, dseq, a.shape[-1])
        tm_major = lambda a: jnp.swapaxes(d3(a), 0, 1)
        xl3 = d3(xl)
        mix_l, h_last = _lru_sample(
            tm_major(xl), tm_major(gate), jnp.swapaxes(state_conv[l], 0, 1).astype(F32),
            state_h[l], conv_w[l], lru_vec, wa_bd, wx_bd)
        tail3 = d3(tail)
        ki = tail3[:, :, :idx_dim]
        wi = tail3[:, :, idx_dim:idx_dim + IDX_HEADS]
        mix_a = _attn_sample(
            d3(qib), wi, d3(qb), ki.astype(BF16), d3(kb), d3(vb),
            cache_k[l], cache_v[l], cache_kidx[l], page_table, gn_a, topk=topk_s,
            n_heads=n_heads)
        xs = _finish(xs, jnp.swapaxes(mix_l, 0, 1).reshape(dbs * dseq, d_lru),
                     mix_a.reshape(dbs * dseq, d_att), wo, wfi, wfo, fin_vec, alpha=alpha,
                     tm=_row_tile(dbs * dseq, FINISH_ROWS_SAMPLE))
        conv_new = jnp.concatenate([state_conv[l].astype(F32), xl3], axis=1)[:, -(CONV_W - 1):]
        outs_s.append((k.reshape(dbs, dseq, n_heads, head_dim), v.reshape(dbs, dseq, n_heads, head_dim),
                       ki, conv_new, h_last))

    stack = lambda outs, j: jnp.stack([o[j] for o in outs])
    return (xp.reshape(bsz, seq, d_model), xs.reshape(dbs, dseq, d_model),
            *(stack(outs_p, j) for j in range(5)), *(stack(outs_s, j) for j in range(5)))
```

```python
import functools

import jax
import jax.numpy as jnp
import numpy as np
from jax import lax
from jax.experimental import pallas as pl
from jax.experimental.pallas import tpu as pltpu

CONV_W = 4
LRU_C = 8.0
IDX_HEADS = 8
TOPK_MAX = 256
ROPE_FRACTION = 4
ROPE_THETA = 500000.0
RMS_EPS = 1e-6
LN_EPS = 1e-5

LANES = 128
SUBLANES = 8
VMEM_LIMIT = 56 * 1024 * 1024
MASKED = -1e30
INT_MIN = -2 ** 31

PROJ_ROWS = 512
PROJ_ROWS_SAMPLE = 256
FINISH_ROWS = 512
FINISH_ROWS_SAMPLE = 256
LRU_ROWS = 256
ATTN_QUERIES = 512
SCORE_GROUP = 4
SELECT_ROWS = 128

F32 = jnp.float32
BF16 = jnp.bfloat16
I32 = jnp.int32


def _params(*sem):
    return pltpu.CompilerParams(dimension_semantics=sem, vmem_limit_bytes=VMEM_LIMIT)


def _nt_dot(a, b):
    return lax.dot_general(a, b, (((1,), (1,)), ((), ())), preferred_element_type=F32)


def _proj_streams(x_ref, w_ref, c_ref, s1_ref, s2_ref, *, d_lru, d_att, d_qi, idx_dim, rope_half,
                  q_scale):
    xb = x_ref[...].astype(BF16)
    c, s1, s2 = c_ref[...], s1_ref[...], s2_ref[...]

    def proj(lo, width):
        return jnp.dot(xb, w_ref[:, lo:lo + width], preferred_element_type=F32)

    def tiled(t, width):
        reps = width // LANES
        return t if reps == 1 else jnp.concatenate([t] * reps, axis=1)

    def rope(z, cc, ss1, ss2):
        width = z.shape[1]
        return (z * cc + pltpu.roll(z, width - rope_half, 1) * ss1
                + pltpu.roll(z, rope_half, 1) * ss2)

    lo = 0
    xl = proj(lo, d_lru); lo += d_lru
    gate = proj(lo, d_lru); lo += d_lru
    q = rope(proj(lo, d_att), tiled(c, d_att), tiled(s1, d_att), tiled(s2, d_att)); lo += d_att
    k = rope(proj(lo, d_att), tiled(c, d_att), tiled(s1, d_att), tiled(s2, d_att)); lo += d_att
    v = proj(lo, d_att); lo += d_att
    qi = rope(proj(lo, d_qi), tiled(c, d_qi), tiled(s1, d_qi), tiled(s2, d_qi)); lo += d_qi
    tail = proj(lo, LANES)
    is_key = lax.broadcasted_iota(I32, tail.shape, 1) < idx_dim
    tail = rope(tail, jnp.where(is_key, c, 1.0), jnp.where(is_key, s1, 0.0),
                jnp.where(is_key, s2, 0.0))
    return xl, gate, q * q_scale, k, v, qi, tail


def _proj_rows_body(x_ref, w_ref, c_ref, s1_ref, s2_ref,
                    xl_ref, gate_ref, qb_ref, k_ref, kb_ref, v_ref, vb_ref, qib_ref, tail_ref,
                    **geom):
    xl, gate, q, k, v, qi, tail = _proj_streams(x_ref, w_ref, c_ref, s1_ref, s2_ref, **geom)
    xl_ref[...] = xl
    gate_ref[...] = gate
    qb_ref[...] = q.astype(BF16)
    k_ref[...] = k
    kb_ref[...] = k.astype(BF16)
    v_ref[...] = v
    vb_ref[...] = v.astype(BF16)
    qib_ref[...] = qi.astype(BF16)
    tail_ref[...] = tail


def _proj_cols_body(x_ref, w_ref, c_ref, s1_ref, s2_ref,
                    xl_ref, gate_ref, qt_ref, qit_ref, kt_ref, khm_ref, vt_ref, vtc_ref,
                    kidxt_ref, kidx_ref, wt_ref, *, cha, n_heads, idx_heads, **geom):
    xl, gate, q, k, v, qi, tail = _proj_streams(x_ref, w_ref, c_ref, s1_ref, s2_ref, **geom)
    idx_dim = geom["idx_dim"]
    head_dim = k.shape[1] // n_heads
    xl_ref[...] = xl
    gate_ref[...] = gate
    qt_ref[0] = q.T.astype(BF16)
    qit_ref[0] = qi.T.astype(BF16)
    kt_ref[0] = k.T
    for h in range(n_heads):
        khm_ref[0, h] = k[:, h * head_dim:(h + 1) * head_dim].astype(BF16)
    vt = v.T
    vt_ref[0] = vt
    for j in range(vt.shape[1] // cha):
        vtc_ref[0, j] = vt[:, j * cha:(j + 1) * cha].astype(BF16)
    tail_t = tail.T
    kidxt_ref[0] = tail_t[:idx_dim]
    wt_ref[0] = tail_t[idx_dim:idx_dim + idx_heads]
    kidx_ref[0] = tail[:, :idx_dim].astype(BF16)


def _rope_tables(pos, head_dim):
    rope_dim = head_dim // ROPE_FRACTION
    half = rope_dim // 2
    freqs = ROPE_THETA ** (-jnp.arange(half, dtype=F32) / half)
    dim = jnp.arange(LANES) % head_dim
    ang = pos.astype(F32)[:, None] * freqs[dim % half][None, :]
    cos, sin = jnp.cos(ang), jnp.sin(ang)
    c = jnp.where(dim < rope_dim, cos, 1.0)
    s1 = jnp.where(dim < half, -sin, 0.0)
    s2 = jnp.where((dim >= half) & (dim < rope_dim), sin, 0.0)
    return [c, s1, s2], half


def _project_rows(x2d, w_pad, pos, *, d_lru, d_att, d_qi, idx_dim, head_dim, tm):
    n, d_model = x2d.shape
    (c, s1, s2), half = _rope_tables(pos, head_dim)
    row = lambda w: pl.BlockSpec((tm, w), lambda i: (i, 0))
    outs = [
        (d_lru, F32), (d_lru, F32), (d_att, BF16), (d_att, F32), (d_att, BF16),
        (d_att, F32), (d_att, BF16), (d_qi, BF16), (LANES, F32)]
    body = functools.partial(_proj_rows_body, d_lru=d_lru, d_att=d_att, d_qi=d_qi,
                             idx_dim=idx_dim, rope_half=half, q_scale=head_dim ** -0.5)
    return pl.pallas_call(
        body,
        grid=(n // tm,),
        in_specs=[row(d_model), pl.BlockSpec(w_pad.shape, lambda i: (0, 0)),
                  row(LANES), row(LANES), row(LANES)],
        out_specs=[row(w) for w, _ in outs],
        out_shape=[jax.ShapeDtypeStruct((n, w), dt) for w, dt in outs],
        compiler_params=_params("parallel"),
        name="proj",
    )(x2d, w_pad, c, s1, s2)


def _project_cols(x3d, w_pad, *, d_lru, d_att, d_qi, idx_dim, head_dim, tm, cha, n_heads,
                  idx_heads):
    bsz, t, d_model = x3d.shape
    n, nt = bsz * t, t // tm
    (c, s1, s2), half = _rope_tables(jnp.arange(t), head_dim)
    row = lambda w: pl.BlockSpec((tm, w), lambda i: (i, 0))
    table = pl.BlockSpec((tm, LANES), lambda i: (i % nt, 0))
    col = lambda w: pl.BlockSpec((1, w, tm), lambda i: (i // nt, 0, i % nt))
    col_shape = lambda w, dt: jax.ShapeDtypeStruct((bsz, w, t), dt)
    body = functools.partial(
        _proj_cols_body, d_lru=d_lru, d_att=d_att, d_qi=d_qi, idx_dim=idx_dim, rope_half=half,
        q_scale=head_dim ** -0.5, cha=cha, n_heads=n_heads, idx_heads=idx_heads)
    return pl.pallas_call(
        body,
        grid=(n // tm,),
        in_specs=[row(d_model), pl.BlockSpec(w_pad.shape, lambda i: (0, 0)), table, table, table],
        out_specs=[
            row(d_lru), row(d_lru), col(d_att), col(d_qi), col(d_att),
            pl.BlockSpec((1, n_heads, tm, head_dim), lambda i: (i // nt, 0, i % nt, 0)),
            col(d_att),
            pl.BlockSpec((1, tm // cha, d_att, cha), lambda i: (i // nt, i % nt, 0, 0)),
            col(idx_dim), pl.BlockSpec((1, tm, idx_dim), lambda i: (i // nt, i % nt, 0)),
            col(idx_heads)],
        out_shape=[
            jax.ShapeDtypeStruct((n, d_lru), F32), jax.ShapeDtypeStruct((n, d_lru), F32),
            col_shape(d_att, BF16), col_shape(d_qi, BF16), col_shape(d_att, F32),
            jax.ShapeDtypeStruct((bsz, n_heads, t, head_dim), BF16),
            col_shape(d_att, F32),
            jax.ShapeDtypeStruct((bsz, t // cha, d_att, cha), BF16),
            col_shape(idx_dim, F32), jax.ShapeDtypeStruct((bsz, t, idx_dim), BF16),
            col_shape(idx_heads, F32)],
        compiler_params=_params("parallel"),
        name="proj_prompt",
    )(x3d.reshape(n, d_model), w_pad, c, s1, s2)


def _softplus(x):
    return jnp.maximum(x, 0.0) + jnp.log1p(jnp.exp(-jnp.abs(x)))


def _gelu_tanh(x):
    return 0.5 * x * (1.0 + jnp.tanh(np.sqrt(2.0 / np.pi).astype(np.float32)
                                     * (x + 0.044715 * (x * x * x))))


def _lru_gates(xc, wa_ref, wx_ref, b_a, b_x, lam):
    xcb = xc.astype(BF16)
    r = jax.nn.sigmoid(jnp.dot(xcb, wa_ref[...], preferred_element_type=F32) + b_a)
    i = jax.nn.sigmoid(jnp.dot(xcb, wx_ref[...], preferred_element_type=F32) + b_x)
    log_a = -LRU_C * r * _softplus(-lam)
    a = jnp.exp(log_a)
    t = jnp.tanh(log_a)
    b = jnp.sqrt(-2.0 * t / (1.0 - t)) * (i * xc)
    return a, b


def _rms_gain(y, g):
    return y * lax.rsqrt(jnp.mean(y * y, axis=-1, keepdims=True) + RMS_EPS) * g


def _lru_prompt_body(xl_ref, gate_ref, cprev_ref, h0_ref, cw_ref, p_ref, wa_ref, wx_ref,
                     mix_ref, hlast_ref, ext_ref, hc_ref, *, tt):
    j = pl.program_id(1)

    @pl.when(j == 0)
    def _():
        ext_ref[0:SUBLANES, :] = cprev_ref[0]
        hc_ref[0:1, :] = h0_ref[0]

    xl = xl_ref[0]
    ext_ref[SUBLANES:SUBLANES + tt, :] = xl
    conv_b, b_a, b_x, lam, gn = (p_ref[r:r + 1, :] for r in range(5))
    xc = conv_b + (cw_ref[0:1, :] * ext_ref[SUBLANES - 3:SUBLANES - 3 + tt, :]
                   + cw_ref[1:2, :] * ext_ref[SUBLANES - 2:SUBLANES - 2 + tt, :]
                   + cw_ref[2:3, :] * ext_ref[SUBLANES - 1:SUBLANES - 1 + tt, :]
                   + cw_ref[3:4, :] * xl)
    ext_ref[0:SUBLANES, :] = ext_ref[tt:tt + SUBLANES, :]

    a, b = _lru_gates(xc, wa_ref, wx_ref, b_a, b_x, lam)
    groups = tt // SUBLANES
    a = a.reshape(groups, SUBLANES, a.shape[1])
    b = b.reshape(groups, SUBLANES, b.shape[1])
    row = lax.broadcasted_iota(I32, a.shape, 1)
    d = 1
    while d < SUBLANES:
        keep = row >= d
        a_prev = jnp.where(keep, pltpu.roll(a, d, 1), 1.0)
        b_prev = jnp.where(keep, pltpu.roll(b, d, 1), 0.0)
        b = a * b_prev + b
        a = a * a_prev
        d *= 2
    state = hc_ref[0:1, :]
    hs = []
    for g in range(groups):
        hs.append(a[g] * state + b[g])
        state = hs[-1][SUBLANES - 1:SUBLANES, :]
    h = jnp.concatenate(hs, axis=0)
    hc_ref[0:1, :] = state
    hlast_ref[0] = state
    mix_ref[0] = _rms_gain(h * _gelu_tanh(gate_ref[0]), gn).astype(BF16)


def _lru_prompt(xl, gate, conv_prev, h0, conv_w, pvec, wa_bd, wx_bd, *, tt):
    bsz, t, d = xl.shape
    cprev8 = jnp.concatenate(
        [jnp.zeros((bsz, SUBLANES - (CONV_W - 1), d), F32), conv_prev.astype(F32)], axis=1)
    const = lambda shape: pl.BlockSpec(shape, lambda b, j: (0,) * len(shape))
    return pl.pallas_call(
        functools.partial(_lru_prompt_body, tt=tt),
        grid=(bsz, t // tt),
        in_specs=[pl.BlockSpec((1, tt, d), lambda b, j: (b, j, 0)),
                  pl.BlockSpec((1, tt, d), lambda b, j: (b, j, 0)),
                  pl.BlockSpec((1, SUBLANES, d), lambda b, j: (b, 0, 0)),
                  pl.BlockSpec((1, 1, d), lambda b, j: (b, 0, 0)),
                  const(conv_w.shape), const(pvec.shape), const(wa_bd.shape), const(wx_bd.shape)],
        out_specs=[pl.BlockSpec((1, tt, d), lambda b, j: (b, j, 0)),
                   pl.BlockSpec((1, 1, d), lambda b, j: (b, 0, 0))],
        out_shape=[jax.ShapeDtypeStruct((bsz, t, d), BF16),
                   jax.ShapeDtypeStruct((bsz, 1, d), F32)],
        scratch_shapes=[pltpu.VMEM((tt + SUBLANES, d), F32), pltpu.VMEM((SUBLANES, d), F32)],
        compiler_params=_params("parallel", "arbitrary"),
        name="lru_prompt",
    )(xl, gate, cprev8, h0.astype(F32)[:, None, :], conv_w, pvec, wa_bd, wx_bd)


def _lru_sample_body(xl_ref, gate_ref, cprev_ref, h0_ref, cw_ref, p_ref, wa_ref, wx_ref,
                     mix_ref, hlast_ref, *, t_len):
    conv_b, b_a, b_x, lam, gn = (p_ref[r:r + 1, :] for r in range(5))
    xp = [cprev_ref[s] for s in range(CONV_W - 1)] + [xl_ref[s] for s in range(t_len)]
    h = h0_ref[...]
    for s in range(t_len):
        xc = conv_b + (cw_ref[0:1, :] * xp[s] + cw_ref[1:2, :] * xp[s + 1]
                       + cw_ref[2:3, :] * xp[s + 2] + cw_ref[3:4, :] * xp[s + 3])
        a, b = _lru_gates(xc, wa_ref, wx_ref, b_a, b_x, lam)
        h = a * h + b
        mix_ref[s] = _rms_gain(h * _gelu_tanh(gate_ref[s]), gn).astype(BF16)
    hlast_ref[...] = h


def _lru_sample(xl_t, gate_t, cprev_t, h0, conv_w, pvec, wa_bd, wx_bd):
    t_len, dbs, d = xl_t.shape
    return pl.pallas_call(
        functools.partial(_lru_sample_body, t_len=t_len),
        out_shape=[jax.ShapeDtypeStruct((t_len, dbs, d), BF16),
                   jax.ShapeDtypeStruct((dbs, d), F32)],
        compiler_params=pltpu.CompilerParams(vmem_limit_bytes=VMEM_LIMIT),
        name="lru_sample",
    )(xl_t, gate_t, cprev_t, h0.astype(F32), conv_w, pvec, wa_bd, wx_bd)


KEY_NEG_INF = INT_MIN + 0x7FFFFF
REFINE_STEPS = 8


def _key_to_float(key):
    return pltpu.bitcast(jnp.where(key >= 0, key, key ^ 0x7FFFFFFF), F32)


KEY16_NEG_INF = -2 ** 15 + 0x7F


def _key16_to_bf16(key):
    bits = lax.shift_left(jnp.where(key >= 0, key, key ^ 0x7FFF), 16)
    return pltpu.bitcast(bits, F32).astype(BF16)


def _threshold_search(count, cell_span, total, topk, theta_ref, need_ref, count_bf16=None):
    shape = theta_ref.shape

    if count_bf16 is None:
        def value_step(it, carry):
            base, n_base = carry
            trial = base ^ lax.shift_left(jnp.int32(1), jnp.int32(31) - it)
            trial_f = _key_to_float(trial)
            n = count(lambda s: s >= trial_f)
            ok = n >= topk
            return jnp.where(ok, trial, base), jnp.where(ok, n, n_base)

        theta_key, n_ge = lax.fori_loop(
            0, 32, value_step, (jnp.full(shape, INT_MIN, I32), jnp.zeros(shape, I32) + total))
    else:
        def coarse_step(it, base):
            trial = base + lax.shift_left(jnp.int32(1), jnp.int32(15) - it)
            return jnp.where(count_bf16(_key16_to_bf16(trial)) >= topk, trial, base)

        k16 = lax.fori_loop(0, 16, coarse_step, jnp.full(shape, -2 ** 15, I32))
        k16 = jnp.maximum(k16, KEY16_NEG_INF)
        key32 = lambda k: lax.shift_left(k, 16) + jnp.where(k < 0, 0xFFFF, 0)
        lo0 = jnp.maximum(key32(k16 - 2), KEY_NEG_INF)
        hi0 = jnp.maximum(key32(k16 + 1), lo0 + 1)

        def fine_step(_, carry):
            lo, hi = carry
            mid = lo + lax.shift_right_arithmetic(hi - lo, 1)
            mid_f = _key_to_float(mid)
            ok = count(lambda s: s >= mid_f) >= topk
            return jnp.where(ok, mid, lo), jnp.where(ok, hi, mid)

        theta_key, _ = lax.fori_loop(0, 18, fine_step, (lo0, hi0))
        theta_f = _key_to_float(theta_key)
        n_ge = count(lambda s: s >= theta_f)
    theta = jnp.where(theta_key < KEY_NEG_INF, -jnp.inf, _key_to_float(theta_key))
    theta_ref[...] = theta
    has_surplus = jnp.max(n_ge) > topk

    @pl.when(has_surplus)
    def _():
        above = _key_to_float(jnp.maximum(theta_key, KEY_NEG_INF) + 1)
        cell_min, cell_max = cell_span(theta, above)

        @pl.when(jnp.max(jnp.where(cell_max > cell_min, 1, 0)) > 0)
        def _():
            def refine_step(_, carry):
                lo, hi = carry
                mid = lo + 0.5 * (hi - lo)
                ok = count(lambda s: s >= mid) >= topk
                return jnp.where(ok, mid, lo), jnp.where(ok, hi, mid)

            theta_ref[...] = lax.fori_loop(0, REFINE_STEPS, refine_step, (theta, above))[0]

        theta_fine = theta_ref[...]
        need_ref[...] = (topk - count(lambda s: s > theta_fine)).astype(F32)

    return has_surplus


def _strict_triangle(n, lower):
    r = lax.broadcasted_iota(I32, (n, n), 0)
    c = lax.broadcasted_iota(I32, (n, n), 1)
    return jnp.where((c < r) if lower else (r < c), 1.0, 0.0).astype(BF16)


def _attn_prompt_body(qit_ref, wt_ref, qt_ref, kidx_ref, k_ref, vt_ref, gn_ref, out_ref,
                      sc_ref, sb_ref, y_ref, theta_ref, need_ref, m_ref, l_ref, s_ref, mx_ref, *,
                      tq, ch, cha, topk, n_heads, head_dim, idx_heads, idx_dim):
    i = pl.program_id(1)
    n_keys = (i + 1) * tq
    qpos = i * tq + lax.broadcasted_iota(I32, (1, tq), 1)
    sub = LANES
    kpos_sub = lax.broadcasted_iota(I32, (sub, tq), 0)
    kpos_ch = lax.broadcasted_iota(I32, (ch, tq), 0)
    nc = (i + 1) * (tq // ch)

    def score_chunk(c, carry):
        for j in range(ch // sub):
            start = pl.multiple_of(c * ch + j * sub, sub)
            rows = pl.ds(start, sub)
            kc = kidx_ref[0, rows, :]
            acc = jnp.zeros((sub, tq), F32)
            for h in range(idx_heads):
                d = jnp.dot(kc, qit_ref[0, h * idx_dim:(h + 1) * idx_dim, :],
                            preferred_element_type=F32)
                acc = acc + jnp.maximum(d, 0.0) * wt_ref[0, h:h + 1, :]
            score = jnp.where(start + kpos_sub <= qpos, acc, -jnp.inf)
            sc_ref[rows, :] = score
            sb_ref[rows, :] = score.astype(BF16)
        return carry

    n_pairs = lax.div(nc, 2)

    def over_chunks(body, init):
        carry = lax.fori_loop(0, n_pairs, lambda j, x: body(2 * j + 1, body(2 * j, x)), init)
        return lax.fori_loop(2 * n_pairs, nc, body, carry)

    over_chunks(score_chunk, 0)

    chunk_rows = lambda c: pl.ds(pl.multiple_of(c * ch, ch), ch)
    fold = lambda x, op: op(x.reshape(ch // SUBLANES, SUBLANES, tq), axis=0)

    def count_bf16(trial):
        packed = 2 * SUBLANES
        one, zero = jnp.ones((ch, tq), BF16), jnp.zeros((ch, tq), BF16)

        def body(c, acc):
            hit = jnp.where(sb_ref[chunk_rows(c), :] >= trial, one, zero)
            parts = hit.reshape(ch // packed, packed, tq)
            tot = parts[0]
            for r in range(1, ch // packed):
                tot = tot + parts[r]
            return acc + tot.astype(F32)
        acc = over_chunks(body, jnp.zeros((packed, tq), F32))
        return jnp.sum(acc, axis=0, keepdims=True).astype(I32)

    def count(pred):
        def body(c, acc):
            hit = jnp.where(pred(sc_ref[chunk_rows(c), :]), 1, 0).astype(I32)
            return acc + fold(hit, jnp.sum)
        acc = over_chunks(body, jnp.zeros((SUBLANES, tq), I32))
        return jnp.sum(acc, axis=0, keepdims=True)

    def cell_span(lo, hi):
        def body(c, carry):
            s = sc_ref[chunk_rows(c), :]
            inside = (s >= lo) & (s < hi)
            return (jnp.minimum(carry[0], fold(jnp.where(inside, s, jnp.inf), jnp.min)),
                    jnp.maximum(carry[1], fold(jnp.where(inside, s, -jnp.inf), jnp.max)))
        init = (jnp.full((SUBLANES, tq), jnp.inf, F32), jnp.full((SUBLANES, tq), -jnp.inf, F32))
        lo_acc, hi_acc = over_chunks(body, init)
        return (jnp.min(lo_acc, axis=0, keepdims=True), jnp.max(hi_acc, axis=0, keepdims=True))

    has_surplus = _threshold_search(count, cell_span, n_keys, topk, theta_ref, need_ref,
                                    count_bf16=count_bf16)
    theta = theta_ref[...]

    @pl.when(jnp.logical_not(has_surplus))
    def _():
        def bias_chunk(c, carry):
            rows = chunk_rows(c)
            sel = (sc_ref[rows, :] >= theta) & (c * ch + kpos_ch <= qpos)
            sc_ref[rows, :] = jnp.where(sel, 0.0, MASKED)
            return carry
        over_chunks(bias_chunk, 0)

    @pl.when(has_surplus)
    def _():
        need = need_ref[...]
        lower = _strict_triangle(ch, lower=True)

        def bias_chunk(c, seen):
            rows = chunk_rows(c)
            s = sc_ref[rows, :]
            tie = jnp.where(s == theta, 1.0, 0.0)
            rank = seen + jnp.dot(lower, tie.astype(BF16), preferred_element_type=F32)
            sel = ((s > theta) | ((s == theta) & (rank < need))) & (c * ch + kpos_ch <= qpos)
            sc_ref[rows, :] = jnp.where(sel, 0.0, MASKED)
            return seen + jnp.sum(fold(tie, jnp.sum), axis=0, keepdims=True)
        over_chunks(bias_chunk, jnp.zeros((1, tq), F32))

    heads = [(h, slice(h * head_dim, (h + 1) * head_dim)) for h in range(n_heads)]
    m_ref[...] = jnp.full(m_ref.shape, -jnp.inf, F32)
    l_ref[...] = jnp.zeros(l_ref.shape, F32)
    y_ref[...] = jnp.zeros(y_ref.shape, F32)

    nca = (i + 1) * (tq // cha)

    def logits(c, slot):
        rows = pl.ds(pl.multiple_of(c * cha, cha), cha)
        bias = sc_ref[rows, :]
        for h, hs in heads:
            s = jnp.dot(k_ref[0, h, rows, :], qt_ref[0, hs, :],
                        preferred_element_type=F32) + bias
            s_ref[slot, h] = s
            mx_ref[slot, h:h + 1, :] = jnp.max(s, axis=0, keepdims=True)

    def update(c, slot):
        for h, hs in heads:
            m = m_ref[h:h + 1, :]
            m_new = jnp.maximum(m, mx_ref[slot, h:h + 1, :])
            alpha = jnp.exp(m - m_new)
            p = jnp.exp(s_ref[slot, h] - m_new)
            m_ref[h:h + 1, :] = m_new
            l_ref[h:h + 1, :] = alpha * l_ref[h:h + 1, :] + jnp.sum(p, axis=0, keepdims=True)
            y_ref[hs, :] = alpha * y_ref[hs, :] + jnp.dot(
                vt_ref[0, c, hs, :], p.astype(BF16), preferred_element_type=F32)

    logits(0, 0)

    def chunk_pair(j, carry):
        c0 = 2 * j
        logits(c0 + 1, 1)
        update(c0, 0)
        logits(jnp.minimum(c0 + 2, nca - 1), 0)
        update(c0 + 1, 1)
        return carry

    lax.fori_loop(0, nca // 2, chunk_pair, 0)
    for h, hs in heads:
        y_ref[hs, :] = y_ref[hs, :] / l_ref[h:h + 1, :]
    out_ref[0] = _rms_gain(y_ref[...].T, gn_ref[...]).astype(BF16)


def _attn_prompt(qit, wt, qt, kidxb, k_hm, vt, gn_att, *, tq, topk, idx_dim):
    bsz, d_att, t = qt.shape
    n_heads, head_dim = k_hm.shape[1], k_hm.shape[3]
    ch = tq
    cha = vt.shape[3]
    assert (tq // cha) % 2 == 0
    cols = lambda a: pl.BlockSpec((1, a.shape[1], tq), lambda b, i: (b, 0, i))
    full = lambda a: pl.BlockSpec((1,) + a.shape[1:], lambda b, i: (b,) + (0,) * (a.ndim - 1))
    body = functools.partial(
        _attn_prompt_body, tq=tq, ch=ch, cha=cha, topk=topk, n_heads=n_heads,
        head_dim=head_dim, idx_heads=wt.shape[1], idx_dim=idx_dim)
    return pl.pallas_call(
        body,
        grid=(bsz, t // tq),
        in_specs=[cols(qit), cols(wt), cols(qt), full(kidxb), full(k_hm), full(vt),
                  pl.BlockSpec((1, d_att), lambda b, i: (0, 0))],
        out_specs=pl.BlockSpec((1, tq, d_att), lambda b, i: (b, i, 0)),
        out_shape=jax.ShapeDtypeStruct((bsz, t, d_att), BF16),
        scratch_shapes=[pltpu.VMEM((t, tq), F32), pltpu.VMEM((t, tq), BF16),
                        pltpu.VMEM((d_att, tq), F32),
                        pltpu.VMEM((1, tq), F32), pltpu.VMEM((1, tq), F32),
                        pltpu.VMEM((n_heads, tq), F32), pltpu.VMEM((n_heads, tq), F32),
                        pltpu.VMEM((2, n_heads, cha, tq), F32),
                        pltpu.VMEM((2, n_heads, tq), F32)],
        compiler_params=_params("parallel", "arbitrary"),
        name="attn_prompt",
    )(qit, wt, qt, kidxb, k_hm, vt, gn_att)


def _score_sample_body(pt_ref, qi_ref, w_ref, *refs, group, n_pages, page, t_len, idx_heads,
                       past_len):
    del pt_ref
    page_refs, new_ref, out_ref = refs[:group * n_pages], refs[-2], refs[-1]
    for g in range(group):
        qi = qi_ref[g]
        w = w_ref[g]
        pages = page_refs[g * n_pages:(g + 1) * n_pages]
        kt = jnp.concatenate([r[0].astype(BF16) for r in pages] + [new_ref[g]], axis=1)
        d = jnp.maximum(jnp.dot(qi, kt, preferred_element_type=F32), 0.0) * w
        s = jnp.sum(d.reshape(t_len, idx_heads, past_len + page), axis=1)
        tpos = past_len + lax.broadcasted_iota(I32, s.shape, 0)
        kpos = lax.broadcasted_iota(I32, s.shape, 1)
        out_ref[g] = jnp.where(kpos <= tpos, s, -jnp.inf)


def _select_sample_body(s_ref, bias_ref, sc_ref, theta_ref, need_ref, *, topk, page):
    nc, rows, _ = sc_ref.shape
    for c in range(nc):
        sc_ref[c] = s_ref[:, c * page:(c + 1) * page]

    def count(pred):
        def body(c, acc):
            return acc + jnp.where(pred(sc_ref[c]), 1, 0).astype(I32)
        acc = lax.fori_loop(0, nc, body, jnp.zeros((rows, page), I32))
        return jnp.sum(acc, axis=1, keepdims=True)

    def cell_span(lo, hi):
        def body(c, carry):
            s = sc_ref[c]
            inside = (s >= lo) & (s < hi)
            return (jnp.minimum(carry[0], jnp.where(inside, s, jnp.inf)),
                    jnp.maximum(carry[1], jnp.where(inside, s, -jnp.inf)))
        init = (jnp.full((rows, page), jnp.inf, F32), jnp.full((rows, page), -jnp.inf, F32))
        lo_acc, hi_acc = lax.fori_loop(0, nc, body, init)
        return (jnp.min(lo_acc, axis=1, keepdims=True), jnp.max(hi_acc, axis=1, keepdims=True))

    has_surplus = _threshold_search(count, cell_span, nc * page, topk, theta_ref, need_ref)
    theta = theta_ref[...]

    @pl.when(jnp.logical_not(has_surplus))
    def _():
        for c in range(nc):
            s = sc_ref[c]
            sel = (s >= theta) & (s > -jnp.inf)
            bias_ref[:, c * page:(c + 1) * page] = jnp.where(sel, 0.0, MASKED)

    @pl.when(has_surplus)
    def _():
        need = need_ref[...]
        upper = _strict_triangle(page, lower=False)
        seen = jnp.zeros((rows, 1), F32)
        for c in range(nc):
            s = sc_ref[c]
            tie = jnp.where(s == theta, 1.0, 0.0)
            rank = seen + jnp.dot(tie.astype(BF16), upper, preferred_element_type=F32)
            sel = ((s > theta) | ((s == theta) & (rank < need))) & (s > -jnp.inf)
            bias_ref[:, c * page:(c + 1) * page] = jnp.where(sel, 0.0, MASKED)
            seen = seen + jnp.sum(tie, axis=1, keepdims=True)


def _attn_sample_body(pt_ref, q_ref, bias_ref, gn_ref, *refs, n_pages, page, t_len, n_heads,
                      head_dim):
    del pt_ref
    k_refs, v_refs = refs[:n_pages], refs[n_pages:2 * n_pages]
    knew_ref, vnew_ref, out_ref = refs[2 * n_pages:]
    past = n_pages * page
    bias = bias_ref[0]
    outs = []
    del past
    logits = []
    for h in range(n_heads):
        kt = jnp.concatenate([r[0, h].astype(BF16) for r in k_refs] + [knew_ref[0, h]], axis=1)
        logits.append(jnp.dot(q_ref[0, h], kt, preferred_element_type=F32) + bias)
    for h in range(n_heads):
        s = logits[h]
        m = jnp.max(s, axis=1, keepdims=True)
        p = jnp.exp(s - m)
        l = jnp.sum(p, axis=1, keepdims=True)
        vt = jnp.concatenate([r[0, h].astype(BF16) for r in v_refs] + [vnew_ref[0, h]], axis=1)
        outs.append(_nt_dot(p.astype(BF16), vt) / l)
    y = jnp.concatenate(outs, axis=1)
    out_ref[0] = _rms_gain(y, gn_ref[...]).astype(BF16)


def _attn_sample(qib, wi, qb, kidx_new, k_new, v_new, cache_k, cache_v, cache_kidx, page_table,
                 gn_att, *, topk, n_heads):
    dbs, t_len, d_att = qb.shape
    n_pages = page_table.shape[1]
    page = cache_k.shape[1]
    idx_dim = cache_kidx.shape[2]
    idx_heads = wi.shape[2]
    head_dim = d_att // n_heads
    nkp = (n_pages + 1) * page
    past_len = n_pages * page
    pad = lambda a: jnp.pad(a, ((0, 0), (0, page - t_len), (0, 0)))
    kidx_t = jnp.transpose(cache_kidx, (0, 2, 1))
    k_t = jnp.transpose(cache_k, (0, 2, 3, 1))
    v_t = jnp.transpose(cache_v, (0, 2, 3, 1))
    new_t = lambda a: jnp.transpose(pad(a).reshape(dbs, page, n_heads, head_dim), (0, 2, 3, 1))

    def paged(shape):
        return [pl.BlockSpec((1,) + shape, functools.partial(
            lambda b, pt, p: (pt[b, p],) + (0,) * len(shape), p=p)) for p in range(n_pages)]

    per_seq = lambda shape: pl.BlockSpec((1,) + shape, lambda b, pt: (b,) + (0,) * len(shape))

    group = _row_tile(dbs, SCORE_GROUP)
    seq_group = lambda shape: pl.BlockSpec((group,) + shape,
                                           lambda b, pt: (b,) + (0,) * len(shape))
    group_pages = [pl.BlockSpec((1, idx_dim, page), functools.partial(
        lambda b, pt, g, p: (pt[b * group + g, p], 0, 0), g=g, p=p))
        for g in range(group) for p in range(n_pages)]
    scores = pl.pallas_call(
        functools.partial(_score_sample_body, group=group, n_pages=n_pages, page=page,
                          t_len=t_len, idx_heads=idx_heads, past_len=past_len),
        grid_spec=pltpu.PrefetchScalarGridSpec(
            num_scalar_prefetch=1, grid=(dbs // group,),
            in_specs=[seq_group((t_len * idx_heads, idx_dim)), seq_group((t_len * idx_heads, 1))]
                     + group_pages + [seq_group((idx_dim, page))],
            out_specs=seq_group((t_len, nkp))),
        out_shape=jax.ShapeDtypeStruct((dbs, t_len, nkp), F32),
        compiler_params=_params("parallel"),
        name="score_sample",
    )(page_table, qib.reshape(dbs, t_len * idx_heads, idx_dim),
      wi.reshape(dbs, t_len * idx_heads, 1), *([kidx_t] * (group * n_pages)),
      jnp.transpose(pad(kidx_new), (0, 2, 1)))

    rows = dbs * t_len
    tr = _row_tile(rows, SELECT_ROWS)
    bias = pl.pallas_call(
        functools.partial(_select_sample_body, topk=topk, page=page),
        grid=(rows // tr,),
        in_specs=[pl.BlockSpec((tr, nkp), lambda r: (r, 0))],
        out_specs=pl.BlockSpec((tr, nkp), lambda r: (r, 0)),
        out_shape=jax.ShapeDtypeStruct((rows, nkp), F32),
        scratch_shapes=[pltpu.VMEM((n_pages + 1, tr, page), F32), pltpu.VMEM((tr, 1), F32),
                        pltpu.VMEM((tr, 1), F32)],
        compiler_params=_params("parallel"),
        name="select_sample",
    )(scores.reshape(rows, nkp)).reshape(dbs, t_len, nkp)

    kv_pages = paged((n_heads, head_dim, page))
    return pl.pallas_call(
        functools.partial(_attn_sample_body, n_pages=n_pages, page=page, t_len=t_len,
                          n_heads=n_heads, head_dim=head_dim),
        grid_spec=pltpu.PrefetchScalarGridSpec(
            num_scalar_prefetch=1, grid=(dbs,),
            in_specs=[per_seq((n_heads, t_len, head_dim)), per_seq((t_len, nkp)),
                      pl.BlockSpec((1, d_att), lambda b, pt: (0, 0))]
                     + kv_pages + kv_pages
                     + [per_seq((n_heads, head_dim, page)), per_seq((n_heads, head_dim, page))],
            out_specs=per_seq((t_len, d_att))),
        out_shape=jax.ShapeDtypeStruct((dbs, t_len, d_att), BF16),
        compiler_params=_params("parallel"),
        name="attn_sample",
    )(page_table, qb.reshape(dbs, t_len, n_heads, head_dim).transpose(0, 2, 1, 3), bias, gn_att,
      *([k_t] * n_pages), *([v_t] * n_pages), new_t(k_new), new_t(v_new))


def _layer_norm(x, g, b):
    mu = jnp.mean(x, axis=-1, keepdims=True)
    xc = x - mu
    var = jnp.mean(xc * xc, axis=-1, keepdims=True)
    return xc * lax.rsqrt(var + LN_EPS) * g + b


def _finish_body(x_ref, ml_ref, ma_ref, wo_ref, wfi_ref, wfo_ref, p_ref, out_ref,
                 *, alpha, d_lru, d_ff, fc):
    ln1_g, ln1_b, ln2_g, ln2_b = (p_ref[r:r + 1, :] for r in range(4))
    y = (jnp.dot(ml_ref[...], wo_ref[0:d_lru, :], preferred_element_type=F32)
         + jnp.dot(ma_ref[...], wo_ref[d_lru:, :], preferred_element_type=F32))
    x1 = _layer_norm(alpha * x_ref[...] + y, ln1_g, ln1_b)
    x1b = x1.astype(BF16)
    f = jnp.zeros(x1.shape, F32)
    for c in range(d_ff // fc):
        u = jnp.dot(x1b, wfi_ref[:, c * fc:(c + 1) * fc], preferred_element_type=F32)
        g = jnp.dot(x1b, wfi_ref[:, d_ff + c * fc:d_ff + (c + 1) * fc],
                    preferred_element_type=F32)
        hidden = (g * jax.nn.sigmoid(g) * u).astype(BF16)
        f = f + jnp.dot(hidden, wfo_ref[c * fc:(c + 1) * fc, :], preferred_element_type=F32)
    out_ref[...] = _layer_norm(alpha * x1 + f, ln2_g, ln2_b)


def _finish(x2d, mix_lru, mix_att, wo, wfi, wfo, pvec, *, alpha, tm):
    n, d_model = x2d.shape
    d_lru = mix_lru.shape[1]
    d_ff = wfo.shape[0]
    fc = 2 * LANES if d_ff % (2 * LANES) == 0 else LANES
    row = lambda w: pl.BlockSpec((tm, w), lambda i: (i, 0))
    const = lambda a: pl.BlockSpec(a.shape, lambda i: (0, 0), pipeline_mode=pl.Buffered(1))
    return pl.pallas_call(
        functools.partial(_finish_body, alpha=alpha, d_lru=d_lru, d_ff=d_ff, fc=fc),
        grid=(n // tm,),
        in_specs=[row(d_model), row(d_lru), row(mix_att.shape[1]),
                  const(wo), const(wfi), const(wfo), const(pvec)],
        out_specs=row(d_model),
        out_shape=jax.ShapeDtypeStruct((n, d_model), F32),
        compiler_params=_params("parallel"),
        name="finish",
    )(x2d, mix_lru, mix_att, wo, wfi, wfo, pvec)


def _block_diag(w):
    nb, bi, bo = w.shape
    eye = jnp.eye(nb, dtype=w.dtype)
    return (w[:, :, None, :] * eye[:, None, :, None]).reshape(nb * bi, nb * bo)


def _row_tile(n, want):
    tm = min(n, want)
    while n % tm:
        tm //= 2
    return tm


def kernel(x_prompt, x_sample, cache_k, cache_v, cache_kidx, state_conv, state_h, page_table,
           w_in, conv_w, conv_b, w_a, b_a, w_x, b_x, lam, gn_lru, gn_att, w_out,
           ln1_g, ln1_b, w_ffn_in, w_ffn_out, ln2_g, ln2_b):
    depth, d_model, d_in = w_in.shape
    bsz, seq, _ = x_prompt.shape
    dbs, dseq, _ = x_sample.shape
    d_lru = conv_w.shape[2]
    n_phys, page, n_heads, head_dim = cache_k.shape[1:]
    d_att = n_heads * head_dim
    idx_dim = cache_kidx.shape[3]
    d_qi = IDX_HEADS * idx_dim
    n_pages = page_table.shape[1]
    past_len = n_pages * page
    alpha = (2.0 * depth) ** 0.25
    assert d_in == 2 * d_lru + 3 * d_att + d_qi + idx_dim + IDX_HEADS
    assert idx_dim + IDX_HEADS <= LANES and LANES % head_dim == 0 and head_dim == idx_dim
    geom = dict(d_lru=d_lru, d_att=d_att, d_qi=d_qi, idx_dim=idx_dim, head_dim=head_dim)
    d_main = d_in - idx_dim - IDX_HEADS

    pos_s = jnp.tile(past_len + jnp.arange(dseq), dbs)
    topk_p = min(TOPK_MAX, seq // 4)
    topk_s = min(TOPK_MAX, (past_len + dseq) // 4)

    xp = x_prompt.reshape(bsz * seq, d_model)
    xs = x_sample.reshape(dbs * dseq, d_model)
    outs_p, outs_s = [], []
    for l in range(depth):
        w_pad = jnp.pad(w_in[l], ((0, 0), (0, d_main + LANES - d_in))).astype(BF16)
        wa_bd = _block_diag(w_a[l]).astype(BF16)
        wx_bd = _block_diag(w_x[l]).astype(BF16)
        lru_vec = jnp.stack([conv_b[l], b_a[l], b_x[l], lam[l], gn_lru[l]]
                            + [jnp.zeros_like(lam[l])] * 3)
        fin_vec = jnp.stack([ln1_g[l], ln1_b[l], ln2_g[l], ln2_b[l]])
        wo, wfi, wfo = (w_out[l].astype(BF16), w_ffn_in[l].astype(BF16),
                        w_ffn_out[l].astype(BF16))
        gn_a = gn_att[l][None, :]

        tq = _row_tile(seq, ATTN_QUERIES)
        xl, gate, qt, qit, kt, k_hm, vt, vt_chunks, kidx_t, kidx_b, wt = _project_cols(
            xp.reshape(bsz, seq, d_model), w_pad, tm=_row_tile(seq, PROJ_ROWS), cha=tq // 4,
            n_heads=n_heads, idx_heads=IDX_HEADS, **geom)
        xl3 = xl.reshape(bsz, seq, d_lru)
        mix_l, h_last = _lru_prompt(
            xl3, gate.reshape(bsz, seq, d_lru), jnp.zeros((bsz, CONV_W - 1, d_lru), F32),
            jnp.zeros((bsz, d_lru), F32), conv_w[l], lru_vec, wa_bd, wx_bd,
            tt=_row_tile(seq, LRU_ROWS))
        mix_a = _attn_prompt(qit, wt, qt, kidx_b, k_hm, vt_chunks, gn_a, tq=tq, topk=topk_p,
                             idx_dim=idx_dim)
        xp = _finish(xp, mix_l.reshape(bsz * seq, d_lru), mix_a.reshape(bsz * seq, d_att),
                     wo, wfi, wfo, fin_vec, alpha=alpha, tm=_row_tile(bsz * seq, FINISH_ROWS))
        token_major = lambda a: a.reshape(bsz, n_heads, head_dim, seq).transpose(0, 3, 1, 2)
        outs_p.append((token_major(kt), token_major(vt), jnp.swapaxes(kidx_t, 1, 2),
                       xl3[:, seq - (CONV_W - 1):], h_last[:, 0]))

        xl, gate, qb, k, kb, v, vb, qib, tail = _project_rows(
            xs, w_pad, pos_s, tm=_row_tile(dbs * dseq, PROJ_ROWS_SAMPLE), **geom)
        d3 = lambda a: a.reshape(dbs, dseq, a.shape[-1])
        tm_major = lambda a: jnp.swapaxes(d3(a), 0, 1)
        xl3 = d3(xl)
        mix_l, h_last = _lru_sample(
            tm_major(xl), tm_major(gate), jnp.swapaxes(state_conv[l], 0, 1).astype(F32),
            state_h[l], conv_w[l], lru_vec, wa_bd, wx_bd)
        tail3 = d3(tail)
        ki = tail3[:, :, :idx_dim]
        wi = tail3[:, :, idx_dim:idx_dim + IDX_HEADS]
        mix_a = _attn_sample(
            d3(qib), wi, d3(qb), ki.astype(BF16), d3(kb), d3(vb),
            cache_k[l], cache_v[l], cache_kidx[l], page_table, gn_a, topk=topk_s,
            n_heads=n_heads)
        xs = _finish(xs, jnp.swapaxes(mix_l, 0, 1).reshape(dbs * dseq, d_lru),
                     mix_a.reshape(dbs * dseq, d_att), wo, wfi, wfo, fin_vec, alpha=alpha,
                     tm=_row_tile(dbs * dseq, FINISH_ROWS_SAMPLE))
        conv_new = jnp.concatenate([state_conv[l].astype(F32), xl3], axis=1)[:, -(CONV_W - 1):]
        outs_s.append((k.reshape(dbs, dseq, n_heads, head_dim), v.reshape(dbs, dseq, n_heads, head_dim),
                       ki, conv_new, h_last))

    stack = lambda outs, j: jnp.stack([o[j] for o in outs])
    return (xp.reshape(bsz, seq, d_model), xs.reshape(dbs, dseq, d_model),
            *(stack(outs_p, j) for j in range(5)), *(stack(outs_s, j) for j in range(5)))
```

```python
import functools

import jax
import jax.numpy as jnp
import numpy as np
from jax import lax
from jax.experimental import pallas as pl
from jax.experimental.pallas import tpu as pltpu

CONV_W = 4
LRU_C = 8.0
IDX_HEADS = 8
TOPK_MAX = 256
ROPE_FRACTION = 4
ROPE_THETA = 500000.0
RMS_EPS = 1e-6
LN_EPS = 1e-5

LANES = 128
SUBLANES = 8
VMEM_LIMIT = 56 * 1024 * 1024
MASKED = -1e30
INT_MIN = -2 ** 31

PROJ_ROWS = 512
PROJ_ROWS_SAMPLE = 256
FINISH_ROWS = 512
FINISH_ROWS_SAMPLE = 256
LRU_ROWS = 512
ATTN_QUERIES = 512
SCORE_GROUP = 8
SELECT_ROWS = 128

F32 = jnp.float32
BF16 = jnp.bfloat16
I32 = jnp.int32


def _params(*sem):
    return pltpu.CompilerParams(dimension_semantics=sem, vmem_limit_bytes=VMEM_LIMIT)


def _nt_dot(a, b):
    return lax.dot_general(a, b, (((1,), (1,)), ((), ())), preferred_element_type=F32)


def _proj_streams(x_ref, w_ref, c_ref, s1_ref, s2_ref, *, d_lru, d_att, d_qi, idx_dim, rope_half,
                  q_scale):
    xb = x_ref[...].astype(BF16)
    c, s1, s2 = c_ref[...], s1_ref[...], s2_ref[...]

    def proj(lo, width):
        return jnp.dot(xb, w_ref[:, lo:lo + width], preferred_element_type=F32)

    def tiled(t, width):
        reps = width // LANES
        return t if reps == 1 else jnp.concatenate([t] * reps, axis=1)

    def rope(z, cc, ss1, ss2):
        width = z.shape[1]
        return (z * cc + pltpu.roll(z, width - rope_half, 1) * ss1
                + pltpu.roll(z, rope_half, 1) * ss2)

    lo = 0
    xl = proj(lo, d_lru); lo += d_lru
    gate = proj(lo, d_lru); lo += d_lru
    q = rope(proj(lo, d_att), tiled(c, d_att), tiled(s1, d_att), tiled(s2, d_att)); lo += d_att
    k = rope(proj(lo, d_att), tiled(c, d_att), tiled(s1, d_att), tiled(s2, d_att)); lo += d_att
    v = proj(lo, d_att); lo += d_att
    qi = rope(proj(lo, d_qi), tiled(c, d_qi), tiled(s1, d_qi), tiled(s2, d_qi)); lo += d_qi
    tail = proj(lo, LANES)
    is_key = lax.broadcasted_iota(I32, tail.shape, 1) < idx_dim
    tail = rope(tail, jnp.where(is_key, c, 1.0), jnp.where(is_key, s1, 0.0),
                jnp.where(is_key, s2, 0.0))
    return xl, gate, q * q_scale, k, v, qi, tail


def _proj_rows_body(x_ref, w_ref, c_ref, s1_ref, s2_ref,
                    xl_ref, gate_ref, qb_ref, k_ref, kb_ref, v_ref, vb_ref, qib_ref, tail_ref,
                    **geom):
    xl, gate, q, k, v, qi, tail = _proj_streams(x_ref, w_ref, c_ref, s1_ref, s2_ref, **geom)
    xl_ref[...] = xl
    gate_ref[...] = gate
    qb_ref[...] = q.astype(BF16)
    k_ref[...] = k
    kb_ref[...] = k.astype(BF16)
    v_ref[...] = v
    vb_ref[...] = v.astype(BF16)
    qib_ref[...] = qi.astype(BF16)
    tail_ref[...] = tail


def _proj_cols_body(x_ref, w_ref, c_ref, s1_ref, s2_ref,
                    xl_ref, gate_ref, qt_ref, qit_ref, kt_ref, khm_ref, vt_ref, vtc_ref,
                    kidxt_ref, kidx_ref, wt_ref, *, cha, n_heads, idx_heads, **geom):
    xl, gate, q, k, v, qi, tail = _proj_streams(x_ref, w_ref, c_ref, s1_ref, s2_ref, **geom)
    idx_dim = geom["idx_dim"]
    head_dim = k.shape[1] // n_heads
    xl_ref[...] = xl
    gate_ref[...] = gate
    qt_ref[0] = q.T.astype(BF16)
    qit_ref[0] = qi.T.astype(BF16)
    kt_ref[0] = k.T
    for h in range(n_heads):
        khm_ref[0, h] = k[:, h * head_dim:(h + 1) * head_dim].astype(BF16)
    vt = v.T
    vt_ref[0] = vt
    for j in range(vt.shape[1] // cha):
        vtc_ref[0, j] = vt[:, j * cha:(j + 1) * cha].astype(BF16)
    tail_t = tail.T
    kidxt_ref[0] = tail_t[:idx_dim]
    wt_ref[0] = tail_t[idx_dim:idx_dim + idx_heads]
    kidx_ref[0] = tail[:, :idx_dim].astype(BF16)


def _rope_tables(pos, head_dim):
    rope_dim = head_dim // ROPE_FRACTION
    half = rope_dim // 2
    freqs = ROPE_THETA ** (-jnp.arange(half, dtype=F32) / half)
    dim = jnp.arange(LANES) % head_dim
    ang = pos.astype(F32)[:, None] * freqs[dim % half][None, :]
    cos, sin = jnp.cos(ang), jnp.sin(ang)
    c = jnp.where(dim < rope_dim, cos, 1.0)
    s1 = jnp.where(dim < half, -sin, 0.0)
    s2 = jnp.where((dim >= half) & (dim < rope_dim), sin, 0.0)
    return [c, s1, s2], half


def _project_rows(x2d, w_pad, pos, *, d_lru, d_att, d_qi, idx_dim, head_dim, tm):
    n, d_model = x2d.shape
    (c, s1, s2), half = _rope_tables(pos, head_dim)
    row = lambda w: pl.BlockSpec((tm, w), lambda i: (i, 0))
    outs = [
        (d_lru, F32), (d_lru, F32), (d_att, BF16), (d_att, F32), (d_att, BF16),
        (d_att, F32), (d_att, BF16), (d_qi, BF16), (LANES, F32)]
    body = functools.partial(_proj_rows_body, d_lru=d_lru, d_att=d_att, d_qi=d_qi,
                             idx_dim=idx_dim, rope_half=half, q_scale=head_dim ** -0.5)
    return pl.pallas_call(
        body,
        grid=(n // tm,),
        in_specs=[row(d_model), pl.BlockSpec(w_pad.shape, lambda i: (0, 0)),
                  row(LANES), row(LANES), row(LANES)],
        out_specs=[row(w) for w, _ in outs],
        out_shape=[jax.ShapeDtypeStruct((n, w), dt) for w, dt in outs],
        compiler_params=_params("parallel"),
        name="proj",
    )(x2d, w_pad, c, s1, s2)


def _project_cols(x3d, w_pad, *, d_lru, d_att, d_qi, idx_dim, head_dim, tm, cha, n_heads,
                  idx_heads):
    bsz, t, d_model = x3d.shape
    n, nt = bsz * t, t // tm
    (c, s1, s2), half = _rope_tables(jnp.arange(t), head_dim)
    row = lambda w: pl.BlockSpec((tm, w), lambda i: (i, 0))
    table = pl.BlockSpec((tm, LANES), lambda i: (i % nt, 0))
    col = lambda w: pl.BlockSpec((1, w, tm), lambda i: (i // nt, 0, i % nt))
    col_shape = lambda w, dt: jax.ShapeDtypeStruct((bsz, w, t), dt)
    body = functools.partial(
        _proj_cols_body, d_lru=d_lru, d_att=d_att, d_qi=d_qi, idx_dim=idx_dim, rope_half=half,
        q_scale=head_dim ** -0.5, cha=cha, n_heads=n_heads, idx_heads=idx_heads)
    return pl.pallas_call(
        body,
        grid=(n // tm,),
        in_specs=[row(d_model), pl.BlockSpec(w_pad.shape, lambda i: (0, 0)), table, table, table],
        out_specs=[
            row(d_lru), row(d_lru), col(d_att), col(d_qi), col(d_att),
            pl.BlockSpec((1, n_heads, tm, head_dim), lambda i: (i // nt, 0, i % nt, 0)),
            col(d_att),
            pl.BlockSpec((1, tm // cha, d_att, cha), lambda i: (i // nt, i % nt, 0, 0)),
            col(idx_dim), pl.BlockSpec((1, tm, idx_dim), lambda i: (i // nt, i % nt, 0)),
            col(idx_heads)],
        out_shape=[
            jax.ShapeDtypeStruct((n, d_lru), F32), jax.ShapeDtypeStruct((n, d_lru), F32),
            col_shape(d_att, BF16), col_shape(d_qi, BF16), col_shape(d_att, F32),
            jax.ShapeDtypeStruct((bsz, n_heads, t, head_dim), BF16),
            col_shape(d_att, F32),
            jax.ShapeDtypeStruct((bsz, t // cha, d_att, cha), BF16),
            col_shape(idx_dim, F32), jax.ShapeDtypeStruct((bsz, t, idx_dim), BF16),
            col_shape(idx_heads, F32)],
        compiler_params=_params("parallel"),
        name="proj_prompt",
    )(x3d.reshape(n, d_model), w_pad, c, s1, s2)


def _softplus(x):
    return jnp.maximum(x, 0.0) + jnp.log1p(jnp.exp(-jnp.abs(x)))


def _gelu_tanh(x):
    return 0.5 * x * (1.0 + jnp.tanh(np.sqrt(2.0 / np.pi).astype(np.float32)
                                     * (x + 0.044715 * (x * x * x))))


def _lru_gates(xc, wa_ref, wx_ref, b_a, b_x, lam):
    xcb = xc.astype(BF16)
    r = jax.nn.sigmoid(jnp.dot(xcb, wa_ref[...], preferred_element_type=F32) + b_a)
    i = jax.nn.sigmoid(jnp.dot(xcb, wx_ref[...], preferred_element_type=F32) + b_x)
    log_a = -LRU_C * r * _softplus(-lam)
    a = jnp.exp(log_a)
    t = jnp.tanh(log_a)
    b = jnp.sqrt(-2.0 * t / (1.0 - t)) * (i * xc)
    return a, b


def _rms_gain(y, g):
    return y * lax.rsqrt(jnp.mean(y * y, axis=-1, keepdims=True) + RMS_EPS) * g


def _lru_prompt_body(xl_ref, gate_ref, cprev_ref, h0_ref, cw_ref, p_ref, wa_ref, wx_ref,
                     mix_ref, hlast_ref, ext_ref, hc_ref, *, tt):
    j = pl.program_id(1)

    @pl.when(j == 0)
    def _():
        ext_ref[0:SUBLANES, :] = cprev_ref[0]
        hc_ref[0:1, :] = h0_ref[0]

    xl = xl_ref[0]
    ext_ref[SUBLANES:SUBLANES + tt, :] = xl
    conv_b, b_a, b_x, lam, gn = (p_ref[r:r + 1, :] for r in range(5))
    xc = conv_b + (cw_ref[0:1, :] * ext_ref[SUBLANES - 3:SUBLANES - 3 + tt, :]
                   + cw_ref[1:2, :] * ext_ref[SUBLANES - 2:SUBLANES - 2 + tt, :]
                   + cw_ref[2:3, :] * ext_ref[SUBLANES - 1:SUBLANES - 1 + tt, :]
                   + cw_ref[3:4, :] * xl)
    ext_ref[0:SUBLANES, :] = ext_ref[tt:tt + SUBLANES, :]

    a, b = _lru_gates(xc, wa_ref, wx_ref, b_a, b_x, lam)
    groups = tt // SUBLANES
    a = a.reshape(groups, SUBLANES, a.shape[1])
    b = b.reshape(groups, SUBLANES, b.shape[1])
    row = lax.broadcasted_iota(I32, a.shape, 1)
    d = 1
    while d < SUBLANES:
        keep = row >= d
        a_prev = jnp.where(keep, pltpu.roll(a, d, 1), 1.0)
        b_prev = jnp.where(keep, pltpu.roll(b, d, 1), 0.0)
        b = a * b_prev + b
        a = a * a_prev
        d *= 2
    state = hc_ref[0:1, :]
    hs = []
    for g in range(groups):
        hs.append(a[g] * state + b[g])
        state = hs[-1][SUBLANES - 1:SUBLANES, :]
    h = jnp.concatenate(hs, axis=0)
    hc_ref[0:1, :] = state
    hlast_ref[0] = state
    mix_ref[0] = _rms_gain(h * _gelu_tanh(gate_ref[0]), gn).astype(BF16)


def _lru_prompt(xl, gate, conv_prev, h0, conv_w, pvec, wa_bd, wx_bd, *, tt):
    bsz, t, d = xl.shape
    cprev8 = jnp.concatenate(
        [jnp.zeros((bsz, SUBLANES - (CONV_W - 1), d), F32), conv_prev.astype(F32)], axis=1)
    const = lambda shape: pl.BlockSpec(shape, lambda b, j: (0,) * len(shape))
    return pl.pallas_call(
        functools.partial(_lru_prompt_body, tt=tt),
        grid=(bsz, t // tt),
        in_specs=[pl.BlockSpec((1, tt, d), lambda b, j: (b, j, 0)),
                  pl.BlockSpec((1, tt, d), lambda b, j: (b, j, 0)),
                  pl.BlockSpec((1, SUBLANES, d), lambda b, j: (b, 0, 0)),
                  pl.BlockSpec((1, 1, d), lambda b, j: (b, 0, 0)),
                  const(conv_w.shape), const(pvec.shape), const(wa_bd.shape), const(wx_bd.shape)],
        out_specs=[pl.BlockSpec((1, tt, d), lambda b, j: (b, j, 0)),
                   pl.BlockSpec((1, 1, d), lambda b, j: (b, 0, 0))],
        out_shape=[jax.ShapeDtypeStruct((bsz, t, d), BF16),
                   jax.ShapeDtypeStruct((bsz, 1, d), F32)],
        scratch_shapes=[pltpu.VMEM((tt + SUBLANES, d), F32), pltpu.VMEM((SUBLANES, d), F32)],
        compiler_params=_params("parallel", "arbitrary"),
        name="lru_prompt",
    )(xl, gate, cprev8, h0.astype(F32)[:, None, :], conv_w, pvec, wa_bd, wx_bd)


def _lru_sample_body(xl_ref, gate_ref, cprev_ref, h0_ref, cw_ref, p_ref, wa_ref, wx_ref,
                     mix_ref, hlast_ref, *, t_len):
    conv_b, b_a, b_x, lam, gn = (p_ref[r:r + 1, :] for r in range(5))
    xp = [cprev_ref[s] for s in range(CONV_W - 1)] + [xl_ref[s] for s in range(t_len)]
    h = h0_ref[...]
    for s in range(t_len):
        xc = conv_b + (cw_ref[0:1, :] * xp[s] + cw_ref[1:2, :] * xp[s + 1]
                       + cw_ref[2:3, :] * xp[s + 2] + cw_ref[3:4, :] * xp[s + 3])
        a, b = _lru_gates(xc, wa_ref, wx_ref, b_a, b_x, lam)
        h = a * h + b
        mix_ref[s] = _rms_gain(h * _gelu_tanh(gate_ref[s]), gn).astype(BF16)
    hlast_ref[...] = h


def _lru_sample(xl_t, gate_t, cprev_t, h0, conv_w, pvec, wa_bd, wx_bd):
    t_len, dbs, d = xl_t.shape
    return pl.pallas_call(
        functools.partial(_lru_sample_body, t_len=t_len),
        out_shape=[jax.ShapeDtypeStruct((t_len, dbs, d), BF16),
                   jax.ShapeDtypeStruct((dbs, d), F32)],
        compiler_params=pltpu.CompilerParams(vmem_limit_bytes=VMEM_LIMIT),
        name="lru_sample",
    )(xl_t, gate_t, cprev_t, h0.astype(F32), conv_w, pvec, wa_bd, wx_bd)


KEY_NEG_INF = INT_MIN + 0x7FFFFF
REFINE_STEPS = 8


def _key_to_float(key):
    return pltpu.bitcast(jnp.where(key >= 0, key, key ^ 0x7FFFFFFF), F32)


KEY16_NEG_INF = -2 ** 15 + 0x7F


def _key16_to_bf16(key):
    bits = lax.shift_left(jnp.where(key >= 0, key, key ^ 0x7FFF), 16)
    return pltpu.bitcast(bits, F32).astype(BF16)


def _threshold_search(count, cell_span, total, topk, theta_ref, need_ref, count_bf16=None):
    shape = theta_ref.shape

    if count_bf16 is None:
        def value_step(it, carry):
            base, n_base = carry
            trial = base ^ lax.shift_left(jnp.int32(1), jnp.int32(31) - it)
            trial_f = _key_to_float(trial)
            n = count(lambda s: s >= trial_f)
            ok = n >= topk
            return jnp.where(ok, trial, base), jnp.where(ok, n, n_base)

        theta_key, n_ge = lax.fori_loop(
            0, 32, value_step, (jnp.full(shape, INT_MIN, I32), jnp.zeros(shape, I32) + total))
    else:
        def coarse_step(it, base):
            trial = base + lax.shift_left(jnp.int32(1), jnp.int32(15) - it)
            return jnp.where(count_bf16(_key16_to_bf16(trial)) >= topk, trial, base)

        k16 = lax.fori_loop(0, 16, coarse_step, jnp.full(shape, -2 ** 15, I32))
        k16 = jnp.maximum(k16, KEY16_NEG_INF)
        key32 = lambda k: lax.shift_left(k, 16) + jnp.where(k < 0, 0xFFFF, 0)
        lo0 = jnp.maximum(key32(k16 - 2), KEY_NEG_INF)
        hi0 = jnp.maximum(key32(k16 + 1), lo0 + 1)

        def fine_step(_, carry):
            lo, hi = carry
            mid = lo + lax.shift_right_arithmetic(hi - lo, 1)
            mid_f = _key_to_float(mid)
            ok = count(lambda s: s >= mid_f) >= topk
            return jnp.where(ok, mid, lo), jnp.where(ok, hi, mid)

        theta_key, _ = lax.fori_loop(0, 18, fine_step, (lo0, hi0))
        theta_f = _key_to_float(theta_key)
        n_ge = count(lambda s: s >= theta_f)
    theta = jnp.where(theta_key < KEY_NEG_INF, -jnp.inf, _key_to_float(theta_key))
    theta_ref[...] = theta
    has_surplus = jnp.max(n_ge) > topk

    @pl.when(has_surplus)
    def _():
        above = _key_to_float(jnp.maximum(theta_key, KEY_NEG_INF) + 1)
        cell_min, cell_max = cell_span(theta, above)

        @pl.when(jnp.max(jnp.where(cell_max > cell_min, 1, 0)) > 0)
        def _():
            def refine_step(_, carry):
                lo, hi = carry
                mid = lo + 0.5 * (hi - lo)
                ok = count(lambda s: s >= mid) >= topk
                return jnp.where(ok, mid, lo), jnp.where(ok, hi, mid)

            theta_ref[...] = lax.fori_loop(0, REFINE_STEPS, refine_step, (theta, above))[0]

        theta_fine = theta_ref[...]
        need_ref[...] = (topk - count(lambda s: s > theta_fine)).astype(F32)

    return has_surplus


def _strict_triangle(n, lower):
    r = lax.broadcasted_iota(I32, (n, n), 0)
    c = lax.broadcasted_iota(I32, (n, n), 1)
    return jnp.where((c < r) if lower else (r < c), 1.0, 0.0).astype(BF16)


def _attn_prompt_body(qit_ref, wt_ref, qt_ref, kidx_ref, k_ref, vt_ref, gn_ref, out_ref,
                      sc_ref, sb_ref, y_ref, theta_ref, need_ref, m_ref, l_ref, s_ref, mx_ref, *,
                      tq, ch, cha, topk, n_heads, head_dim, idx_heads, idx_dim):
    i = pl.program_id(1)
    n_keys = (i + 1) * tq
    qpos = i * tq + lax.broadcasted_iota(I32, (1, tq), 1)
    sub = LANES
    kpos_sub = lax.broadcasted_iota(I32, (sub, tq), 0)
    kpos_ch = lax.broadcasted_iota(I32, (ch, tq), 0)
    nc = (i + 1) * (tq // ch)

    def score_chunk(c, carry):
        for j in range(ch // sub):
            start = pl.multiple_of(c * ch + j * sub, sub)
            rows = pl.ds(start, sub)
            kc = kidx_ref[0, rows, :]
            acc = jnp.zeros((sub, tq), F32)
            for h in range(idx_heads):
                d = jnp.dot(kc, qit_ref[0, h * idx_dim:(h + 1) * idx_dim, :],
                            preferred_element_type=F32)
                acc = acc + jnp.maximum(d, 0.0) * wt_ref[0, h:h + 1, :]
            score = jnp.where(start + kpos_sub <= qpos, acc, -jnp.inf)
            sc_ref[rows, :] = score
            sb_ref[rows, :] = score.astype(BF16)
        return carry

    n_pairs = lax.div(nc, 2)

    def over_chunks(body, init):
        carry = lax.fori_loop(0, n_pairs, lambda j, x: body(2 * j + 1, body(2 * j, x)), init)
        return lax.fori_loop(2 * n_pairs, nc, body, carry)

    over_chunks(score_chunk, 0)

    chunk_rows = lambda c: pl.ds(pl.multiple_of(c * ch, ch), ch)
    fold = lambda x, op: op(x.reshape(ch // SUBLANES, SUBLANES, tq), axis=0)

    def count_bf16(trial):
        packed = 2 * SUBLANES
        one, zero = jnp.ones((ch, tq), BF16), jnp.zeros((ch, tq), BF16)

        def body(c, acc):
            hit = jnp.where(sb_ref[chunk_rows(c), :] >= trial, one, zero)
            parts = hit.reshape(ch // packed, packed, tq)
            tot = parts[0]
            for r in range(1, ch // packed):
                tot = tot + parts[r]
            return acc + tot.astype(F32)
        acc = over_chunks(body, jnp.zeros((packed, tq), F32))
        return jnp.sum(acc, axis=0, keepdims=True).astype(I32)

    def count(pred):
        def body(c, acc):
            hit = jnp.where(pred(sc_ref[chunk_rows(c), :]), 1, 0).astype(I32)
            return acc + fold(hit, jnp.sum)
        acc = over_chunks(body, jnp.zeros((SUBLANES, tq), I32))
        return jnp.sum(acc, axis=0, keepdims=True)

    def cell_span(lo, hi):
        def body(c, carry):
            s = sc_ref[chunk_rows(c), :]
            inside = (s >= lo) & (s < hi)
            return (jnp.minimum(carry[0], fold(jnp.where(inside, s, jnp.inf), jnp.min)),
                    jnp.maximum(carry[1], fold(jnp.where(inside, s, -jnp.inf), jnp.max)))
        init = (jnp.full((SUBLANES, tq), jnp.inf, F32), jnp.full((SUBLANES, tq), -jnp.inf, F32))
        lo_acc, hi_acc = over_chunks(body, init)
        return (jnp.min(lo_acc, axis=0, keepdims=True), jnp.max(hi_acc, axis=0, keepdims=True))

    has_surplus = _threshold_search(count, cell_span, n_keys, topk, theta_ref, need_ref,
                                    count_bf16=count_bf16)
    theta = theta_ref[...]

    @pl.when(jnp.logical_not(has_surplus))
    def _():
        def bias_chunk(c, carry):
            rows = chunk_rows(c)
            sel = (sc_ref[rows, :] >= theta) & (c * ch + kpos_ch <= qpos)
            sc_ref[rows, :] = jnp.where(sel, 0.0, MASKED)
            return carry
        over_chunks(bias_chunk, 0)

    @pl.when(has_surplus)
    def _():
        need = need_ref[...]
        lower = _strict_triangle(ch, lower=True)

        def bias_chunk(c, seen):
            rows = chunk_rows(c)
            s = sc_ref[rows, :]
            tie = jnp.where(s == theta, 1.0, 0.0)
            rank = seen + jnp.dot(lower, tie.astype(BF16), preferred_element_type=F32)
            sel = ((s > theta) | ((s == theta) & (rank < need))) & (c * ch + kpos_ch <= qpos)
            sc_ref[rows, :] = jnp.where(sel, 0.0, MASKED)
            return seen + jnp.sum(fold(tie, jnp.sum), axis=0, keepdims=True)
        over_chunks(bias_chunk, jnp.zeros((1, tq), F32))

    heads = [(h, slice(h * head_dim, (h + 1) * head_dim)) for h in range(n_heads)]
    m_ref[...] = jnp.full(m_ref.shape, -jnp.inf, F32)
    l_ref[...] = jnp.zeros(l_ref.shape, F32)
    y_ref[...] = jnp.zeros(y_ref.shape, F32)

    nca = (i + 1) * (tq // cha)

    def logits(c, slot):
        rows = pl.ds(pl.multiple_of(c * cha, cha), cha)
        bias = sc_ref[rows, :]
        for h, hs in heads:
            s = jnp.dot(k_ref[0, h, rows, :], qt_ref[0, hs, :],
                        preferred_element_type=F32) + bias
            s_ref[slot, h] = s
            mx_ref[slot, h:h + 1, :] = jnp.max(s, axis=0, keepdims=True)

    def update(c, slot):
        for h, hs in heads:
            m = m_ref[h:h + 1, :]
            m_new = jnp.maximum(m, mx_ref[slot, h:h + 1, :])
            alpha = jnp.exp(m - m_new)
            p = jnp.exp(s_ref[slot, h] - m_new)
            m_ref[h:h + 1, :] = m_new
            l_ref[h:h + 1, :] = alpha * l_ref[h:h + 1, :] + jnp.sum(p, axis=0, keepdims=True)
            y_ref[hs, :] = alpha * y_ref[hs, :] + jnp.dot(
                vt_ref[0, c, hs, :], p.astype(BF16), preferred_element_type=F32)

    logits(0, 0)

    def chunk_pair(j, carry):
        c0 = 2 * j
        logits(c0 + 1, 1)
        update(c0, 0)
        logits(jnp.minimum(c0 + 2, nca - 1), 0)
        update(c0 + 1, 1)
        return carry

    lax.fori_loop(0, nca // 2, chunk_pair, 0)
    for h, hs in heads:
        y_ref[hs, :] = y_ref[hs, :] / l_ref[h:h + 1, :]
    out_ref[0] = _rms_gain(y_ref[...].T, gn_ref[...]).astype(BF16)


def _attn_prompt(qit, wt, qt, kidxb, k_hm, vt, gn_att, *, tq, topk, idx_dim):
    bsz, d_att, t = qt.shape
    n_heads, head_dim = k_hm.shape[1], k_hm.shape[3]
    ch = tq
    cha = vt.shape[3]
    assert (tq // cha) % 2 == 0
    cols = lambda a: pl.BlockSpec((1, a.shape[1], tq), lambda b, i: (b, 0, i))
    full = lambda a: pl.BlockSpec((1,) + a.shape[1:], lambda b, i: (b,) + (0,) * (a.ndim - 1))
    body = functools.partial(
        _attn_prompt_body, tq=tq, ch=ch, cha=cha, topk=topk, n_heads=n_heads,
        head_dim=head_dim, idx_heads=wt.shape[1], idx_dim=idx_dim)
    return pl.pallas_call(
        body,
        grid=(bsz, t // tq),
        in_specs=[cols(qit), cols(wt), cols(qt), full(kidxb), full(k_hm), full(vt),
                  pl.BlockSpec((1, d_att), lambda b, i: (0, 0))],
        out_specs=pl.BlockSpec((1, tq, d_att), lambda b, i: (b, i, 0)),
        out_shape=jax.ShapeDtypeStruct((bsz, t, d_att), BF16),
        scratch_shapes=[pltpu.VMEM((t, tq), F32), pltpu.VMEM((t, tq), BF16),
                        pltpu.VMEM((d_att, tq), F32),
                        pltpu.VMEM((1, tq), F32), pltpu.VMEM((1, tq), F32),
                        pltpu.VMEM((n_heads, tq), F32), pltpu.VMEM((n_heads, tq), F32),
                        pltpu.VMEM((2, n_heads, cha, tq), F32),
                        pltpu.VMEM((2, n_heads, tq), F32)],
        compiler_params=_params("parallel", "arbitrary"),
        name="attn_prompt",
    )(qit, wt, qt, kidxb, k_hm, vt, gn_att)


def _score_sample_body(pt_ref, qi_ref, w_ref, *refs, group, n_pages, page, t_len, idx_heads,
                       past_len):
    del pt_ref
    page_refs, new_ref, out_ref = refs[:group * n_pages], refs[-2], refs[-1]
    for g in range(group):
        qi = qi_ref[g]
        w = w_ref[g]
        pages = page_refs[g * n_pages:(g + 1) * n_pages]
        kt = jnp.concatenate([r[0].astype(BF16) for r in pages] + [new_ref[g]], axis=1)
        d = jnp.maximum(jnp.dot(qi, kt, preferred_element_type=F32), 0.0) * w
        s = jnp.sum(d.reshape(t_len, idx_heads, past_len + page), axis=1)
        tpos = past_len + lax.broadcasted_iota(I32, s.shape, 0)
        kpos = lax.broadcasted_iota(I32, s.shape, 1)
        out_ref[g] = jnp.where(kpos <= tpos, s, -jnp.inf)


def _select_sample_body(s_ref, bias_ref, sc_ref, theta_ref, need_ref, *, topk, page):
    nc, rows, _ = sc_ref.shape
    for c in range(nc):
        sc_ref[c] = s_ref[:, c * page:(c + 1) * page]

    def count(pred):
        def body(c, acc):
            return acc + jnp.where(pred(sc_ref[c]), 1, 0).astype(I32)
        acc = lax.fori_loop(0, nc, body, jnp.zeros((rows, page), I32))
        return jnp.sum(acc, axis=1, keepdims=True)

    def cell_span(lo, hi):
        def body(c, carry):
            s = sc_ref[c]
            inside = (s >= lo) & (s < hi)
            return (jnp.minimum(carry[0], jnp.where(inside, s, jnp.inf)),
                    jnp.maximum(carry[1], jnp.where(inside, s, -jnp.inf)))
        init = (jnp.full((rows, page), jnp.inf, F32), jnp.full((rows, page), -jnp.inf, F32))
        lo_acc, hi_acc = lax.fori_loop(0, nc, body, init)
        return (jnp.min(lo_acc, axis=1, keepdims=True), jnp.max(hi_acc, axis=1, keepdims=True))

    has_surplus = _threshold_search(count, cell_span, nc * page, topk, theta_ref, need_ref)
    theta = theta_ref[...]

    @pl.when(jnp.logical_not(has_surplus))
    def _():
        for c in range(nc):
            s = sc_ref[c]
            sel = (s >= theta) & (s > -jnp.inf)
            bias_ref[:, c * page:(c + 1) * page] = jnp.where(sel, 0.0, MASKED)

    @pl.when(has_surplus)
    def _():
        need = need_ref[...]
        upper = _strict_triangle(page, lower=False)
        seen = jnp.zeros((rows, 1), F32)
        for c in range(nc):
            s = sc_ref[c]
            tie = jnp.where(s == theta, 1.0, 0.0)
            rank = seen + jnp.dot(tie.astype(BF16), upper, preferred_element_type=F32)
            sel = ((s > theta) | ((s == theta) & (rank < need))) & (s > -jnp.inf)
            bias_ref[:, c * page:(c + 1) * page] = jnp.where(sel, 0.0, MASKED)
            seen = seen + jnp.sum(tie, axis=1, keepdims=True)


def _attn_sample_body(pt_ref, q_ref, bias_ref, gn_ref, *refs, n_pages, page, t_len, n_heads,
                      head_dim):
    del pt_ref
    k_refs, v_refs = refs[:n_pages], refs[n_pages:2 * n_pages]
    knew_ref, vnew_ref, out_ref = refs[2 * n_pages:]
    past = n_pages * page
    bias = bias_ref[0]
    outs = []
    del past
    logits = []
    for h in range(n_heads):
        kt = jnp.concatenate([r[0, h].astype(BF16) for r in k_refs] + [knew_ref[0, h]], axis=1)
        logits.append(jnp.dot(q_ref[0, h], kt, preferred_element_type=F32) + bias)
    for h in range(n_heads):
        s = logits[h]
        m = jnp.max(s, axis=1, keepdims=True)
        p = jnp.exp(s - m)
        l = jnp.sum(p, axis=1, keepdims=True)
        vt = jnp.concatenate([r[0, h].astype(BF16) for r in v_refs] + [vnew_ref[0, h]], axis=1)
        outs.append(_nt_dot(p.astype(BF16), vt) / l)
    y = jnp.concatenate(outs, axis=1)
    out_ref[0] = _rms_gain(y, gn_ref[...]).astype(BF16)


def _attn_sample(qib, wi, qb, kidx_new, k_new, v_new, cache_k, cache_v, cache_kidx, page_table,
                 gn_att, *, topk, n_heads):
    dbs, t_len, d_att = qb.shape
    n_pages = page_table.shape[1]
    page = cache_k.shape[1]
    idx_dim = cache_kidx.shape[2]
    idx_heads = wi.shape[2]
    head_dim = d_att // n_heads
    nkp = (n_pages + 1) * page
    past_len = n_pages * page
    pad = lambda a: jnp.pad(a, ((0, 0), (0, page - t_len), (0, 0)))
    kidx_t = jnp.transpose(cache_kidx, (0, 2, 1))
    k_t = jnp.transpose(cache_k, (0, 2, 3, 1))
    v_t = jnp.transpose(cache_v, (0, 2, 3, 1))
    new_t = lambda a: jnp.transpose(pad(a).reshape(dbs, page, n_heads, head_dim), (0, 2, 3, 1))

    def paged(shape):
        return [pl.BlockSpec((1,) + shape, functools.partial(
            lambda b, pt, p: (pt[b, p],) + (0,) * len(shape), p=p)) for p in range(n_pages)]

    per_seq = lambda shape: pl.BlockSpec((1,) + shape, lambda b, pt: (b,) + (0,) * len(shape))

    group = _row_tile(dbs, SCORE_GROUP)
    seq_group = lambda shape: pl.BlockSpec((group,) + shape,
                                           lambda b, pt: (b,) + (0,) * len(shape))
    group_pages = [pl.BlockSpec((1, idx_dim, page), functools.partial(
        lambda b, pt, g, p: (pt[b * group + g, p], 0, 0), g=g, p=p))
        for g in range(group) for p in range(n_pages)]
    scores = pl.pallas_call(
        functools.partial(_score_sample_body, group=group, n_pages=n_pages, page=page,
                          t_len=t_len, idx_heads=idx_heads, past_len=past_len),
        grid_spec=pltpu.PrefetchScalarGridSpec(
            num_scalar_prefetch=1, grid=(dbs // group,),
            in_specs=[seq_group((t_len * idx_heads, idx_dim)), seq_group((t_len * idx_heads, 1))]
                     + group_pages + [seq_group((idx_dim, page))],
            out_specs=seq_group((t_len, nkp))),
        out_shape=jax.ShapeDtypeStruct((dbs, t_len, nkp), F32),
        compiler_params=_params("parallel"),
        name="score_sample",
    )(page_table, qib.reshape(dbs, t_len * idx_heads, idx_dim),
      wi.reshape(dbs, t_len * idx_heads, 1), *([kidx_t] * (group * n_pages)),
      jnp.transpose(pad(kidx_new), (0, 2, 1)))

    rows = dbs * t_len
    tr = _row_tile(rows, SELECT_ROWS)
    bias = pl.pallas_call(
        functools.partial(_select_sample_body, topk=topk, page=page),
        grid=(rows // tr,),
        in_specs=[pl.BlockSpec((tr, nkp), lambda r: (r, 0))],
        out_specs=pl.BlockSpec((tr, nkp), lambda r: (r, 0)),
        out_shape=jax.ShapeDtypeStruct((rows, nkp), F32),
        scratch_shapes=[pltpu.VMEM((n_pages + 1, tr, page), F32), pltpu.VMEM((tr, 1), F32),
                        pltpu.VMEM((tr, 1), F32)],
        compiler_params=_params("parallel"),
        name="select_sample",
    )(scores.reshape(rows, nkp)).reshape(dbs, t_len, nkp)

    kv_pages = paged((n_heads, head_dim, page))
    return pl.pallas_call(
        functools.partial(_attn_sample_body, n_pages=n_pages, page=page, t_len=t_len,
                          n_heads=n_heads, head_dim=head_dim),
        grid_spec=pltpu.PrefetchScalarGridSpec(
            num_scalar_prefetch=1, grid=(dbs,),
            in_specs=[per_seq((n_heads, t_len, head_dim)), per_seq((t_len, nkp)),
                      pl.BlockSpec((1, d_att), lambda b, pt: (0, 0))]
                     + kv_pages + kv_pages
                     + [per_seq((n_heads, head_dim, page)), per_seq((n_heads, head_dim, page))],
            out_specs=per_seq((t_len, d_att))),
        out_shape=jax.ShapeDtypeStruct((dbs, t_len, d_att), BF16),
        compiler_params=_params("parallel"),
        name="attn_sample",
    )(page_table, qb.reshape(dbs, t_len, n_heads, head_dim).transpose(0, 2, 1, 3), bias, gn_att,
      *([k_t] * n_pages), *([v_t] * n_pages), new_t(k_new), new_t(v_new))


def _layer_norm(x, g, b):
    mu = jnp.mean(x, axis=-1, keepdims=True)
    xc = x - mu
    var = jnp.mean(xc * xc, axis=-1, keepdims=True)
    return xc * lax.rsqrt(var + LN_EPS) * g + b


def _finish_body(x_ref, ml_ref, ma_ref, wo_ref, wfi_ref, wfo_ref, p_ref, out_ref,
                 *, alpha, d_lru, d_ff, fc):
    ln1_g, ln1_b, ln2_g, ln2_b = (p_ref[r:r + 1, :] for r in range(4))
    y = (jnp.dot(ml_ref[...], wo_ref[0:d_lru, :], preferred_element_type=F32)
         + jnp.dot(ma_ref[...], wo_ref[d_lru:, :], preferred_element_type=F32))
    x1 = _layer_norm(alpha * x_ref[...] + y, ln1_g, ln1_b)
    x1b = x1.astype(BF16)
    f = jnp.zeros(x1.shape, F32)
    for c in range(d_ff // fc):
        u = jnp.dot(x1b, wfi_ref[:, c * fc:(c + 1) * fc], preferred_element_type=F32)
        g = jnp.dot(x1b, wfi_ref[:, d_ff + c * fc:d_ff + (c + 1) * fc],
                    preferred_element_type=F32)
        hidden = (g * jax.nn.sigmoid(g) * u).astype(BF16)
        f = f + jnp.dot(hidden, wfo_ref[c * fc:(c + 1) * fc, :], preferred_element_type=F32)
    out_ref[...] = _layer_norm(alpha * x1 + f, ln2_g, ln2_b)


def _finish(x2d, mix_lru, mix_att, wo, wfi, wfo, pvec, *, alpha, tm):
    n, d_model = x2d.shape
    d_lru = mix_lru.shape[1]
    d_ff = wfo.shape[0]
    fc = 2 * LANES if d_ff % (2 * LANES) == 0 else LANES
    row = lambda w: pl.BlockSpec((tm, w), lambda i: (i, 0))
    const = lambda a: pl.BlockSpec(a.shape, lambda i: (0, 0), pipeline_mode=pl.Buffered(1))
    return pl.pallas_call(
        functools.partial(_finish_body, alpha=alpha, d_lru=d_lru, d_ff=d_ff, fc=fc),
        grid=(n // tm,),
        in_specs=[row(d_model), row(d_lru), row(mix_att.shape[1]),
                  const(wo), const(wfi), const(wfo), const(pvec)],
        out_specs=row(d_model),
        out_shape=jax.ShapeDtypeStruct((n, d_model), F32),
        compiler_params=_params("parallel"),
        name="finish",
    )(x2d, mix_lru, mix_att, wo, wfi, wfo, pvec)


def _block_diag(w):
    nb, bi, bo = w.shape
    eye = jnp.eye(nb, dtype=w.dtype)
    return (w[:, :, None, :] * eye[:, None, :, None]).reshape(nb * bi, nb * bo)


def _row_tile(n, want):
    tm = min(n, want)
    while n % tm:
        tm //= 2
    return tm


def kernel(x_prompt, x_sample, cache_k, cache_v, cache_kidx, state_conv, state_h, page_table,
           w_in, conv_w, conv_b, w_a, b_a, w_x, b_x, lam, gn_lru, gn_att, w_out,
           ln1_g, ln1_b, w_ffn_in, w_ffn_out, ln2_g, ln2_b):
    depth, d_model, d_in = w_in.shape
    bsz, seq, _ = x_prompt.shape
    dbs, dseq, _ = x_sample.shape
    d_lru = conv_w.shape[2]
    n_phys, page, n_heads, head_dim = cache_k.shape[1:]
    d_att = n_heads * head_dim
    idx_dim = cache_kidx.shape[3]
    d_qi = IDX_HEADS * idx_dim
    n_pages = page_table.shape[1]
    past_len = n_pages * page
    alpha = (2.0 * depth) ** 0.25
    assert d_in == 2 * d_lru + 3 * d_att + d_qi + idx_dim + IDX_HEADS
    assert idx_dim + IDX_HEADS <= LANES and LANES % head_dim == 0 and head_dim == idx_dim
    geom = dict(d_lru=d_lru, d_att=d_att, d_qi=d_qi, idx_dim=idx_dim, head_dim=head_dim)
    d_main = d_in - idx_dim - IDX_HEADS

    pos_s = jnp.tile(past_len + jnp.arange(dseq), dbs)
    topk_p = min(TOPK_MAX, seq // 4)
    topk_s = min(TOPK_MAX, (past_len + dseq) // 4)

    xp = x_prompt.reshape(bsz * seq, d_model)
    xs = x_sample.reshape(dbs * dseq, d_model)
    outs_p, outs_s = [], []
    for l in range(depth):
        w_pad = jnp.pad(w_in[l], ((0, 0), (0, d_main + LANES - d_in))).astype(BF16)
        wa_bd = _block_diag(w_a[l]).astype(BF16)
        wx_bd = _block_diag(w_x[l]).astype(BF16)
        lru_vec = jnp.stack([conv_b[l], b_a[l], b_x[l], lam[l], gn_lru[l]]
                            + [jnp.zeros_like(lam[l])] * 3)
        fin_vec = jnp.stack([ln1_g[l], ln1_b[l], ln2_g[l], ln2_b[l]])
        wo, wfi, wfo = (w_out[l].astype(BF16), w_ffn_in[l].astype(BF16),
                        w_ffn_out[l].astype(BF16))
        gn_a = gn_att[l][None, :]

        tq = _row_tile(seq, ATTN_QUERIES)
        xl, gate, qt, qit, kt, k_hm, vt, vt_chunks, kidx_t, kidx_b, wt = _project_cols(
            xp.reshape(bsz, seq, d_model), w_pad, tm=_row_tile(seq, PROJ_ROWS), cha=tq // 4,
            n_heads=n_heads, idx_heads=IDX_HEADS, **geom)
        xl3 = xl.reshape(bsz, seq, d_lru)
        mix_l, h_last = _lru_prompt(
            xl3, gate.reshape(bsz, seq, d_lru), jnp.zeros((bsz, CONV_W - 1, d_lru), F32),
            jnp.zeros((bsz, d_lru), F32), conv_w[l], lru_vec, wa_bd, wx_bd,
            tt=_row_tile(seq, LRU_ROWS))
        mix_a = _attn_prompt(qit, wt, qt, kidx_b, k_hm, vt_chunks, gn_a, tq=tq, topk=topk_p,
                             idx_dim=idx_dim)
        xp = _finish(xp, mix_l.reshape(bsz * seq, d_lru), mix_a.reshape(bsz * seq, d_att),
                     wo, wfi, wfo, fin_vec, alpha=alpha, tm=_row_tile(bsz * seq, FINISH_ROWS))
        token_major = lambda a: a.reshape(bsz, n_heads, head_dim, seq).transpose(0, 3, 1, 2)
        outs_p.append((token_major(kt), token_major(vt), jnp.swapaxes(kidx_t, 1, 2),
                       xl3[:, seq - (CONV_W - 1):], h_last[:, 0]))

        xl, gate, qb, k, kb, v, vb, qib, tail = _project_rows(
            xs, w_pad, pos_s, tm=_row_tile(dbs * dseq, PROJ_ROWS_SAMPLE), **geom)
        d3 = lambda a: a.reshape(dbs, dseq, a.shape[-1])
        tm_major = lambda a: jnp.swapaxes(d3(a), 0, 1)
        xl3 = d3(xl)
        mix_l, h_last = _lru_sample(
            tm_major(xl), tm_major(gate), jnp.swapaxes(state_conv[l], 0, 1).astype(F32),
            state_h[l], conv_w[l], lru_vec, wa_bd, wx_bd)
        tail3 = d3(tail)
        ki = tail3[:, :, :idx_dim]
        wi = tail3[:, :, idx_dim:idx_dim + IDX_HEADS]
        mix_a = _attn_sample(
            d3(qib), wi, d3(qb), ki.astype(BF16), d3(kb), d3(vb),
            cache_k[l], cache_v[l], cache_kidx[l], page_table, gn_a, topk=topk_s,
            n_heads=n_heads)
        xs = _finish(xs, jnp.swapaxes(mix_l, 0, 1).reshape(dbs * dseq, d_lru),
                     mix_a.reshape(dbs * dseq, d_att), wo, wfi, wfo, fin_vec, alpha=alpha,
                     tm=_row_tile(dbs * dseq, FINISH_ROWS_SAMPLE))
        conv_new = jnp.concatenate([state_conv[l].astype(F32), xl3], axis=1)[:, -(CONV_W - 1):]
        outs_s.append((k.reshape(dbs, dseq, n_heads, head_dim), v.reshape(dbs, dseq, n_heads, head_dim),
                       ki, conv_new, h_last))

    stack = lambda outs, j: jnp.stack([o[j] for o in outs])
    return (xp.reshape(bsz, seq, d_model), xs.reshape(dbs, dseq, d_model),
            *(stack(outs_p, j) for j in range(5)), *(stack(outs_s, j) for j in range(5)))
```
